```python
import math, functools
import jax, jax.numpy as jnp
from jax import lax
import numpy as np

D_MODEL = 1024
BATCH = 2
SEQ = 8192
DEPTH = 4

N_MIXERS = 3
N_LAYERS_CONV = len(range(0, DEPTH, N_MIXERS))
N_LAYERS_FOX = len(range(1, DEPTH, N_MIXERS))
N_LAYERS_SSD = len(range(2, DEPTH, N_MIXERS))

RMS_EPS = 1e-6

D_FF = -(-8 * D_MODEL // (3 * 256)) * 256

CONV_WIDTH = 3

ATTN_HEAD_DIM = 64
ATTN_HEADS = D_MODEL // ATTN_HEAD_DIM
ATTN_WIDTH = ATTN_HEADS * ATTN_HEAD_DIM
Q_BLOCK = 128
FOX_IN = 3 * ATTN_WIDTH + ATTN_HEADS

SSM_EXPAND = 2
SSM_D_INNER = SSM_EXPAND * D_MODEL
SSM_HEAD_DIM = 64
SSM_HEADS = SSM_D_INNER // SSM_HEAD_DIM
SSM_GROUPS = 8
SSM_HEADS_PER_GROUP = SSM_HEADS // SSM_GROUPS
SSM_STATE = 128
SSM_CONV = 4
SSM_CHUNK = 128
SSM_CONV_DIM = SSM_D_INNER + 2 * SSM_GROUPS * SSM_STATE
SSM_IN = SSM_D_INNER + SSM_CONV_DIM + SSM_HEADS

kernel_name = "hybrid_conv_fox_ssd_trunk"


def rms_norm(x, w, eps=RMS_EPS):
    xf = x.astype(jnp.float32)
    y = xf * lax.rsqrt(jnp.mean(xf * xf, axis=-1, keepdims=True) + eps)
    return (y * w.astype(jnp.float32)).astype(x.dtype)


def causal_depthwise_conv(u, w):
    k_w = w.shape[0]
    seq = u.shape[1]
    up = jnp.pad(u, ((0, 0), (k_w - 1, 0), (0, 0)))
    out = up[:, 0:seq] * w[0]
    for k in range(1, k_w):
        out = out + up[:, k:k + seq] * w[k]
    return out


def swiglu_ffn(h, w_gu, w_down):
    g, u = jnp.split(h @ w_gu, 2, axis=-1)
    return (jax.nn.silu(g) * u) @ w_down


def short_conv_mixer(h, w_in, conv_w, w_out):
    b_gate, c_gate, v = jnp.split(h @ w_in, 3, axis=-1)
    u = causal_depthwise_conv(c_gate * v, conv_w)
    return (b_gate * u) @ w_out


def forgetting_attention(h, w_in, b_f, q_gain, k_gain, w_out):
    bsz, seq, _ = h.shape
    proj = h @ w_in
    q, k, v, f_logit = jnp.split(proj, [ATTN_WIDTH, 2 * ATTN_WIDTH, 3 * ATTN_WIDTH], axis=-1)
    q = rms_norm(q.reshape(bsz, seq, ATTN_HEADS, ATTN_HEAD_DIM), q_gain).astype(jnp.float32)
    k = rms_norm(k.reshape(bsz, seq, ATTN_HEADS, ATTN_HEAD_DIM), k_gain).astype(jnp.float32)
    v = v.reshape(bsz, seq, ATTN_HEADS, ATTN_HEAD_DIM).astype(jnp.float32)
    log_f = jax.nn.log_sigmoid((f_logit + b_f).astype(jnp.float32))
    cum = jnp.cumsum(log_f, axis=1)
    cum_k = jnp.transpose(cum, (0, 2, 1))[:, :, None, :]
    n_blk = seq // Q_BLOCK
    q_blk = jnp.moveaxis(q.reshape(bsz, n_blk, Q_BLOCK, ATTN_HEADS, ATTN_HEAD_DIM), 1, 0)
    cum_q = jnp.moveaxis(cum.reshape(bsz, n_blk, Q_BLOCK, ATTN_HEADS), 1, 0)
    key_pos = jnp.arange(seq)
    scale = ATTN_HEAD_DIM ** -0.5

    def attend(args):
        qb, cq, bi = args
        logits = jnp.einsum('bqhd,bkhd->bhqk', qb, k) * scale
        logits = logits + jnp.transpose(cq, (0, 2, 1))[..., None] - cum_k
        q_pos = bi * Q_BLOCK + jnp.arange(Q_BLOCK)
        causal = q_pos[:, None] >= key_pos[None, :]
        logits = jnp.where(causal, logits, -jnp.inf)
        p = jax.nn.softmax(logits, axis=-1)
        return jnp.einsum('bhqk,bkhd->bqhd', p, v)

    out = lax.map(attend, (q_blk, cum_q, jnp.arange(n_blk)))
    out = jnp.moveaxis(out, 0, 1).reshape(bsz, seq, ATTN_WIDTH).astype(h.dtype)
    return out @ w_out


def ssd_chunked(xs, dt, a, b_m, c_m):
    bsz, seq = xs.shape[:2]
    nc = seq // SSM_CHUNK
    L, G, K, P, N = SSM_CHUNK, SSM_GROUPS, SSM_HEADS_PER_GROUP, SSM_HEAD_DIM, SSM_STATE
    x = xs.reshape(bsz, nc, L, G, K, P)
    dtc = dt.reshape(bsz, nc, L, G, K)
    bc = b_m.reshape(bsz, nc, L, G, N)
    cc = c_m.reshape(bsz, nc, L, G, N)
    acum = jnp.cumsum(dtc * a.reshape(G, K), axis=2)
    seg = acum[:, :, :, None] - acum[:, :, None]
    mask = jnp.tril(jnp.ones((L, L), dtype=bool))[:, :, None, None]
    decay = jnp.exp(jnp.where(mask, seg, -jnp.inf))
    cb = jnp.einsum('bclgn,bcsgn->bclsg', cc, bc)
    w = cb[..., None] * decay * dtc[:, :, None]
    y_diag = jnp.einsum('bclsgk,bcsgkp->bclgkp', w, x)
    decay_states = jnp.exp(acum[:, :, -1:] - acum)
    states = jnp.einsum('bclgn,bclgk,bclgkp->bcgkpn', bc, decay_states * dtc, x)
    chunk_decay = jnp.exp(acum[:, :, -1])

    def step(hst, inp):
        st, dec = inp
        return dec[..., None, None] * hst + st, hst

    h0 = jnp.zeros((bsz, G, K, P, N), jnp.float32)
    _, prev = lax.scan(step, h0, (jnp.moveaxis(states, 1, 0), jnp.moveaxis(chunk_decay, 1, 0)))
    prev = jnp.moveaxis(prev, 0, 1)
    y_off = jnp.einsum('bclgn,bcgkpn,bclgk->bclgkp', cc, prev, jnp.exp(acum))
    return (y_diag + y_off).reshape(bsz, seq, SSM_HEADS, P)


def mamba2_mixer(h, w_in, conv_w, conv_b, dt_bias, a_log, d_skip, norm_w, w_out):
    bsz, seq, _ = h.shape
    proj = h @ w_in
    z, xbc, dt = jnp.split(proj, [SSM_D_INNER, SSM_D_INNER + SSM_CONV_DIM], axis=-1)
    xbc = jax.nn.silu(causal_depthwise_conv(xbc, conv_w) + conv_b)
    xs, b_m, c_m = jnp.split(xbc, [SSM_D_INNER, SSM_D_INNER + SSM_GROUPS * SSM_STATE], axis=-1)
    xs = xs.reshape(bsz, seq, SSM_HEADS, SSM_HEAD_DIM).astype(jnp.float32)
    b_m = b_m.reshape(bsz, seq, SSM_GROUPS, SSM_STATE).astype(jnp.float32)
    c_m = c_m.reshape(bsz, seq, SSM_GROUPS, SSM_STATE).astype(jnp.float32)
    dt = jax.nn.softplus((dt + dt_bias).astype(jnp.float32))
    a = -jnp.exp(a_log.astype(jnp.float32))
    y = ssd_chunked(xs, dt, a, b_m, c_m) + d_skip.astype(jnp.float32)[:, None] * xs
    y = y.reshape(bsz, seq, SSM_D_INNER) * jax.nn.silu(z.astype(jnp.float32))
    yg = y.reshape(bsz, seq, SSM_GROUPS, SSM_D_INNER // SSM_GROUPS)
    yg = yg * lax.rsqrt(jnp.mean(yg * yg, axis=-1, keepdims=True) + RMS_EPS)
    y = (yg.reshape(bsz, seq, SSM_D_INNER) * norm_w.astype(jnp.float32)).astype(h.dtype)
    return y @ w_out


def setup_inputs(seed: int = 0) -> dict:
    key = jax.random.key(seed)
    ks = iter(jax.random.split(key, 32))
    f32 = jnp.float32

    def normal(shape, scale):
        return jax.random.normal(next(ks), shape, f32) * scale

    def gains(shape):
        return 1.0 + normal(shape, 0.02)

    out_scale = (2.0 * DEPTH) ** -0.5
    nC, nF, nS = N_LAYERS_CONV, N_LAYERS_FOX, N_LAYERS_SSD
    dt0 = jnp.exp(jax.random.uniform(next(ks), (nS, SSM_HEADS), f32, math.log(1e-3), math.log(1e-1)))
    return {
        "x": normal((BATCH, SEQ, D_MODEL), 1.0),
        "mix_norm": gains((DEPTH, D_MODEL)),
        "ffn_norm": gains((DEPTH, D_MODEL)),
        "ffn_w_gu": normal((DEPTH, D_MODEL, 2 * D_FF), D_MODEL ** -0.5),
        "ffn_w_down": normal((DEPTH, D_FF, D_MODEL), D_FF ** -0.5 * out_scale),
        "conv_w_in": normal((nC, D_MODEL, 3 * D_MODEL), D_MODEL ** -0.5),
        "conv_w_dw": normal((nC, CONV_WIDTH, D_MODEL), CONV_WIDTH ** -0.5),
        "conv_w_out": normal((nC, D_MODEL, D_MODEL), D_MODEL ** -0.5 * out_scale),
        "fox_w_in": normal((nF, D_MODEL, FOX_IN), D_MODEL ** -0.5),
        "fox_b_f": 2.0 + normal((nF, ATTN_HEADS), 0.5),
        "fox_q_gain": gains((nF, ATTN_HEAD_DIM)),
        "fox_k_gain": gains((nF, ATTN_HEAD_DIM)),
        "fox_w_out": normal((nF, ATTN_WIDTH, D_MODEL), ATTN_WIDTH ** -0.5 * out_scale),
        "ssd_w_in": normal((nS, D_MODEL, SSM_IN), D_MODEL ** -0.5),
        "ssd_conv_w": normal((nS, SSM_CONV, SSM_CONV_DIM), SSM_CONV ** -0.5),
        "ssd_conv_b": normal((nS, SSM_CONV_DIM), 0.02),
        "ssd_dt_bias": dt0 + jnp.log(-jnp.expm1(-dt0)),
        "ssd_a_log": jnp.log(jax.random.uniform(next(ks), (nS, SSM_HEADS), f32, 1.0, 16.0)),
        "ssd_d": gains((nS, SSM_HEADS)),
        "ssd_norm_w": gains((nS, SSM_D_INNER)),
        "ssd_w_out": normal((nS, SSM_D_INNER, D_MODEL), SSM_D_INNER ** -0.5 * out_scale),
    }


def reference(x, mix_norm, ffn_norm, ffn_w_gu, ffn_w_down,
              conv_w_in, conv_w_dw, conv_w_out,
              fox_w_in, fox_b_f, fox_q_gain, fox_k_gain, fox_w_out,
              ssd_w_in, ssd_conv_w, ssd_conv_b, ssd_dt_bias, ssd_a_log, ssd_d, ssd_norm_w, ssd_w_out):
    for i in range(DEPTH):
        kind, j = i % N_MIXERS, i // N_MIXERS
        h = rms_norm(x, mix_norm[i])
        if kind == 0:
            m = short_conv_mixer(h, conv_w_in[j], conv_w_dw[j], conv_w_out[j])
        elif kind == 1:
            m = forgetting_attention(h, fox_w_in[j], fox_b_f[j], fox_q_gain[j], fox_k_gain[j], fox_w_out[j])
        else:
            m = mamba2_mixer(h, ssd_w_in[j], ssd_conv_w[j], ssd_conv_b[j], ssd_dt_bias[j],
                             ssd_a_log[j], ssd_d[j], ssd_norm_w[j], ssd_w_out[j])
        x = x + m
        x = x + swiglu_ffn(rms_norm(x, ffn_norm[i]), ffn_w_gu[i], ffn_w_down[i])
    return x
```

```python
import functools

import jax
import jax.numpy as jnp
from jax import lax
from jax.experimental import pallas as pl
from jax.experimental.pallas import tpu as pltpu

F32 = jnp.float32
BF16 = jnp.bfloat16

RMS_EPS = 1e-6
ATTN_HEAD_DIM = 64
SSM_HEAD_DIM = 64
SSM_GROUPS = 8
SSM_STATE = 128
SSM_CHUNK = 128
LANES = 128
NEG_BIG = -1e30

VMEM_LIMIT = 56 * 1024 * 1024


def _params(sem):
    return pltpu.CompilerParams(dimension_semantics=sem, vmem_limit_bytes=VMEM_LIMIT)


def _const_spec(shape):
    nd = len(shape)
    return pl.BlockSpec(shape, lambda *_: (0,) * nd)


def _rmsnorm(x, w):
    return x * lax.rsqrt(jnp.mean(x * x, axis=-1, keepdims=True) + RMS_EPS) * w


def _dot(a, b):
    return jnp.dot(a, b, preferred_element_type=F32)


def _softplus(x):
    return jnp.maximum(x, 0.0) + jnp.log1p(jnp.exp(-jnp.abs(x)))


def _silu(x):
    return x * jax.nn.sigmoid(x)


def _lane_cumsum(x):
    n = x.shape[-1]
    lane = lax.broadcasted_iota(jnp.int32, x.shape, x.ndim - 1)
    s = 1
    while s < n:
        x = x + jnp.where(lane >= s, pltpu.roll(x, s, x.ndim - 1), 0.0)
        s *= 2
    return x


def _ffn_body(x_ref, nw_ref, wg_ref, wu_ref, wd_ref, o_ref, a_ref, *, chunk):
    x = x_ref[...]
    h = _rmsnorm(x, nw_ref[...]).astype(BF16)
    for c in range(wg_ref.shape[1] // chunk):
        sl = slice(c * chunk, (c + 1) * chunk)
        g = _dot(h, wg_ref[:, sl])
        u = _dot(h, wu_ref[:, sl])
        a_ref[:, sl] = (_silu(g) * u).astype(BF16)
    o_ref[...] = x + _dot(a_ref[...], wd_ref[...])


def _ffn(x, nw, wg, wu, wd, *, tm=512, chunk=256):
    m, d = x.shape
    dff = wg.shape[1]
    return pl.pallas_call(
        functools.partial(_ffn_body, chunk=chunk),
        grid=(m // tm,),
        in_specs=[
            pl.BlockSpec((tm, d), lambda i: (i, 0)),
            _const_spec((1, d)),
            _const_spec((d, dff)),
            _const_spec((d, dff)),
            _const_spec((dff, d)),
        ],
        out_specs=pl.BlockSpec((tm, d), lambda i: (i, 0)),
        out_shape=jax.ShapeDtypeStruct((m, d), F32),
        scratch_shapes=[pltpu.VMEM((tm, dff), BF16)],
        compiler_params=_params(("arbitrary",)),
        name="ffn",
    )(x, nw, wg, wu, wd)


def _conv_mixer_body(x_ref, nw_ref, win_ref, wdw_ref, wout_ref, o_ref, ext_ref, *, tm, halo):
    i = pl.program_id(1)
    d = x_ref.shape[-1]
    x = x_ref[...]
    h = _rmsnorm(x, nw_ref[...]).astype(BF16)
    proj = _dot(h, win_ref[...])
    bg = proj[:, :d]
    cv = proj[:, d:2 * d] * proj[:, 2 * d:]

    @pl.when(i == 0)
    def _():
        ext_ref[0:halo, :] = jnp.zeros((halo, d), F32)

    @pl.when(i > 0)
    def _():
        ext_ref[0:halo, :] = ext_ref[tm:tm + halo, :]

    ext_ref[halo:halo + tm, :] = cv
    kw = wdw_ref.shape[0]
    u = cv * wdw_ref[kw - 1:kw, :]
    for k in range(kw - 1):
        off = halo - (kw - 1) + k
        u = u + ext_ref[off:off + tm, :] * wdw_ref[k:k + 1, :]
    y = (bg * u).astype(BF16)
    o_ref[...] = x + _dot(y, wout_ref[...])


def _conv_mixer(x, nw, win, wdw, wout, *, tm=512):
    b, s, d = x.shape
    halo = 8
    return pl.pallas_call(
        functools.partial(_conv_mixer_body, tm=tm, halo=halo),
        grid=(b, s // tm),
        in_specs=[
            pl.BlockSpec((None, tm, d), lambda bi, i: (bi, i, 0)),
            _const_spec((1, d)),
            _const_spec(win.shape),
            _const_spec(wdw.shape),
            _const_spec(wout.shape),
        ],
        out_specs=pl.BlockSpec((None, tm, d), lambda bi, i: (bi, i, 0)),
        out_shape=jax.ShapeDtypeStruct((b, s, d), F32),
        scratch_shapes=[pltpu.VMEM((halo + tm, d), F32)],
        compiler_params=_params(("arbitrary", "arbitrary")),
        name="conv_mixer",
    )(x, nw, win, wdw, wout)


def _fox_in_body(x_ref, nw_ref, wqkv_ref, wf_ref, bf_ref, qg_ref, kg_ref, e_ref, et_ref,
                 q_ref, k_ref, v_ref, ct_ref, carry_ref, *, n_heads):
    i = pl.program_id(1)
    aw = q_ref.shape[-1]
    x = x_ref[...]
    h = _rmsnorm(x, nw_ref[...]).astype(BF16)
    qkv = _dot(h, wqkv_ref[...])

    def head_norm(t, gain):
        ms = _dot((t * t).astype(BF16), e_ref[...])
        r = lax.rsqrt(ms + RMS_EPS)
        r_hi = r.astype(BF16)
        r_lo = (r - r_hi.astype(F32)).astype(BF16)
        rb = _dot(r_hi, et_ref[...]) + _dot(r_lo, et_ref[...])
        return t * rb * gain

    q_ref[...] = head_norm(qkv[:, :aw], qg_ref[...]).astype(BF16)
    k_ref[...] = head_norm(qkv[:, aw:2 * aw], kg_ref[...]).astype(BF16)
    v_ref[...] = qkv[:, 2 * aw:].astype(BF16)

    fl = _dot(h, wf_ref[...]) + bf_ref[...]
    logf = -_softplus(-fl)
    local = _lane_cumsum(logf.T[0:n_heads, :])

    @pl.when(i == 0)
    def _():
        carry_ref[...] = jnp.zeros_like(carry_ref)

    cum = local + carry_ref[:, 0:1]
    ct_ref[...] = cum
    tm = cum.shape[-1]
    carry_ref[...] = jnp.broadcast_to(cum[:, tm - 1:tm], carry_ref.shape)


def _fox_in(x, nw, wqkv, wf, bf, qg, kg, e, et, *, n_heads, tm=512):
    b, s, d = x.shape
    aw = wqkv.shape[1] // 3
    row = lambda bi, i: (bi, i, 0)
    return pl.pallas_call(
        functools.partial(_fox_in_body, n_heads=n_heads),
        grid=(b, s // tm),
        in_specs=[
            pl.BlockSpec((None, tm, d), row),
            _const_spec((1, d)),
            _const_spec(wqkv.shape),
            _const_spec(wf.shape),
            _const_spec(bf.shape),
            _const_spec(qg.shape),
            _const_spec(kg.shape),
            _const_spec(e.shape),
            _const_spec(et.shape),
        ],
        out_specs=[
            pl.BlockSpec((None, tm, aw), row),
            pl.BlockSpec((None, tm, aw), row),
            pl.BlockSpec((None, tm, aw), row),
            pl.BlockSpec((None, n_heads, tm), lambda bi, i: (bi, 0, i)),
        ],
        out_shape=[
            jax.ShapeDtypeStruct((b, s, aw), BF16),
            jax.ShapeDtypeStruct((b, s, aw), BF16),
            jax.ShapeDtypeStruct((b, s, aw), BF16),
            jax.ShapeDtypeStruct((b, n_heads, s), F32),
        ],
        scratch_shapes=[pltpu.VMEM((n_heads, LANES), F32)],
        compiler_params=_params(("arbitrary", "arbitrary")),
        name="fox_in",
    )(x, nw, wqkv, wf, bf, qg, kg, e, et)


def _fox_attn_body(q_ref, k_ref, v_ref, ck_ref, o_ref, *, t):
    i = pl.program_id(2)
    hd = ATTN_HEAD_DIM
    q2 = q_ref[...]
    lane = lax.broadcasted_iota(jnp.int32, (t, 2 * hd), 1)
    lo = lane < hd
    zero = jnp.zeros_like(q2)
    qs = (jnp.where(lo, q2, zero), jnp.where(lo, zero, q2))
    row = lax.broadcasted_iota(jnp.int32, (t, t), 0)
    col = lax.broadcasted_iota(jnp.int32, (t, t), 1)
    causal = row >= col

    def block(j, carry, masked):
        ms, ls, acc = carry
        start = pl.multiple_of(j * t, t)
        k2 = k_ref[pl.ds(start, t), :]
        v2 = v_ref[pl.ds(start, t), :]
        ck = ck_ref[j]
        new_m, new_l, alphas, pvs = [], [], [], []
        for hh in range(2):
            s = lax.dot_general(qs[hh], k2, (((1,), (1,)), ((), ())),
                                preferred_element_type=F32) - ck[hh:hh + 1, :]
            if masked:
                s = jnp.where(causal, s, NEG_BIG)
            m_new = jnp.maximum(ms[hh], jnp.max(s, axis=1, keepdims=True))
            alpha = jnp.exp(ms[hh] - m_new)
            p = jnp.exp(s - m_new)
            new_l.append(alpha * ls[hh] + jnp.sum(p, axis=1, keepdims=True))
            new_m.append(m_new)
            alphas.append(alpha)
            pvs.append(_dot(p.astype(BF16), v2))
        acc = acc * jnp.where(lo, alphas[0], alphas[1]) + jnp.where(lo, pvs[0], pvs[1])
        return tuple(new_m), tuple(new_l), acc

    m0 = jnp.full((t, 1), NEG_BIG, F32)
    l0 = jnp.zeros((t, 1), F32)
    carry = ((m0, m0), (l0, l0), jnp.zeros((t, 2 * hd), F32))
    carry = lax.fori_loop(0, i, lambda j, c: block(j, c, False), carry)
    _, ls, acc = block(i, carry, True)
    o_ref[...] = (acc / jnp.where(lo, ls[0], ls[1])).astype(o_ref.dtype)


def _fox_attn(q, k, v, ck, *, t):
    b, s, aw = q.shape
    pairs = aw // LANES
    return pl.pallas_call(
        functools.partial(_fox_attn_body, t=t),
        grid=(b, pairs, s // t),
        in_specs=[
            pl.BlockSpec((None, t, LANES), lambda bi, p, i: (bi, i, p)),
            pl.BlockSpec((None, s, LANES), lambda bi, p, i: (bi, 0, p)),
            pl.BlockSpec((None, s, LANES), lambda bi, p, i: (bi, 0, p)),
            pl.BlockSpec((None, None, s // t, 2, t), lambda bi, p, i: (bi, p, 0, 0, 0)),
        ],
        out_specs=pl.BlockSpec((None, t, LANES), lambda bi, p, i: (bi, i, p)),
        out_shape=jax.ShapeDtypeStruct((b, s, aw), BF16),
        compiler_params=_params(("arbitrary", "arbitrary", "arbitrary")),
        name="fox_attn",
    )(q, k, v, ck)


def _proj_res_body(y_ref, w_ref, x_ref, o_ref):
    o_ref[...] = x_ref[...] + _dot(y_ref[...], w_ref[...])


def _proj_res(y, w, x, *, tm=512):
    m, kdim = y.shape
    d = w.shape[1]
    return pl.pallas_call(
        _proj_res_body,
        grid=(m // tm,),
        in_specs=[
            pl.BlockSpec((tm, kdim), lambda i: (i, 0)),
            _const_spec(w.shape),
            pl.BlockSpec((tm, d), lambda i: (i, 0)),
        ],
        out_specs=pl.BlockSpec((tm, d), lambda i: (i, 0)),
        out_shape=jax.ShapeDtypeStruct((m, d), F32),
        compiler_params=_params(("arbitrary",)),
        name="proj_res",
    )(y, w, x)


def _ssd_in_body(x_ref, nw_ref, wz_ref, wxbc_ref, wdt_ref, cw_ref, cb_ref, dtb_ref,
                 z_ref, xs_ref, b_ref, c_ref, dt_ref, ext_ref, *, tm, halo):
    i = pl.program_id(1)
    di = xs_ref.shape[-1]
    gn = b_ref.shape[-1]
    cdim = ext_ref.shape[-1]
    x = x_ref[...]
    h = _rmsnorm(x, nw_ref[...]).astype(BF16)
    z_ref[...] = _dot(h, wz_ref[...])
    dt_ref[...] = _softplus(_dot(h, wdt_ref[...]) + dtb_ref[...])
    raw = _dot(h, wxbc_ref[...])

    @pl.when(i == 0)
    def _():
        ext_ref[0:halo, :] = jnp.zeros((halo, cdim), F32)

    @pl.when(i > 0)
    def _():
        ext_ref[0:halo, :] = ext_ref[tm:tm + halo, :]

    ext_ref[halo:halo + tm, :] = raw
    kw = cw_ref.shape[0]
    u = raw * cw_ref[kw - 1:kw, :] + cb_ref[...]
    for k in range(kw - 1):
        off = halo - (kw - 1) + k
        u = u + ext_ref[off:off + tm, :] * cw_ref[k:k + 1, :]
    u = _silu(u)
    xs_ref[...] = u[:, :di]
    b_ref[...] = u[:, di:di + gn]
    c_ref[...] = u[:, di + gn:]


def _ssd_in(x, nw, wz, wxbc, wdt, cw, cb, dtb, *, tm=256):
    b, s, d = x.shape
    di = wz.shape[1]
    cdim = wxbc.shape[1]
    gn = (cdim - di) // 2
    halo = 8
    row = lambda bi, i: (bi, i, 0)
    return pl.pallas_call(
        functools.partial(_ssd_in_body, tm=tm, halo=halo),
        grid=(b, s // tm),
        in_specs=[
            pl.BlockSpec((None, tm, d), row),
            _const_spec((1, d)),
            _const_spec(wz.shape),
            _const_spec(wxbc.shape),
            _const_spec(wdt.shape),
            _const_spec(cw.shape),
            _const_spec(cb.shape),
            _const_spec(dtb.shape),
        ],
        out_specs=[
            pl.BlockSpec((None, tm, di), row),
            pl.BlockSpec((None, tm, di), row),
            pl.BlockSpec((None, tm, gn), row),
            pl.BlockSpec((None, tm, gn), row),
            pl.BlockSpec((None, tm, LANES), row),
        ],
        out_shape=[
            jax.ShapeDtypeStruct((b, s, di), F32),
            jax.ShapeDtypeStruct((b, s, di), F32),
            jax.ShapeDtypeStruct((b, s, gn), F32),
            jax.ShapeDtypeStruct((b, s, gn), F32),
            jax.ShapeDtypeStruct((b, s, LANES), F32),
        ],
        scratch_shapes=[pltpu.VMEM((halo + tm, cdim), F32)],
        compiler_params=_params(("arbitrary", "arbitrary")),
        name="ssd_in",
    )(x, nw, wz, wxbc, wdt, cw, cb, dtb)


def _ssd_scan_body(xs_ref, b_ref, c_ref, dt_ref, alog_ref, dskip_ref, eh_ref, y_ref, st_ref,
                   *, n_heads):
    ci = pl.program_id(1)
    L = xs_ref.shape[0]
    P, N, G = SSM_HEAD_DIM, SSM_STATE, SSM_GROUPS
    hpg = n_heads // G
    gw = hpg * P

    @pl.when(ci == 0)
    def _():
        st_ref[...] = jnp.zeros_like(st_ref)

    lane_h = lax.broadcasted_iota(jnp.int32, (1, LANES), 1)
    a = jnp.where(lane_h < n_heads, -jnp.exp(alog_ref[...]), 0.0)
    dt = dt_ref[...]
    dtT = dt.T
    acumT = _lane_cumsum((dt * a).T)
    acum = acumT.T
    a_last = acum[L - 1:L, :]

    sdt = jnp.exp(a_last - acum) * dt
    ea = jnp.exp(acum)

    def expand(t):
        hi = t.astype(BF16)
        lo_ = (t - hi.astype(F32)).astype(BF16)
        return _dot(hi, eh_ref[...]) + _dot(lo_, eh_ref[...])

    sdt_b = expand(sdt)
    ea_b = expand(ea)
    xs = xs_ref[...]
    xw = (xs * sdt_b).astype(BF16)
    xs16 = xs.astype(BF16)
    decay_last = ea_b[L - 1:L, :]

    row = lax.broadcasted_iota(jnp.int32, (L, L), 0)
    col = lax.broadcasted_iota(jnp.int32, (L, L), 1)
    tril = row >= col
    lane2 = lax.broadcasted_iota(jnp.int32, (L, 2 * P), 1)
    lo = lane2 < P

    for g in range(G):
        bg = b_ref[:, g * N:(g + 1) * N].astype(BF16)
        cg = c_ref[:, g * N:(g + 1) * N].astype(BF16)
        cb = lax.dot_general(cg, bg, (((1,), (1,)), ((), ())), preferred_element_type=F32)
        gs = slice(g * gw, (g + 1) * gw)
        st = st_ref[g]
        y_off = _dot(cg, st.astype(BF16)) * ea_b[:, gs]
        contrib = lax.dot_general(bg, xw[:, gs], (((0,), (0,)), ((), ())),
                                  preferred_element_type=F32)
        st_ref[g] = st * decay_last[:, gs] + contrib
        for pr in range(hpg // 2):
            ws = []
            for hh in range(2):
                hidx = g * hpg + 2 * pr + hh
                seg = acum[:, hidx:hidx + 1] - acumT[hidx:hidx + 1, :]
                decay = jnp.exp(jnp.where(tril, seg, NEG_BIG))
                ws.append((cb * decay * dtT[hidx:hidx + 1, :]).astype(BF16))
            ps = slice(g * gw + pr * 2 * P, g * gw + (pr + 1) * 2 * P)
            x2 = xs16[:, ps]
            y_diag = jnp.where(lo, _dot(ws[0], x2), _dot(ws[1], x2))
            y_ref[:, ps] = y_diag + y_off[:, pr * 2 * P:(pr + 1) * 2 * P] + dskip_ref[:, ps] * xs[:, ps]


def _ssd_scan(xs, bm, cm, dt, alog, dskip, eh, *, n_heads):
    b, s, di = xs.shape
    gn = bm.shape[-1]
    L = SSM_CHUNK
    row = lambda bi, i: (bi, i, 0)
    return pl.pallas_call(
        functools.partial(_ssd_scan_body, n_heads=n_heads),
        grid=(b, s // L),
        in_specs=[
            pl.BlockSpec((None, L, di), row),
            pl.BlockSpec((None, L, gn), row),
            pl.BlockSpec((None, L, gn), row),
            pl.BlockSpec((None, L, LANES), row),
            _const_spec(alog.shape),
            _const_spec(dskip.shape),
            _const_spec(eh.shape),
        ],
        out_specs=pl.BlockSpec((None, L, di), row),
        out_shape=jax.ShapeDtypeStruct((b, s, di), F32),
        scratch_shapes=[pltpu.VMEM((SSM_GROUPS, SSM_STATE, di // SSM_GROUPS), F32)],
        compiler_params=_params(("arbitrary", "arbitrary")),
        name="ssd_scan",
    )(xs, bm, cm, dt, alog, dskip, eh)


def _ssd_out_body(y_ref, z_ref, nw_ref, w_ref, x_ref, o_ref, yn_ref):
    di = y_ref.shape[-1]
    gw = di // SSM_GROUPS
    for g in range(SSM_GROUPS):
        gs = slice(g * gw, (g + 1) * gw)
        yg = y_ref[:, gs] * _silu(z_ref[:, gs])
        yn_ref[:, gs] = _rmsnorm(yg, nw_ref[:, gs]).astype(BF16)
    o_ref[...] = x_ref[...] + _dot(yn_ref[...], w_ref[...])


def _ssd_out(y, z, nw, w, x, *, tm=512):
    m, di = y.shape
    d = w.shape[1]
    return pl.pallas_call(
        _ssd_out_body,
        grid=(m // tm,),
        in_specs=[
            pl.BlockSpec((tm, di), lambda i: (i, 0)),
            pl.BlockSpec((tm, di), lambda i: (i, 0)),
            _const_spec(nw.shape),
            _const_spec(w.shape),
            pl.BlockSpec((tm, d), lambda i: (i, 0)),
        ],
        out_specs=pl.BlockSpec((tm, d), lambda i: (i, 0)),
        out_shape=jax.ShapeDtypeStruct((m, d), F32),
        scratch_shapes=[pltpu.VMEM((tm, di), BF16)],
        compiler_params=_params(("arbitrary",)),
        name="ssd_out",
    )(y, z, nw, w, x)


def _pad_lanes(a, width=LANES):
    return jnp.pad(a, ((0, 0), (0, width - a.shape[-1])))


def _conv_layer(x, nw, w_in, w_dw, w_out):
    return _conv_mixer(x, nw[None], w_in.astype(BF16), w_dw, w_out.astype(BF16))


def _fox_layer(x, nw, w_in, b_f, q_gain, k_gain, w_out, *, t=256):
    b, s, d = x.shape
    n_heads = b_f.shape[0]
    hd = ATTN_HEAD_DIM
    aw = n_heads * hd
    wqkv = w_in[:, :3 * aw].astype(BF16)
    wf = _pad_lanes(w_in[:, 3 * aw:]).astype(BF16)
    bf = _pad_lanes(b_f[None])
    qg = jnp.tile(q_gain, n_heads)[None] * (hd ** -0.5)
    kg = jnp.tile(k_gain, n_heads)[None]
    head_of = jnp.arange(aw) // hd
    e = (head_of[:, None] == jnp.arange(LANES)[None, :]).astype(F32)
    et = e.T.astype(BF16)
    e = (e / hd).astype(BF16)
    q, k, v, ct = _fox_in(x, nw[None], wqkv, wf, bf, qg, kg, e, et, n_heads=n_heads)
    ck = ct.reshape(b, n_heads // 2, 2, s // t, t).transpose(0, 1, 3, 2, 4)
    attn = _fox_attn(q, k, v, ck, t=t)
    return _proj_res(attn.reshape(b * s, aw), w_out.astype(BF16), x.reshape(b * s, d)).reshape(b, s, d)


def _ssd_layer(x, nw, w_in, conv_w, conv_b, dt_bias, a_log, d_skip, norm_w, w_out):
    b, s, d = x.shape
    n_heads = a_log.shape[0]
    di = n_heads * SSM_HEAD_DIM
    cdim = conv_w.shape[1]
    wz = w_in[:, :di].astype(BF16)
    wxbc = w_in[:, di:di + cdim].astype(BF16)
    wdt = _pad_lanes(w_in[:, di + cdim:]).astype(BF16)
    z, xs, bm, cm, dt = _ssd_in(x, nw[None], wz, wxbc, wdt, conv_w, conv_b[None],
                                _pad_lanes(dt_bias[None]))
    head_of = jnp.arange(di) // SSM_HEAD_DIM
    eh = (jnp.arange(LANES)[:, None] == head_of[None, :]).astype(BF16)
    dskip = jnp.repeat(d_skip, SSM_HEAD_DIM)[None]
    y = _ssd_scan(xs, bm, cm, dt, _pad_lanes(a_log[None]), dskip, eh, n_heads=n_heads)
    m = b * s
    return _ssd_out(y.reshape(m, di), z.reshape(m, di), norm_w[None], w_out.astype(BF16),
                    x.reshape(m, d)).reshape(b, s, d)


def kernel(x, mix_norm, ffn_norm, ffn_w_gu, ffn_w_down, conv_w_in, conv_w_dw, conv_w_out, fox_w_in, fox_b_f, fox_q_gain, fox_k_gain, fox_w_out, ssd_w_in, ssd_conv_w, ssd_conv_b, ssd_dt_bias, ssd_a_log, ssd_d, ssd_norm_w, ssd_w_out):
    b, s, d = x.shape
    depth = mix_norm.shape[0]
    dff = ffn_w_down.shape[1]
    for i in range(depth):
        kind, j = i % 3, i // 3
        if kind == 0:
            x = _conv_layer(x, mix_norm[i], conv_w_in[j], conv_w_dw[j], conv_w_out[j])
        elif kind == 1:
            x = _fox_layer(x, mix_norm[i], fox_w_in[j], fox_b_f[j], fox_q_gain[j], fox_k_gain[j],
                           fox_w_out[j])
        else:
            x = _ssd_layer(x, mix_norm[i], ssd_w_in[j], ssd_conv_w[j], ssd_conv_b[j],
                           ssd_dt_bias[j], ssd_a_log[j], ssd_d[j], ssd_norm_w[j], ssd_w_out[j])
        wg = ffn_w_gu[i][:, :dff].astype(BF16)
        wu = ffn_w_gu[i][:, dff:].astype(BF16)
        x = _ffn(x.reshape(b * s, d), ffn_norm[i][None], wg, wu,
                 ffn_w_down[i].astype(BF16)).reshape(b, s, d)
    return x
```

```python
import functools

import jax
import jax.numpy as jnp
from jax import lax
from jax.experimental import pallas as pl
from jax.experimental.pallas import tpu as pltpu

F32 = jnp.float32
BF16 = jnp.bfloat16

RMS_EPS = 1e-6
ATTN_HEAD_DIM = 64
SSM_HEAD_DIM = 64
SSM_GROUPS = 8
SSM_STATE = 128
SSM_CHUNK = 128
LANES = 128
NEG_BIG = -1e30
LOG2E = 1.4426950408889634
SKIP_LOG2 = 96.0

VMEM_LIMIT = 56 * 1024 * 1024


def _params(sem):
    return pltpu.CompilerParams(dimension_semantics=sem, vmem_limit_bytes=VMEM_LIMIT)


def _const_spec(shape):
    nd = len(shape)
    return pl.BlockSpec(shape, lambda *_: (0,) * nd)


def _rmsnorm(x, w):
    return x * lax.rsqrt(jnp.mean(x * x, axis=-1, keepdims=True) + RMS_EPS) * w


def _dot(a, b):
    return jnp.dot(a, b, preferred_element_type=F32)


def _softplus(x):
    return jnp.maximum(x, 0.0) + jnp.log1p(jnp.exp(-jnp.abs(x)))


def _silu(x):
    return x * jax.nn.sigmoid(x)


def _lane_cumsum(x):
    n = x.shape[-1]
    lane = lax.broadcasted_iota(jnp.int32, x.shape, x.ndim - 1)
    s = 1
    while s < n:
        x = x + jnp.where(lane >= s, pltpu.roll(x, s, x.ndim - 1), 0.0)
        s *= 2
    return x


def _ffn_body(x_ref, nw_ref, wg_ref, wu_ref, wd_ref, o_ref, a_ref, *, chunk):
    x = x_ref[...]
    h = _rmsnorm(x, nw_ref[...]).astype(BF16)
    for c in range(wg_ref.shape[1] // chunk):
        sl = slice(c * chunk, (c + 1) * chunk)
        g = _dot(h, wg_ref[:, sl])
        u = _dot(h, wu_ref[:, sl])
        a_ref[:, sl] = (_silu(g) * u).astype(BF16)
    o_ref[...] = x + _dot(a_ref[...], wd_ref[...])


def _ffn(x, nw, wg, wu, wd, *, tm=512, chunk=256):
    m, d = x.shape
    dff = wg.shape[1]
    return pl.pallas_call(
        functools.partial(_ffn_body, chunk=chunk),
        grid=(m // tm,),
        in_specs=[
            pl.BlockSpec((tm, d), lambda i: (i, 0)),
            _const_spec((1, d)),
            _const_spec((d, dff)),
            _const_spec((d, dff)),
            _const_spec((dff, d)),
        ],
        out_specs=pl.BlockSpec((tm, d), lambda i: (i, 0)),
        out_shape=jax.ShapeDtypeStruct((m, d), F32),
        scratch_shapes=[pltpu.VMEM((tm, dff), BF16)],
        compiler_params=_params(("arbitrary",)),
        name="ffn",
    )(x, nw, wg, wu, wd)


def _conv_mixer_body(x_ref, nw_ref, win_ref, wdw_ref, wout_ref, o_ref, ext_ref, *, tm, halo):
    i = pl.program_id(1)
    d = x_ref.shape[-1]
    x = x_ref[...]
    h = _rmsnorm(x, nw_ref[...]).astype(BF16)
    proj = _dot(h, win_ref[...])
    bg = proj[:, :d]
    cv = proj[:, d:2 * d] * proj[:, 2 * d:]

    @pl.when(i == 0)
    def _():
        ext_ref[0:halo, :] = jnp.zeros((halo, d), F32)

    @pl.when(i > 0)
    def _():
        ext_ref[0:halo, :] = ext_ref[tm:tm + halo, :]

    ext_ref[halo:halo + tm, :] = cv
    kw = wdw_ref.shape[0]
    u = cv * wdw_ref[kw - 1:kw, :]
    for k in range(kw - 1):
        off = halo - (kw - 1) + k
        u = u + ext_ref[off:off + tm, :] * wdw_ref[k:k + 1, :]
    y = (bg * u).astype(BF16)
    o_ref[...] = x + _dot(y, wout_ref[...])


def _conv_mixer(x, nw, win, wdw, wout, *, tm=512):
    b, s, d = x.shape
    halo = 8
    return pl.pallas_call(
        functools.partial(_conv_mixer_body, tm=tm, halo=halo),
        grid=(b, s // tm),
        in_specs=[
            pl.BlockSpec((None, tm, d), lambda bi, i: (bi, i, 0)),
            _const_spec((1, d)),
            _const_spec(win.shape),
            _const_spec(wdw.shape),
            _const_spec(wout.shape),
        ],
        out_specs=pl.BlockSpec((None, tm, d), lambda bi, i: (bi, i, 0)),
        out_shape=jax.ShapeDtypeStruct((b, s, d), F32),
        scratch_shapes=[pltpu.VMEM((halo + tm, d), F32)],
        compiler_params=_params(("arbitrary", "arbitrary")),
        name="conv_mixer",
    )(x, nw, win, wdw, wout)


def _fox_in_body(x_ref, nw_ref, wqkv_ref, wf_ref, bf_ref, qg_ref, kg_ref, e_ref, et_ref,
                 q_ref, k_ref, v_ref, ct_ref, carry_ref, *, n_heads):
    i = pl.program_id(1)
    aw = q_ref.shape[-1]
    x = x_ref[...]
    h = _rmsnorm(x, nw_ref[...]).astype(BF16)
    qkv = _dot(h, wqkv_ref[...])

    def head_norm(t, gain):
        ms = _dot((t * t).astype(BF16), e_ref[...])
        r = lax.rsqrt(ms + RMS_EPS)
        r_hi = r.astype(BF16)
        r_lo = (r - r_hi.astype(F32)).astype(BF16)
        rb = _dot(r_hi, et_ref[...]) + _dot(r_lo, et_ref[...])
        return t * rb * gain

    q_ref[...] = head_norm(qkv[:, :aw], qg_ref[...]).astype(BF16)
    k_ref[...] = head_norm(qkv[:, aw:2 * aw], kg_ref[...]).astype(BF16)
    v_ref[...] = qkv[:, 2 * aw:].astype(BF16)

    fl = _dot(h, wf_ref[...]) + bf_ref[...]
    logf = -_softplus(-fl) * LOG2E
    local = _lane_cumsum(logf.T[0:n_heads, :])

    @pl.when(i == 0)
    def _():
        carry_ref[...] = jnp.zeros_like(carry_ref)

    cum = local + carry_ref[:, 0:1]
    ct_ref[...] = cum
    tm = cum.shape[-1]
    carry_ref[...] = jnp.broadcast_to(cum[:, tm - 1:tm], carry_ref.shape)


def _fox_in(x, nw, wqkv, wf, bf, qg, kg, e, et, *, n_heads, tm=512):
    b, s, d = x.shape
    aw = wqkv.shape[1] // 3
    row = lambda bi, i: (bi, i, 0)
    return pl.pallas_call(
        functools.partial(_fox_in_body, n_heads=n_heads),
        grid=(b, s // tm),
        in_specs=[
            pl.BlockSpec((None, tm, d), row),
            _const_spec((1, d)),
            _const_spec(wqkv.shape),
            _const_spec(wf.shape),
            _const_spec(bf.shape),
            _const_spec(qg.shape),
            _const_spec(kg.shape),
            _const_spec(e.shape),
            _const_spec(et.shape),
        ],
        out_specs=[
            pl.BlockSpec((None, tm, aw), row),
            pl.BlockSpec((None, tm, aw), row),
            pl.BlockSpec((None, tm, aw), row),
            pl.BlockSpec((None, n_heads, tm), lambda bi, i: (bi, 0, i)),
        ],
        out_shape=[
            jax.ShapeDtypeStruct((b, s, aw), BF16),
            jax.ShapeDtypeStruct((b, s, aw), BF16),
            jax.ShapeDtypeStruct((b, s, aw), BF16),
            jax.ShapeDtypeStruct((b, n_heads, s), F32),
        ],
        scratch_shapes=[pltpu.VMEM((n_heads, LANES), F32)],
        compiler_params=_params(("arbitrary", "arbitrary")),
        name="fox_in",
    )(x, nw, wqkv, wf, bf, qg, kg, e, et)


def _fox_attn_body(thr_ref, cs_ref, ce_ref, q_ref, k_ref, v_ref, ck_ref, o_ref, *, t, n_heads):
    bi, pr, i = pl.program_id(0), pl.program_id(1), pl.program_id(2)
    hd = ATTN_HEAD_DIM
    q2 = q_ref[...]
    lane = lax.broadcasted_iota(jnp.int32, (t, 2 * hd), 1)
    lo = lane < hd
    zero = jnp.zeros_like(q2)
    qs = (jnp.where(lo, q2, zero), jnp.where(lo, zero, q2))

    r0 = bi * n_heads + 2 * pr
    thr = thr_ref[0]
    cs0, cs1 = cs_ref[r0, i], cs_ref[r0 + 1, i]

    def needed(j):
        jj = jnp.maximum(j, 0)
        near = (cs0 - ce_ref[r0, jj] >= thr) | (cs1 - ce_ref[r0 + 1, jj] >= thr)
        return (j >= 0) & near

    j0 = lax.while_loop(needed, lambda j: j - 1, i - 1) + 1

    def scores(j):
        k2 = k_ref[pl.ds(pl.multiple_of(j * t, t), t), :]
        ck = ck_ref[j]
        return tuple(lax.dot_general(qs[hh], k2, (((1,), (1,)), ((), ())),
                                     preferred_element_type=F32) - ck[hh:hh + 1, :]
                     for hh in range(2))

    def update(j, s, ms, ls, acc, masked):
        v2 = v_ref[pl.ds(pl.multiple_of(j * t, t), t), :]
        if masked:
            row = lax.broadcasted_iota(jnp.int32, (t, t), 0)
            col = lax.broadcasted_iota(jnp.int32, (t, t), 1)
            s = tuple(jnp.where(row >= col, sh, NEG_BIG) for sh in s)
        new_m, new_l, alphas, pvs = [], [], [], []
        for hh in range(2):
            m_new = jnp.maximum(ms[hh], jnp.max(s[hh], axis=1, keepdims=True))
            alpha = jnp.exp2(ms[hh] - m_new)
            p = jnp.exp2(s[hh] - m_new)
            new_l.append(alpha * ls[hh] + jnp.sum(p, axis=1, keepdims=True))
            new_m.append(m_new)
            alphas.append(alpha)
            pvs.append(_dot(p.astype(BF16), v2))
        acc = acc * jnp.where(lo, alphas[0], alphas[1]) + jnp.where(lo, pvs[0], pvs[1])
        return tuple(new_m), tuple(new_l), acc

    def body(j, carry):
        s, ms, ls, acc = carry
        s_next = scores(j + 1)
        ms, ls, acc = update(j, s, ms, ls, acc, False)
        return s_next, ms, ls, acc

    m0 = jnp.full((t, 1), NEG_BIG, F32)
    l0 = jnp.zeros((t, 1), F32)
    carry = (scores(j0), (m0, m0), (l0, l0), jnp.zeros((t, 2 * hd), F32))
    s, ms, ls, acc = lax.fori_loop(j0, i, body, carry)
    _, ls, acc = update(i, s, ms, ls, acc, True)
    o_ref[...] = (acc / jnp.where(lo, ls[0], ls[1])).astype(o_ref.dtype)


def _fox_attn(thr, cs, ce, q, k, v, ck, *, t, n_heads):
    b, s, aw = q.shape
    pairs = aw // LANES
    smem = pl.BlockSpec(memory_space=pltpu.SMEM)
    return pl.pallas_call(
        functools.partial(_fox_attn_body, t=t, n_heads=n_heads),
        grid=(b, pairs, s // t),
        in_specs=[
            smem, smem, smem,
            pl.BlockSpec((None, t, LANES), lambda bi, p, i: (bi, i, p)),
            pl.BlockSpec((None, s, LANES), lambda bi, p, i: (bi, 0, p)),
            pl.BlockSpec((None, s, LANES), lambda bi, p, i: (bi, 0, p)),
            pl.BlockSpec((None, None, s // t, 2, t), lambda bi, p, i: (bi, p, 0, 0, 0)),
        ],
        out_specs=pl.BlockSpec((None, t, LANES), lambda bi, p, i: (bi, i, p)),
        out_shape=jax.ShapeDtypeStruct((b, s, aw), BF16),
        compiler_params=_params(("arbitrary", "arbitrary", "arbitrary")),
        name="fox_attn",
    )(thr, cs, ce, q, k, v, ck)


def _proj_res_body(y_ref, w_ref, x_ref, o_ref):
    o_ref[...] = x_ref[...] + _dot(y_ref[...], w_ref[...])


def _proj_res(y, w, x, *, tm=512):
    m, kdim = y.shape
    d = w.shape[1]
    return pl.pallas_call(
        _proj_res_body,
        grid=(m // tm,),
        in_specs=[
            pl.BlockSpec((tm, kdim), lambda i: (i, 0)),
            _const_spec(w.shape),
            pl.BlockSpec((tm, d), lambda i: (i, 0)),
        ],
        out_specs=pl.BlockSpec((tm, d), lambda i: (i, 0)),
        out_shape=jax.ShapeDtypeStruct((m, d), F32),
        compiler_params=_params(("arbitrary",)),
        name="proj_res",
    )(y, w, x)


def _ssd_in_body(x_ref, nw_ref, wz_ref, wxbc_ref, wdt_ref, cw_ref, cb_ref, dtb_ref,
                 z_ref, xs_ref, b_ref, c_ref, dt_ref, ext_ref, *, tm, halo):
    i = pl.program_id(1)
    di = xs_ref.shape[-1]
    gn = b_ref.shape[-1]
    cdim = ext_ref.shape[-1]
    x = x_ref[...]
    h = _rmsnorm(x, nw_ref[...]).astype(BF16)
    z_ref[...] = _dot(h, wz_ref[...])
    dt_ref[...] = _softplus(_dot(h, wdt_ref[...]) + dtb_ref[...])
    raw = _dot(h, wxbc_ref[...])

    @pl.when(i == 0)
    def _():
        ext_ref[0:halo, :] = jnp.zeros((halo, cdim), F32)

    @pl.when(i > 0)
    def _():
        ext_ref[0:halo, :] = ext_ref[tm:tm + halo, :]

    ext_ref[halo:halo + tm, :] = raw
    kw = cw_ref.shape[0]
    u = raw * cw_ref[kw - 1:kw, :] + cb_ref[...]
    for k in range(kw - 1):
        off = halo - (kw - 1) + k
        u = u + ext_ref[off:off + tm, :] * cw_ref[k:k + 1, :]
    u = _silu(u)
    xs_ref[...] = u[:, :di]
    b_ref[...] = u[:, di:di + gn]
    c_ref[...] = u[:, di + gn:]


def _ssd_in(x, nw, wz, wxbc, wdt, cw, cb, dtb, *, tm=256):
    b, s, d = x.shape
    di = wz.shape[1]
    cdim = wxbc.shape[1]
    gn = (cdim - di) // 2
    halo = 8
    row = lambda bi, i: (bi, i, 0)
    return pl.pallas_call(
        functools.partial(_ssd_in_body, tm=tm, halo=halo),
        grid=(b, s // tm),
        in_specs=[
            pl.BlockSpec((None, tm, d), row),
            _const_spec((1, d)),
            _const_spec(wz.shape),
            _const_spec(wxbc.shape),
            _const_spec(wdt.shape),
            _const_spec(cw.shape),
            _const_spec(cb.shape),
            _const_spec(dtb.shape),
        ],
        out_specs=[
            pl.BlockSpec((None, tm, di), row),
            pl.BlockSpec((None, tm, di), row),
            pl.BlockSpec((None, tm, gn), row),
            pl.BlockSpec((None, tm, gn), row),
            pl.BlockSpec((None, tm, LANES), row),
        ],
        out_shape=[
            jax.ShapeDtypeStruct((b, s, di), F32),
            jax.ShapeDtypeStruct((b, s, di), F32),
            jax.ShapeDtypeStruct((b, s, gn), F32),
            jax.ShapeDtypeStruct((b, s, gn), F32),
            jax.ShapeDtypeStruct((b, s, LANES), F32),
        ],
        scratch_shapes=[pltpu.VMEM((halo + tm, cdim), F32)],
        compiler_params=_params(("arbitrary", "arbitrary")),
        name="ssd_in",
    )(x, nw, wz, wxbc, wdt, cw, cb, dtb)


def _ssd_scan_body(xs_ref, b_ref, c_ref, dt_ref, alog_ref, dskip_ref, eh_ref, y_ref, st_ref,
                   *, n_heads):
    ci = pl.program_id(1)
    L = xs_ref.shape[0]
    P, N, G = SSM_HEAD_DIM, SSM_STATE, SSM_GROUPS
    hpg = n_heads // G
    gw = hpg * P

    @pl.when(ci == 0)
    def _():
        st_ref[...] = jnp.zeros_like(st_ref)

    lane_h = lax.broadcasted_iota(jnp.int32, (1, LANES), 1)
    a = jnp.where(lane_h < n_heads, -jnp.exp(alog_ref[...]), 0.0)
    dt = dt_ref[...]
    dtT = dt.T
    acumT = _lane_cumsum((dt * a).T)
    acum = acumT.T
    a_last = acum[L - 1:L, :]

    sdt = jnp.exp(a_last - acum) * dt
    ea = jnp.exp(acum)

    def expand(t):
        hi = t.astype(BF16)
        lo_ = (t - hi.astype(F32)).astype(BF16)
        return _dot(hi, eh_ref[...]) + _dot(lo_, eh_ref[...])

    sdt_b = expand(sdt)
    ea_b = expand(ea)
    xs = xs_ref[...]
    xw = (xs * sdt_b).astype(BF16)
    xs16 = xs.astype(BF16)
    decay_last = ea_b[L - 1:L, :]

    row = lax.broadcasted_iota(jnp.int32, (L, L), 0)
    col = lax.broadcasted_iota(jnp.int32, (L, L), 1)
    tril = row >= col
    lane2 = lax.broadcasted_iota(jnp.int32, (L, 2 * P), 1)
    lo = lane2 < P

    for g in range(G):
        bg = b_ref[:, g * N:(g + 1) * N].astype(BF16)
        cg = c_ref[:, g * N:(g + 1) * N].astype(BF16)
        cb = lax.dot_general(cg, bg, (((1,), (1,)), ((), ())), preferred_element_type=F32)
        gs = slice(g * gw, (g + 1) * gw)
        st = st_ref[g]
        y_off = _dot(cg, st.astype(BF16)) * ea_b[:, gs]
        contrib = lax.dot_general(bg, xw[:, gs], (((0,), (0,)), ((), ())),
                                  preferred_element_type=F32)
        st_ref[g] = st * decay_last[:, gs] + contrib
        for pr in range(hpg // 2):
            ws = []
            for hh in range(2):
                hidx = g * hpg + 2 * pr + hh
                seg = acum[:, hidx:hidx + 1] - acumT[hidx:hidx + 1, :]
                decay = jnp.exp(jnp.where(tril, seg, NEG_BIG))
                ws.append((cb * decay * dtT[hidx:hidx + 1, :]).astype(BF16))
            ps = slice(g * gw + pr * 2 * P, g * gw + (pr + 1) * 2 * P)
            x2 = xs16[:, ps]
            y_diag = jnp.where(lo, _dot(ws[0], x2), _dot(ws[1], x2))
            y_ref[:, ps] = y_diag + y_off[:, pr * 2 * P:(pr + 1) * 2 * P] + dskip_ref[:, ps] * xs[:, ps]


def _ssd_scan(xs, bm, cm, dt, alog, dskip, eh, *, n_heads):
    b, s, di = xs.shape
    gn = bm.shape[-1]
    L = SSM_CHUNK
    row = lambda bi, i: (bi, i, 0)
    return pl.pallas_call(
        functools.partial(_ssd_scan_body, n_heads=n_heads),
        grid=(b, s // L),
        in_specs=[
            pl.BlockSpec((None, L, di), row),
            pl.BlockSpec((None, L, gn), row),
            pl.BlockSpec((None, L, gn), row),
            pl.BlockSpec((None, L, LANES), row),
            _const_spec(alog.shape),
            _const_spec(dskip.shape),
            _const_spec(eh.shape),
        ],
        out_specs=pl.BlockSpec((None, L, di), row),
        out_shape=jax.ShapeDtypeStruct((b, s, di), F32),
        scratch_shapes=[pltpu.VMEM((SSM_GROUPS, SSM_STATE, di // SSM_GROUPS), F32)],
        compiler_params=_params(("arbitrary", "arbitrary")),
        name="ssd_scan",
    )(xs, bm, cm, dt, alog, dskip, eh)


def _ssd_out_body(y_ref, z_ref, nw_ref, w_ref, x_ref, o_ref, yn_ref):
    di = y_ref.shape[-1]
    gw = di // SSM_GROUPS
    for g in range(SSM_GROUPS):
        gs = slice(g * gw, (g + 1) * gw)
        yg = y_ref[:, gs] * _silu(z_ref[:, gs])
        yn_ref[:, gs] = _rmsnorm(yg, nw_ref[:, gs]).astype(BF16)
    o_ref[...] = x_ref[...] + _dot(yn_ref[...], w_ref[...])


def _ssd_out(y, z, nw, w, x, *, tm=512):
    m, di = y.shape
    d = w.shape[1]
    return pl.pallas_call(
        _ssd_out_body,
        grid=(m // tm,),
        in_specs=[
            pl.BlockSpec((tm, di), lambda i: (i, 0)),
            pl.BlockSpec((tm, di), lambda i: (i, 0)),
            _const_spec(nw.shape),
            _const_spec(w.shape),
            pl.BlockSpec((tm, d), lambda i: (i, 0)),
        ],
        out_specs=pl.BlockSpec((tm, d), lambda i: (i, 0)),
        out_shape=jax.ShapeDtypeStruct((m, d), F32),
        scratch_shapes=[pltpu.VMEM((tm, di), BF16)],
        compiler_params=_params(("arbitrary",)),
        name="ssd_out",
    )(y, z, nw, w, x)


def _pad_lanes(a, width=LANES):
    return jnp.pad(a, ((0, 0), (0, width - a.shape[-1])))


def _conv_layer(x, nw, w_in, w_dw, w_out):
    return _conv_mixer(x, nw[None], w_in.astype(BF16), w_dw, w_out.astype(BF16))


def _fox_layer(x, nw, w_in, b_f, q_gain, k_gain, w_out, *, t=256):
    b, s, d = x.shape
    n_heads = b_f.shape[0]
    hd = ATTN_HEAD_DIM
    aw = n_heads * hd
    wqkv = w_in[:, :3 * aw].astype(BF16)
    wf = _pad_lanes(w_in[:, 3 * aw:]).astype(BF16)
    bf = _pad_lanes(b_f[None])
    qg = jnp.tile(q_gain, n_heads)[None] * (hd ** -0.5 * LOG2E)
    kg = jnp.tile(k_gain, n_heads)[None]
    smax = 1.02 * LOG2E * hd ** 0.5 * jnp.max(jnp.abs(q_gain)) * jnp.max(jnp.abs(k_gain))
    thr = -(SKIP_LOG2 + 2.0 * smax).reshape(1).astype(F32)
    head_of = jnp.arange(aw) // hd
    e = (head_of[:, None] == jnp.arange(LANES)[None, :]).astype(F32)
    et = e.T.astype(BF16)
    e = (e / hd).astype(BF16)
    q, k, v, ct = _fox_in(x, nw[None], wqkv, wf, bf, qg, kg, e, et, n_heads=n_heads)
    ck = ct.reshape(b, n_heads // 2, 2, s // t, t).transpose(0, 1, 3, 2, 4)
    cs = ct[:, :, 0::t].reshape(b * n_heads, s // t)
    ce = ct[:, :, t - 1::t].reshape(b * n_heads, s // t)
    attn = _fox_attn(thr, cs, ce, q, k, v, ck, t=t, n_heads=n_heads)
    return _proj_res(attn.reshape(b * s, aw), w_out.astype(BF16), x.reshape(b * s, d)).reshape(b, s, d)


def _ssd_layer(x, nw, w_in, conv_w, conv_b, dt_bias, a_log, d_skip, norm_w, w_out):
    b, s, d = x.shape
    n_heads = a_log.shape[0]
    di = n_heads * SSM_HEAD_DIM
    cdim = conv_w.shape[1]
    wz = w_in[:, :di].astype(BF16)
    wxbc = w_in[:, di:di + cdim].astype(BF16)
    wdt = _pad_lanes(w_in[:, di + cdim:]).astype(BF16)
    z, xs, bm, cm, dt = _ssd_in(x, nw[None], wz, wxbc, wdt, conv_w, conv_b[None],
                                _pad_lanes(dt_bias[None]))
    head_of = jnp.arange(di) // SSM_HEAD_DIM
    eh = (jnp.arange(LANES)[:, None] == head_of[None, :]).astype(BF16)
    dskip = jnp.repeat(d_skip, SSM_HEAD_DIM)[None]
    y = _ssd_scan(xs, bm, cm, dt, _pad_lanes(a_log[None]), dskip, eh, n_heads=n_heads)
    m = b * s
    return _ssd_out(y.reshape(m, di), z.reshape(m, di), norm_w[None], w_out.astype(BF16),
                    x.reshape(m, d)).reshape(b, s, d)


def kernel(x, mix_norm, ffn_norm, ffn_w_gu, ffn_w_down, conv_w_in, conv_w_dw, conv_w_out, fox_w_in, fox_b_f, fox_q_gain, fox_k_gain, fox_w_out, ssd_w_in, ssd_conv_w, ssd_conv_b, ssd_dt_bias, ssd_a_log, ssd_d, ssd_norm_w, ssd_w_out):
    b, s, d = x.shape
    depth = mix_norm.shape[0]
    dff = ffn_w_down.shape[1]
    for i in range(depth):
        kind, j = i % 3, i // 3
        if kind == 0:
            x = _conv_layer(x, mix_norm[i], conv_w_in[j], conv_w_dw[j], conv_w_out[j])
        elif kind == 1:
            x = _fox_layer(x, mix_norm[i], fox_w_in[j], fox_b_f[j], fox_q_gain[j], fox_k_gain[j],
                           fox_w_out[j])
        else:
            x = _ssd_layer(x, mix_norm[i], ssd_w_in[j], ssd_conv_w[j], ssd_conv_b[j],
                           ssd_dt_bias[j], ssd_a_log[j], ssd_d[j], ssd_norm_w[j], ssd_w_out[j])
        wg = ffn_w_gu[i][:, :dff].astype(BF16)
        wu = ffn_w_gu[i][:, dff:].astype(BF16)
        x = _ffn(x.reshape(b * s, d), ffn_norm[i][None], wg, wu,
                 ffn_w_down[i].astype(BF16)).reshape(b, s, d)
    return x
```

```python
import functools

import jax
import jax.numpy as jnp
from jax import lax
from jax.experimental import pallas as pl
from jax.experimental.pallas import tpu as pltpu

F32 = jnp.float32
BF16 = jnp.bfloat16

RMS_EPS = 1e-6
ATTN_HEAD_DIM = 64
SSM_HEAD_DIM = 64
SSM_GROUPS = 8
SSM_STATE = 128
SSM_CHUNK = 128
LANES = 128
NEG_BIG = -1e30
LOG2E = 1.4426950408889634
SKIP_LOG2 = 64.0
FAST_PATH_MAX_LOG2 = 100.0

VMEM_LIMIT = 56 * 1024 * 1024


def _params(sem):
    return pltpu.CompilerParams(dimension_semantics=sem, vmem_limit_bytes=VMEM_LIMIT)


def _const_spec(shape):
    nd = len(shape)
    return pl.BlockSpec(shape, lambda *_: (0,) * nd)


def _rmsnorm(x, w):
    return x * lax.rsqrt(jnp.mean(x * x, axis=-1, keepdims=True) + RMS_EPS) * w


def _dot(a, b):
    return jnp.dot(a, b, preferred_element_type=F32)


def _softplus(x):
    return jnp.maximum(x, 0.0) + jnp.log1p(jnp.exp(-jnp.abs(x)))


def _silu(x):
    return x * jax.nn.sigmoid(x)


def _lane_cumsum(x):
    n = x.shape[-1]
    lane = lax.broadcasted_iota(jnp.int32, x.shape, x.ndim - 1)
    s = 1
    while s < n:
        x = x + jnp.where(lane >= s, pltpu.roll(x, s, x.ndim - 1), 0.0)
        s *= 2
    return x


def _ffn_body(x_ref, nw_ref, wg_ref, wu_ref, wd_ref, o_ref, a_ref, *, chunk):
    x = x_ref[...]
    h = _rmsnorm(x, nw_ref[...]).astype(BF16)
    for c in range(wg_ref.shape[1] // chunk):
        sl = slice(c * chunk, (c + 1) * chunk)
        g = _dot(h, wg_ref[:, sl])
        u = _dot(h, wu_ref[:, sl])
        a_ref[:, sl] = (_silu(g) * u).astype(BF16)
    o_ref[...] = x + _dot(a_ref[...], wd_ref[...])


def _ffn(x, nw, wg, wu, wd, *, tm=512, chunk=256):
    m, d = x.shape
    dff = wg.shape[1]
    return pl.pallas_call(
        functools.partial(_ffn_body, chunk=chunk),
        grid=(m // tm,),
        in_specs=[
            pl.BlockSpec((tm, d), lambda i: (i, 0)),
            _const_spec((1, d)),
            _const_spec((d, dff)),
            _const_spec((d, dff)),
            _const_spec((dff, d)),
        ],
        out_specs=pl.BlockSpec((tm, d), lambda i: (i, 0)),
        out_shape=jax.ShapeDtypeStruct((m, d), F32),
        scratch_shapes=[pltpu.VMEM((tm, dff), BF16)],
        compiler_params=_params(("arbitrary",)),
        name="ffn",
    )(x, nw, wg, wu, wd)


def _conv_mixer_body(x_ref, nw_ref, win_ref, wdw_ref, wout_ref, o_ref, ext_ref, *, tm, halo):
    i = pl.program_id(1)
    d = x_ref.shape[-1]
    x = x_ref[...]
    h = _rmsnorm(x, nw_ref[...]).astype(BF16)
    proj = _dot(h, win_ref[...])
    bg = proj[:, :d]
    cv = proj[:, d:2 * d] * proj[:, 2 * d:]

    @pl.when(i == 0)
    def _():
        ext_ref[0:halo, :] = jnp.zeros((halo, d), F32)

    @pl.when(i > 0)
    def _():
        ext_ref[0:halo, :] = ext_ref[tm:tm + halo, :]

    ext_ref[halo:halo + tm, :] = cv
    kw = wdw_ref.shape[0]
    u = cv * wdw_ref[kw - 1:kw, :]
    for k in range(kw - 1):
        off = halo - (kw - 1) + k
        u = u + ext_ref[off:off + tm, :] * wdw_ref[k:k + 1, :]
    y = (bg * u).astype(BF16)
    o_ref[...] = x + _dot(y, wout_ref[...])


def _conv_mixer(x, nw, win, wdw, wout, *, tm=512):
    b, s, d = x.shape
    halo = 8
    return pl.pallas_call(
        functools.partial(_conv_mixer_body, tm=tm, halo=halo),
        grid=(b, s // tm),
        in_specs=[
            pl.BlockSpec((None, tm, d), lambda bi, i: (bi, i, 0)),
            _const_spec((1, d)),
            _const_spec(win.shape),
            _const_spec(wdw.shape),
            _const_spec(wout.shape),
        ],
        out_specs=pl.BlockSpec((None, tm, d), lambda bi, i: (bi, i, 0)),
        out_shape=jax.ShapeDtypeStruct((b, s, d), F32),
        scratch_shapes=[pltpu.VMEM((halo + tm, d), F32)],
        compiler_params=_params(("arbitrary", "arbitrary")),
        name="conv_mixer",
    )(x, nw, win, wdw, wout)


def _fox_in_body(x_ref, nw_ref, wqkv_ref, wf_ref, bf_ref, qg_ref, kg_ref, e_ref, et_ref,
                 q_ref, k_ref, v_ref, ct_ref, carry_ref, *, n_heads):
    i = pl.program_id(1)
    aw = q_ref.shape[-1]
    x = x_ref[...]
    h = _rmsnorm(x, nw_ref[...]).astype(BF16)
    qkv = _dot(h, wqkv_ref[...])

    def head_norm(t, gain):
        ms = _dot((t * t).astype(BF16), e_ref[...])
        r = lax.rsqrt(ms + RMS_EPS)
        r_hi = r.astype(BF16)
        r_lo = (r - r_hi.astype(F32)).astype(BF16)
        rb = _dot(r_hi, et_ref[...]) + _dot(r_lo, et_ref[...])
        return t * rb * gain

    q_ref[...] = head_norm(qkv[:, :aw], qg_ref[...]).astype(BF16)
    k_ref[...] = head_norm(qkv[:, aw:2 * aw], kg_ref[...]).astype(BF16)
    v_ref[...] = qkv[:, 2 * aw:].astype(BF16)

    fl = _dot(h, wf_ref[...]) + bf_ref[...]
    logf = -_softplus(-fl) * LOG2E
    local = _lane_cumsum(logf.T[0:n_heads, :])

    @pl.when(i == 0)
    def _():
        carry_ref[...] = jnp.zeros_like(carry_ref)

    cum = local + carry_ref[:, 0:1]
    ct_ref[...] = cum
    tm = cum.shape[-1]
    carry_ref[...] = jnp.broadcast_to(cum[:, tm - 1:tm], carry_ref.shape)


def _fox_in(x, nw, wqkv, wf, bf, qg, kg, e, et, *, n_heads, tm=512):
    b, s, d = x.shape
    aw = wqkv.shape[1] // 3
    row = lambda bi, i: (bi, i, 0)
    return pl.pallas_call(
        functools.partial(_fox_in_body, n_heads=n_heads),
        grid=(b, s // tm),
        in_specs=[
            pl.BlockSpec((None, tm, d), row),
            _const_spec((1, d)),
            _const_spec(wqkv.shape),
            _const_spec(wf.shape),
            _const_spec(bf.shape),
            _const_spec(qg.shape),
            _const_spec(kg.shape),
            _const_spec(e.shape),
            _const_spec(et.shape),
        ],
        out_specs=[
            pl.BlockSpec((None, tm, aw), row),
            pl.BlockSpec((None, tm, aw), row),
            pl.BlockSpec((None, tm, aw), row),
            pl.BlockSpec((None, n_heads, tm), lambda bi, i: (bi, 0, i)),
        ],
        out_shape=[
            jax.ShapeDtypeStruct((b, s, aw), BF16),
            jax.ShapeDtypeStruct((b, s, aw), BF16),
            jax.ShapeDtypeStruct((b, s, aw), BF16),
            jax.ShapeDtypeStruct((b, n_heads, s), F32),
        ],
        scratch_shapes=[pltpu.VMEM((n_heads, LANES), F32)],
        compiler_params=_params(("arbitrary", "arbitrary")),
        name="fox_in",
    )(x, nw, wqkv, wf, bf, qg, kg, e, et)


def _fox_attn_body(thr_ref, cs_ref, ce_ref, q_ref, k_ref, v_ref, ck_ref, o_ref, *, t, n_heads):
    bi, pr, i = pl.program_id(0), pl.program_id(1), pl.program_id(2)
    hd = ATTN_HEAD_DIM
    q2 = q_ref[...]
    lane = lax.broadcasted_iota(jnp.int32, (t, 2 * hd), 1)
    lo = lane < hd
    zero = jnp.zeros_like(q2)
    qs = (jnp.where(lo, q2, zero), jnp.where(lo, zero, q2))

    r0 = bi * n_heads + 2 * pr
    thr = thr_ref[0]
    cs0, cs1 = cs_ref[r0, i], cs_ref[r0 + 1, i]

    def needed(j):
        jj = jnp.maximum(j, 0)
        near = (cs0 - ce_ref[r0, jj] >= thr) | (cs1 - ce_ref[r0 + 1, jj] >= thr)
        return (j >= 0) & near

    j0 = lax.while_loop(needed, lambda j: j - 1, i - 1) + 1

    def scores(j):
        k2 = k_ref[pl.ds(pl.multiple_of(j * t, t), t), :]
        ck = ck_ref[j]
        return tuple(lax.dot_general(qs[hh], k2, (((1,), (1,)), ((), ())),
                                     preferred_element_type=F32) - ck[hh:hh + 1, :]
                     for hh in range(2))

    def update(j, s, ms, ls, acc, masked):
        v2 = v_ref[pl.ds(pl.multiple_of(j * t, t), t), :]
        if masked:
            row = lax.broadcasted_iota(jnp.int32, (t, t), 0)
            col = lax.broadcasted_iota(jnp.int32, (t, t), 1)
            s = tuple(jnp.where(row >= col, sh, NEG_BIG) for sh in s)
        new_m, new_l, alphas, pvs = [], [], [], []
        for hh in range(2):
            m_new = jnp.maximum(ms[hh], jnp.max(s[hh], axis=1, keepdims=True))
            alpha = jnp.exp2(ms[hh] - m_new)
            p = jnp.exp2(s[hh] - m_new)
            new_l.append(alpha * ls[hh] + jnp.sum(p, axis=1, keepdims=True))
            new_m.append(m_new)
            alphas.append(alpha)
            pvs.append(_dot(p.astype(BF16), v2))
        acc = acc * jnp.where(lo, alphas[0], alphas[1]) + jnp.where(lo, pvs[0], pvs[1])
        return tuple(new_m), tuple(new_l), acc

    @pl.when(thr_ref[1] <= 0.0)
    def _():
        def body(j, carry):
            s, ms, ls, acc = carry
            s_next = scores(j + 1)
            ms, ls, acc = update(j, s, ms, ls, acc, False)
            return s_next, ms, ls, acc

        m0 = jnp.full((t, 1), NEG_BIG, F32)
        l0 = jnp.zeros((t, 1), F32)
        carry = (scores(j0), (m0, m0), (l0, l0), jnp.zeros((t, 2 * hd), F32))
        s, ms, ls, acc = lax.fori_loop(j0, i, body, carry)
        _, ls, acc = update(i, s, ms, ls, acc, True)
        o_ref[...] = (acc / jnp.where(lo, ls[0], ls[1])).astype(o_ref.dtype)

    @pl.when(thr_ref[1] > 0.0)
    def _():
        e_lo = jnp.where(lo, 1.0, 0.0).astype(BF16)
        e_hi = jnp.where(lo, 0.0, 1.0).astype(BF16)

        def pv(j, p16):
            v2 = v_ref[pl.ds(pl.multiple_of(j * t, t), t), :]
            vb = jnp.concatenate(
                [jnp.concatenate([v2 * e_lo, e_lo], axis=1),
                 jnp.concatenate([v2 * e_hi, e_hi], axis=1)], axis=0)
            return _dot(p16, vb)

        def weights(s):
            return jnp.concatenate([jnp.exp2(s[hh] - ms[hh]).astype(BF16) for hh in range(2)], axis=1)

        row = lax.broadcasted_iota(jnp.int32, (t, t), 0)
        col = lax.broadcasted_iota(jnp.int32, (t, t), 1)
        sd = tuple(jnp.where(row >= col, sh, NEG_BIG) for sh in scores(i))
        ms = tuple(jnp.max(sh, axis=1, keepdims=True) for sh in sd)

        def body(j, carry):
            jprev, p16, acc = carry
            p_new = weights(scores(j))
            return j, p_new, acc + pv(jprev, p16)

        carry = (i, weights(sd), jnp.zeros((t, 4 * hd), F32))
        jprev, p16, acc = lax.fori_loop(j0, i, body, carry)
        acc = acc + pv(jprev, p16)
        o_ref[...] = (acc[:, :2 * hd] / acc[:, 2 * hd:]).astype(o_ref.dtype)


def _fox_attn(thr, cs, ce, q, k, v, ck, *, t, n_heads):
    b, s, aw = q.shape
    pairs = aw // LANES
    smem = pl.BlockSpec(memory_space=pltpu.SMEM)
    return pl.pallas_call(
        functools.partial(_fox_attn_body, t=t, n_heads=n_heads),
        grid=(b, pairs, s // t),
        in_specs=[
            smem, smem, smem,
            pl.BlockSpec((None, t, LANES), lambda bi, p, i: (bi, i, p)),
            pl.BlockSpec((None, s, LANES), lambda bi, p, i: (bi, 0, p)),
            pl.BlockSpec((None, s, LANES), lambda bi, p, i: (bi, 0, p)),
            pl.BlockSpec((None, None, s // t, 2, t), lambda bi, p, i: (bi, p, 0, 0, 0)),
        ],
        out_specs=pl.BlockSpec((None, t, LANES), lambda bi, p, i: (bi, i, p)),
        out_shape=jax.ShapeDtypeStruct((b, s, aw), BF16),
        compiler_params=_params(("arbitrary", "arbitrary", "arbitrary")),
        name="fox_attn",
    )(thr, cs, ce, q, k, v, ck)


def _proj_res_body(y_ref, w_ref, x_ref, o_ref):
    o_ref[...] = x_ref[...] + _dot(y_ref[...], w_ref[...])


def _proj_res(y, w, x, *, tm=512):
    m, kdim = y.shape
    d = w.shape[1]
    return pl.pallas_call(
        _proj_res_body,
        grid=(m // tm,),
        in_specs=[
            pl.BlockSpec((tm, kdim), lambda i: (i, 0)),
            _const_spec(w.shape),
            pl.BlockSpec((tm, d), lambda i: (i, 0)),
        ],
        out_specs=pl.BlockSpec((tm, d), lambda i: (i, 0)),
        out_shape=jax.ShapeDtypeStruct((m, d), F32),
        compiler_params=_params(("arbitrary",)),
        name="proj_res",
    )(y, w, x)


def _ssd_in_body(x_ref, nw_ref, wz_ref, wxbc_ref, wdt_ref, cw_ref, cb_ref, dtb_ref,
                 z_ref, xs_ref, b_ref, c_ref, dt_ref, ext_ref, *, tm, halo):
    i = pl.program_id(1)
    di = xs_ref.shape[-1]
    gn = b_ref.shape[-1]
    cdim = ext_ref.shape[-1]
    x = x_ref[...]
    h = _rmsnorm(x, nw_ref[...]).astype(BF16)
    z_ref[...] = _dot(h, wz_ref[...])
    dt_ref[...] = _softplus(_dot(h, wdt_ref[...]) + dtb_ref[...])
    raw = _dot(h, wxbc_ref[...])

    @pl.when(i == 0)
    def _():
        ext_ref[0:halo, :] = jnp.zeros((halo, cdim), F32)

    @pl.when(i > 0)
    def _():
        ext_ref[0:halo, :] = ext_ref[tm:tm + halo, :]

    ext_ref[halo:halo + tm, :] = raw
    kw = cw_ref.shape[0]
    u = raw * cw_ref[kw - 1:kw, :] + cb_ref[...]
    for k in range(kw - 1):
        off = halo - (kw - 1) + k
        u = u + ext_ref[off:off + tm, :] * cw_ref[k:k + 1, :]
    u = _silu(u)
    xs_ref[...] = u[:, :di]
    b_ref[...] = u[:, di:di + gn]
    c_ref[...] = u[:, di + gn:]


def _ssd_in(x, nw, wz, wxbc, wdt, cw, cb, dtb, *, tm=256):
    b, s, d = x.shape
    di = wz.shape[1]
    cdim = wxbc.shape[1]
    gn = (cdim - di) // 2
    halo = 8
    row = lambda bi, i: (bi, i, 0)
    return pl.pallas_call(
        functools.partial(_ssd_in_body, tm=tm, halo=halo),
        grid=(b, s // tm),
        in_specs=[
            pl.BlockSpec((None, tm, d), row),
            _const_spec((1, d)),
            _const_spec(wz.shape),
            _const_spec(wxbc.shape),
            _const_spec(wdt.shape),
            _const_spec(cw.shape),
            _const_spec(cb.shape),
            _const_spec(dtb.shape),
        ],
        out_specs=[
            pl.BlockSpec((None, tm, di), row),
            pl.BlockSpec((None, tm, di), row),
            pl.BlockSpec((None, tm, gn), row),
            pl.BlockSpec((None, tm, gn), row),
            pl.BlockSpec((None, tm, LANES), row),
        ],
        out_shape=[
            jax.ShapeDtypeStruct((b, s, di), F32),
            jax.ShapeDtypeStruct((b, s, di), F32),
            jax.ShapeDtypeStruct((b, s, gn), F32),
            jax.ShapeDtypeStruct((b, s, gn), F32),
            jax.ShapeDtypeStruct((b, s, LANES), F32),
        ],
        scratch_shapes=[pltpu.VMEM((halo + tm, cdim), F32)],
        compiler_params=_params(("arbitrary", "arbitrary")),
        name="ssd_in",
    )(x, nw, wz, wxbc, wdt, cw, cb, dtb)


def _ssd_scan_body(xs_ref, b_ref, c_ref, dt_ref, alog_ref, dskip_ref, eh_ref, y_ref, st_ref,
                   *, n_heads):
    ci = pl.program_id(1)
    L = xs_ref.shape[0]
    P, N, G = SSM_HEAD_DIM, SSM_STATE, SSM_GROUPS
    hpg = n_heads // G
    gw = hpg * P

    @pl.when(ci == 0)
    def _():
        st_ref[...] = jnp.zeros_like(st_ref)

    lane_h = lax.broadcasted_iota(jnp.int32, (1, LANES), 1)
    a = jnp.where(lane_h < n_heads, -jnp.exp(alog_ref[...]), 0.0)
    dt = dt_ref[...]
    dtT = dt.T
    acumT = _lane_cumsum((dt * a).T)
    acum = acumT.T
    a_last = acum[L - 1:L, :]

    sdt = jnp.exp(a_last - acum) * dt
    ea = jnp.exp(acum)

    def expand(t):
        hi = t.astype(BF16)
        lo_ = (t - hi.astype(F32)).astype(BF16)
        return _dot(hi, eh_ref[...]) + _dot(lo_, eh_ref[...])

    sdt_b = expand(sdt)
    ea_b = expand(ea)
    xs = xs_ref[...]
    xw = (xs * sdt_b).astype(BF16)
    xs16 = xs.astype(BF16)
    decay_last = ea_b[L - 1:L, :]

    row = lax.broadcasted_iota(jnp.int32, (L, L), 0)
    col = lax.broadcasted_iota(jnp.int32, (L, L), 1)
    tril = row >= col
    lane2 = lax.broadcasted_iota(jnp.int32, (L, 2 * P), 1)
    lo = lane2 < P

    for g in range(G):
        bg = b_ref[:, g * N:(g + 1) * N].astype(BF16)
        cg = c_ref[:, g * N:(g + 1) * N].astype(BF16)
        cb = lax.dot_general(cg, bg, (((1,), (1,)), ((), ())), preferred_element_type=F32)
        gs = slice(g * gw, (g + 1) * gw)
        st = st_ref[g]
        y_off = _dot(cg, st.astype(BF16)) * ea_b[:, gs]
        contrib = lax.dot_general(bg, xw[:, gs], (((0,), (0,)), ((), ())),
                                  preferred_element_type=F32)
        st_ref[g] = st * decay_last[:, gs] + contrib
        for pr in range(hpg // 2):
            ws = []
            for hh in range(2):
                hidx = g * hpg + 2 * pr + hh
                seg = acum[:, hidx:hidx + 1] - acumT[hidx:hidx + 1, :]
                decay = jnp.exp(jnp.where(tril, seg, NEG_BIG))
                ws.append((cb * decay * dtT[hidx:hidx + 1, :]).astype(BF16))
            ps = slice(g * gw + pr * 2 * P, g * gw + (pr + 1) * 2 * P)
            x2 = xs16[:, ps]
            y_diag = jnp.where(lo, _dot(ws[0], x2), _dot(ws[1], x2))
            y_ref[:, ps] = y_diag + y_off[:, pr * 2 * P:(pr + 1) * 2 * P] + dskip_ref[:, ps] * xs[:, ps]


def _ssd_scan(xs, bm, cm, dt, alog, dskip, eh, *, n_heads):
    b, s, di = xs.shape
    gn = bm.shape[-1]
    L = SSM_CHUNK
    row = lambda bi, i: (bi, i, 0)
    return pl.pallas_call(
        functools.partial(_ssd_scan_body, n_heads=n_heads),
        grid=(b, s // L),
        in_specs=[
            pl.BlockSpec((None, L, di), row),
            pl.BlockSpec((None, L, gn), row),
            pl.BlockSpec((None, L, gn), row),
            pl.BlockSpec((None, L, LANES), row),
            _const_spec(alog.shape),
            _const_spec(dskip.shape),
            _const_spec(eh.shape),
        ],
        out_specs=pl.BlockSpec((None, L, di), row),
        out_shape=jax.ShapeDtypeStruct((b, s, di), F32),
        scratch_shapes=[pltpu.VMEM((SSM_GROUPS, SSM_STATE, di // SSM_GROUPS), F32)],
        compiler_params=_params(("arbitrary", "arbitrary")),
        name="ssd_scan",
    )(xs, bm, cm, dt, alog, dskip, eh)


def _ssd_out_body(y_ref, z_ref, nw_ref, w_ref, x_ref, o_ref, yn_ref):
    di = y_ref.shape[-1]
    gw = di // SSM_GROUPS
    for g in range(SSM_GROUPS):
        gs = slice(g * gw, (g + 1) * gw)
        yg = y_ref[:, gs] * _silu(z_ref[:, gs])
        yn_ref[:, gs] = _rmsnorm(yg, nw_ref[:, gs]).astype(BF16)
    o_ref[...] = x_ref[...] + _dot(yn_ref[...], w_ref[...])


def _ssd_out(y, z, nw, w, x, *, tm=512):
    m, di = y.shape
    d = w.shape[1]
    return pl.pallas_call(
        _ssd_out_body,
        grid=(m // tm,),
        in_specs=[
            pl.BlockSpec((tm, di), lambda i: (i, 0)),
            pl.BlockSpec((tm, di), lambda i: (i, 0)),
            _const_spec(nw.shape),
            _const_spec(w.shape),
            pl.BlockSpec((tm, d), lambda i: (i, 0)),
        ],
        out_specs=pl.BlockSpec((tm, d), lambda i: (i, 0)),
        out_shape=jax.ShapeDtypeStruct((m, d), F32),
        scratch_shapes=[pltpu.VMEM((tm, di), BF16)],
        compiler_params=_params(("arbitrary",)),
        name="ssd_out",
    )(y, z, nw, w, x)


def _pad_lanes(a, width=LANES):
    return jnp.pad(a, ((0, 0), (0, width - a.shape[-1])))


def _conv_layer(x, nw, w_in, w_dw, w_out):
    return _conv_mixer(x, nw[None], w_in.astype(BF16), w_dw, w_out.astype(BF16))


def _fox_layer(x, nw, w_in, b_f, q_gain, k_gain, w_out, *, t=256):
    b, s, d = x.shape
    n_heads = b_f.shape[0]
    hd = ATTN_HEAD_DIM
    aw = n_heads * hd
    wqkv = w_in[:, :3 * aw].astype(BF16)
    wf = _pad_lanes(w_in[:, 3 * aw:]).astype(BF16)
    bf = _pad_lanes(b_f[None])
    qg = jnp.tile(q_gain, n_heads)[None] * (hd ** -0.5 * LOG2E)
    kg = jnp.tile(k_gain, n_heads)[None]
    smax = 1.02 * LOG2E * hd ** 0.5 * jnp.max(jnp.abs(q_gain)) * jnp.max(jnp.abs(k_gain))
    fast = (2.0 * smax <= FAST_PATH_MAX_LOG2).astype(F32)
    thr = jnp.stack([-(SKIP_LOG2 + 2.0 * smax), fast]).astype(F32)
    head_of = jnp.arange(aw) // hd
    e = (head_of[:, None] == jnp.arange(LANES)[None, :]).astype(F32)
    et = e.T.astype(BF16)
    e = (e / hd).astype(BF16)
    q, k, v, ct = _fox_in(x, nw[None], wqkv, wf, bf, qg, kg, e, et, n_heads=n_heads)
    ck = ct.reshape(b, n_heads // 2, 2, s // t, t).transpose(0, 1, 3, 2, 4)
    cs = ct[:, :, 0::t].reshape(b * n_heads, s // t)
    ce = ct[:, :, t - 1::t].reshape(b * n_heads, s // t)
    attn = _fox_attn(thr, cs, ce, q, k, v, ck, t=t, n_heads=n_heads)
    return _proj_res(attn.reshape(b * s, aw), w_out.astype(BF16), x.reshape(b * s, d)).reshape(b, s, d)


def _ssd_layer(x, nw, w_in, conv_w, conv_b, dt_bias, a_log, d_skip, norm_w, w_out):
    b, s, d = x.shape
    n_heads = a_log.shape[0]
    di = n_heads * SSM_HEAD_DIM
    cdim = conv_w.shape[1]
    wz = w_in[:, :di].astype(BF16)
    wxbc = w_in[:, di:di + cdim].astype(BF16)
    wdt = _pad_lanes(w_in[:, di + cdim:]).astype(BF16)
    z, xs, bm, cm, dt = _ssd_in(x, nw[None], wz, wxbc, wdt, conv_w, conv_b[None],
                                _pad_lanes(dt_bias[None]))
    head_of = jnp.arange(di) // SSM_HEAD_DIM
    eh = (jnp.arange(LANES)[:, None] == head_of[None, :]).astype(BF16)
    dskip = jnp.repeat(d_skip, SSM_HEAD_DIM)[None]
    y = _ssd_scan(xs, bm, cm, dt, _pad_lanes(a_log[None]), dskip, eh, n_heads=n_heads)
    m = b * s
    return _ssd_out(y.reshape(m, di), z.reshape(m, di), norm_w[None], w_out.astype(BF16),
                    x.reshape(m, d)).reshape(b, s, d)


def kernel(x, mix_norm, ffn_norm, ffn_w_gu, ffn_w_down, conv_w_in, conv_w_dw, conv_w_out, fox_w_in, fox_b_f, fox_q_gain, fox_k_gain, fox_w_out, ssd_w_in, ssd_conv_w, ssd_conv_b, ssd_dt_bias, ssd_a_log, ssd_d, ssd_norm_w, ssd_w_out):
    b, s, d = x.shape
    depth = mix_norm.shape[0]
    dff = ffn_w_down.shape[1]
    for i in range(depth):
        kind, j = i % 3, i // 3
        if kind == 0:
            x = _conv_layer(x, mix_norm[i], conv_w_in[j], conv_w_dw[j], conv_w_out[j])
        elif kind == 1:
            x = _fox_layer(x, mix_norm[i], fox_w_in[j], fox_b_f[j], fox_q_gain[j], fox_k_gain[j],
                           fox_w_out[j])
        else:
            x = _ssd_layer(x, mix_norm[i], ssd_w_in[j], ssd_conv_w[j], ssd_conv_b[j],
                           ssd_dt_bias[j], ssd_a_log[j], ssd_d[j], ssd_norm_w[j], ssd_w_out[j])
        wg = ffn_w_gu[i][:, :dff].astype(BF16)
        wu = ffn_w_gu[i][:, dff:].astype(BF16)
        x = _ffn(x.reshape(b * s, d), ffn_norm[i][None], wg, wu,
                 ffn_w_down[i].astype(BF16)).reshape(b, s, d)
    return x
```

```python
import functools

import jax
import jax.numpy as jnp
from jax import lax
from jax.experimental import pallas as pl
from jax.experimental.pallas import tpu as pltpu

F32 = jnp.float32
BF16 = jnp.bfloat16

RMS_EPS = 1e-6
ATTN_HEAD_DIM = 64
SSM_HEAD_DIM = 64
SSM_GROUPS = 8
SSM_STATE = 128
SSM_CHUNK = 128
LANES = 128
NEG_BIG = -1e30
LOG2E = 1.4426950408889634
SKIP_LOG2 = 64.0
FAST_PATH_MAX_LOG2 = 100.0

VMEM_LIMIT = 56 * 1024 * 1024


def _params(sem):
    return pltpu.CompilerParams(dimension_semantics=sem, vmem_limit_bytes=VMEM_LIMIT)


def _const_spec(shape):
    nd = len(shape)
    return pl.BlockSpec(shape, lambda *_: (0,) * nd)


def _rmsnorm(x, w):
    return x * lax.rsqrt(jnp.mean(x * x, axis=-1, keepdims=True) + RMS_EPS) * w


def _dot(a, b):
    return jnp.dot(a, b, preferred_element_type=F32)


def _softplus(x):
    return jnp.maximum(x, 0.0) + jnp.log1p(jnp.exp(-jnp.abs(x)))


def _silu(x):
    return x * jax.nn.sigmoid(x)


def _lane_cumsum(x):
    n = x.shape[-1]
    lane = lax.broadcasted_iota(jnp.int32, x.shape, x.ndim - 1)
    s = 1
    while s < n:
        x = x + jnp.where(lane >= s, pltpu.roll(x, s, x.ndim - 1), 0.0)
        s *= 2
    return x


def _ffn_body(x_ref, nw_ref, wg_ref, wu_ref, wd_ref, o_ref, a_ref, *, chunk):
    x = x_ref[...]
    h = _rmsnorm(x, nw_ref[...]).astype(BF16)
    for c in range(wg_ref.shape[1] // chunk):
        sl = slice(c * chunk, (c + 1) * chunk)
        g = _dot(h, wg_ref[:, sl])
        u = _dot(h, wu_ref[:, sl])
        a_ref[:, sl] = (_silu(g) * u).astype(BF16)
    o_ref[...] = x + _dot(a_ref[...], wd_ref[...])


def _ffn(x, nw, wg, wu, wd, *, tm=512, chunk=256):
    m, d = x.shape
    dff = wg.shape[1]
    return pl.pallas_call(
        functools.partial(_ffn_body, chunk=chunk),
        grid=(m // tm,),
        in_specs=[
            pl.BlockSpec((tm, d), lambda i: (i, 0)),
            _const_spec((1, d)),
            _const_spec((d, dff)),
            _const_spec((d, dff)),
            _const_spec((dff, d)),
        ],
        out_specs=pl.BlockSpec((tm, d), lambda i: (i, 0)),
        out_shape=jax.ShapeDtypeStruct((m, d), F32),
        scratch_shapes=[pltpu.VMEM((tm, dff), BF16)],
        compiler_params=_params(("arbitrary",)),
        name="ffn",
    )(x, nw, wg, wu, wd)


def _conv_mixer_body(x_ref, nw_ref, win_ref, wdw_ref, wout_ref, o_ref, proj_ref, ext_ref, y_ref,
                     *, tm, halo):
    i = pl.program_id(1)
    d = x_ref.shape[-1]
    x = x_ref[...]
    h = _rmsnorm(x, nw_ref[...]).astype(BF16)
    proj_ref[...] = _dot(h, win_ref[...])

    @pl.when(i == 0)
    def _():
        ext_ref[0:halo, :] = jnp.zeros((halo, d), F32)

    @pl.when(i > 0)
    def _():
        ext_ref[0:halo, :] = ext_ref[tm:tm + halo, :]

    kw = wdw_ref.shape[0]
    for c0 in range(0, d, LANES):
        cs = slice(c0, c0 + LANES)
        cv = proj_ref[:, d + c0:d + c0 + LANES] * proj_ref[:, 2 * d + c0:2 * d + c0 + LANES]
        ext_ref[halo:halo + tm, cs] = cv
        u = cv * wdw_ref[kw - 1:kw, cs]
        for k in range(kw - 1):
            off = halo - (kw - 1) + k
            u = u + ext_ref[off:off + tm, cs] * wdw_ref[k:k + 1, cs]
        y_ref[:, cs] = (proj_ref[:, cs] * u).astype(BF16)
    o_ref[...] = x + _dot(y_ref[...], wout_ref[...])


def _conv_mixer(x, nw, win, wdw, wout, *, tm=512):
    b, s, d = x.shape
    halo = 8
    return pl.pallas_call(
        functools.partial(_conv_mixer_body, tm=tm, halo=halo),
        grid=(b, s // tm),
        in_specs=[
            pl.BlockSpec((None, tm, d), lambda bi, i: (bi, i, 0)),
            _const_spec((1, d)),
            _const_spec(win.shape),
            _const_spec(wdw.shape),
            _const_spec(wout.shape),
        ],
        out_specs=pl.BlockSpec((None, tm, d), lambda bi, i: (bi, i, 0)),
        out_shape=jax.ShapeDtypeStruct((b, s, d), F32),
        scratch_shapes=[pltpu.VMEM((tm, 3 * d), F32), pltpu.VMEM((halo + tm, d), F32),
                        pltpu.VMEM((tm, d), BF16)],
        compiler_params=_params(("arbitrary", "arbitrary")),
        name="conv_mixer",
    )(x, nw, win, wdw, wout)


def _fox_in_body(x_ref, nw_ref, wqkv_ref, wf_ref, bf_ref, qg_ref, kg_ref, e_ref, et_ref,
                 q_ref, k_ref, v_ref, ct_ref, carry_ref, *, n_heads):
    i = pl.program_id(1)
    aw = q_ref.shape[-1]
    x = x_ref[...]
    h = _rmsnorm(x, nw_ref[...]).astype(BF16)
    qkv = _dot(h, wqkv_ref[...])

    def head_norm(t, gain):
        ms = _dot((t * t).astype(BF16), e_ref[...])
        r = lax.rsqrt(ms + RMS_EPS)
        r_hi = r.astype(BF16)
        r_lo = (r - r_hi.astype(F32)).astype(BF16)
        rb = _dot(r_hi, et_ref[...]) + _dot(r_lo, et_ref[...])
        return t * rb * gain

    q_ref[...] = head_norm(qkv[:, :aw], qg_ref[...]).astype(BF16)
    k_ref[...] = head_norm(qkv[:, aw:2 * aw], kg_ref[...]).astype(BF16)
    v_ref[...] = qkv[:, 2 * aw:].astype(BF16)

    fl = _dot(h, wf_ref[...]) + bf_ref[...]
    logf = -_softplus(-fl) * LOG2E
    local = _lane_cumsum(logf.T[0:n_heads, :])

    @pl.when(i == 0)
    def _():
        carry_ref[...] = jnp.zeros_like(carry_ref)

    cum = local + carry_ref[:, 0:1]
    ct_ref[...] = cum
    tm = cum.shape[-1]
    carry_ref[...] = jnp.broadcast_to(cum[:, tm - 1:tm], carry_ref.shape)


def _fox_in(x, nw, wqkv, wf, bf, qg, kg, e, et, *, n_heads, tm=512):
    b, s, d = x.shape
    aw = wqkv.shape[1] // 3
    row = lambda bi, i: (bi, i, 0)
    return pl.pallas_call(
        functools.partial(_fox_in_body, n_heads=n_heads),
        grid=(b, s // tm),
        in_specs=[
            pl.BlockSpec((None, tm, d), row),
            _const_spec((1, d)),
            _const_spec(wqkv.shape),
            _const_spec(wf.shape),
            _const_spec(bf.shape),
            _const_spec(qg.shape),
            _const_spec(kg.shape),
            _const_spec(e.shape),
            _const_spec(et.shape),
        ],
        out_specs=[
            pl.BlockSpec((None, tm, aw), row),
            pl.BlockSpec((None, tm, aw), row),
            pl.BlockSpec((None, tm, aw), row),
            pl.BlockSpec((None, n_heads, tm), lambda bi, i: (bi, 0, i)),
        ],
        out_shape=[
            jax.ShapeDtypeStruct((b, s, aw), BF16),
            jax.ShapeDtypeStruct((b, s, aw), BF16),
            jax.ShapeDtypeStruct((b, s, aw), BF16),
            jax.ShapeDtypeStruct((b, n_heads, s), F32),
        ],
        scratch_shapes=[pltpu.VMEM((n_heads, LANES), F32)],
        compiler_params=_params(("arbitrary", "arbitrary")),
        name="fox_in",
    )(x, nw, wqkv, wf, bf, qg, kg, e, et)


def _fox_attn_body(thr_ref, cs_ref, ce_ref, q_ref, k_ref, v_ref, ck_ref, o_ref, *, t, n_heads):
    bi, pr, i = pl.program_id(0), pl.program_id(1), pl.program_id(2)
    hd = ATTN_HEAD_DIM
    q2 = q_ref[...]
    lane = lax.broadcasted_iota(jnp.int32, (t, 2 * hd), 1)
    lo = lane < hd
    zero = jnp.zeros_like(q2)
    qs = (jnp.where(lo, q2, zero), jnp.where(lo, zero, q2))

    r0 = bi * n_heads + 2 * pr
    thr = thr_ref[0]
    cs0, cs1 = cs_ref[r0, i], cs_ref[r0 + 1, i]

    def needed(j):
        jj = jnp.maximum(j, 0)
        near = (cs0 - ce_ref[r0, jj] >= thr) | (cs1 - ce_ref[r0 + 1, jj] >= thr)
        return (j >= 0) & near

    j0 = lax.while_loop(needed, lambda j: j - 1, i - 1) + 1

    def scores(j):
        k2 = k_ref[pl.ds(pl.multiple_of(j * t, t), t), :]
        ck = ck_ref[j]
        return tuple(lax.dot_general(qs[hh], k2, (((1,), (1,)), ((), ())),
                                     preferred_element_type=F32) - ck[hh:hh + 1, :]
                     for hh in range(2))

    def update(j, s, ms, ls, acc, masked):
        v2 = v_ref[pl.ds(pl.multiple_of(j * t, t), t), :]
        if masked:
            row = lax.broadcasted_iota(jnp.int32, (t, t), 0)
            col = lax.broadcasted_iota(jnp.int32, (t, t), 1)
            s = tuple(jnp.where(row >= col, sh, NEG_BIG) for sh in s)
        new_m, new_l, alphas, pvs = [], [], [], []
        for hh in range(2):
            m_new = jnp.maximum(ms[hh], jnp.max(s[hh], axis=1, keepdims=True))
            alpha = jnp.exp2(ms[hh] - m_new)
            p = jnp.exp2(s[hh] - m_new)
            new_l.append(alpha * ls[hh] + jnp.sum(p, axis=1, keepdims=True))
            new_m.append(m_new)
            alphas.append(alpha)
            pvs.append(_dot(p.astype(BF16), v2))
        acc = acc * jnp.where(lo, alphas[0], alphas[1]) + jnp.where(lo, pvs[0], pvs[1])
        return tuple(new_m), tuple(new_l), acc

    @pl.when(thr_ref[1] <= 0.0)
    def _():
        def body(j, carry):
            s, ms, ls, acc = carry
            s_next = scores(j + 1)
            ms, ls, acc = update(j, s, ms, ls, acc, False)
            return s_next, ms, ls, acc

        m0 = jnp.full((t, 1), NEG_BIG, F32)
        l0 = jnp.zeros((t, 1), F32)
        carry = (scores(j0), (m0, m0), (l0, l0), jnp.zeros((t, 2 * hd), F32))
        s, ms, ls, acc = lax.fori_loop(j0, i, body, carry)
        _, ls, acc = update(i, s, ms, ls, acc, True)
        o_ref[...] = (acc / jnp.where(lo, ls[0], ls[1])).astype(o_ref.dtype)

    @pl.when(thr_ref[1] > 0.0)
    def _():
        e_lo = jnp.where(lo, 1.0, 0.0).astype(BF16)
        e_hi = jnp.where(lo, 0.0, 1.0).astype(BF16)

        def pv(j, p16):
            v2 = v_ref[pl.ds(pl.multiple_of(j * t, t), t), :]
            vb = jnp.concatenate(
                [jnp.concatenate([v2 * e_lo, e_lo], axis=1),
                 jnp.concatenate([v2 * e_hi, e_hi], axis=1)], axis=0)
            return _dot(p16, vb)

        def weights(s):
            return jnp.concatenate([jnp.exp2(s[hh] - ms[hh]).astype(BF16) for hh in range(2)], axis=1)

        row = lax.broadcasted_iota(jnp.int32, (t, t), 0)
        col = lax.broadcasted_iota(jnp.int32, (t, t), 1)
        sd = tuple(jnp.where(row >= col, sh, NEG_BIG) for sh in scores(i))
        ms = tuple(jnp.max(sh, axis=1, keepdims=True) for sh in sd)

        def body(j, carry):
            jprev, p16, acc = carry
            p_new = weights(scores(j))
            return j, p_new, acc + pv(jprev, p16)

        carry = (i, weights(sd), jnp.zeros((t, 4 * hd), F32))
        jprev, p16, acc = lax.fori_loop(j0, i, body, carry)
        acc = acc + pv(jprev, p16)
        o_ref[...] = (acc[:, :2 * hd] / acc[:, 2 * hd:]).astype(o_ref.dtype)


def _fox_attn(thr, cs, ce, q, k, v, ck, *, t, n_heads):
    b, s, aw = q.shape
    pairs = aw // LANES
    smem = pl.BlockSpec(memory_space=pltpu.SMEM)
    return pl.pallas_call(
        functools.partial(_fox_attn_body, t=t, n_heads=n_heads),
        grid=(b, pairs, s // t),
        in_specs=[
            smem, smem, smem,
            pl.BlockSpec((None, t, LANES), lambda bi, p, i: (bi, i, p)),
            pl.BlockSpec((None, s, LANES), lambda bi, p, i: (bi, 0, p)),
            pl.BlockSpec((None, s, LANES), lambda bi, p, i: (bi, 0, p)),
            pl.BlockSpec((None, None, s // t, 2, t), lambda bi, p, i: (bi, p, 0, 0, 0)),
        ],
        out_specs=pl.BlockSpec((None, t, LANES), lambda bi, p, i: (bi, i, p)),
        out_shape=jax.ShapeDtypeStruct((b, s, aw), BF16),
        compiler_params=_params(("arbitrary", "arbitrary", "arbitrary")),
        name="fox_attn",
    )(thr, cs, ce, q, k, v, ck)


def _proj_res_body(y_ref, w_ref, x_ref, o_ref):
    o_ref[...] = x_ref[...] + _dot(y_ref[...], w_ref[...])


def _proj_res(y, w, x, *, tm=512):
    m, kdim = y.shape
    d = w.shape[1]
    return pl.pallas_call(
        _proj_res_body,
        grid=(m // tm,),
        in_specs=[
            pl.BlockSpec((tm, kdim), lambda i: (i, 0)),
            _const_spec(w.shape),
            pl.BlockSpec((tm, d), lambda i: (i, 0)),
        ],
        out_specs=pl.BlockSpec((tm, d), lambda i: (i, 0)),
        out_shape=jax.ShapeDtypeStruct((m, d), F32),
        compiler_params=_params(("arbitrary",)),
        name="proj_res",
    )(y, w, x)


def _ssd_in_body(x_ref, nw_ref, wz_ref, wxbc_ref, wdt_ref, cw_ref, cb_ref, dtb_ref,
                 z_ref, xs_ref, b_ref, c_ref, dt_ref, ext_ref, *, tm, halo, chunk):
    i = pl.program_id(1)
    di = xs_ref.shape[-1]
    gn = b_ref.shape[-1]
    cdim = wxbc_ref.shape[-1]
    x = x_ref[...]
    h = _rmsnorm(x, nw_ref[...]).astype(BF16)

    @pl.when(i == 0)
    def _():
        ext_ref[:, 0:halo, :] = jnp.zeros((cdim // chunk, halo, chunk), F32)

    @pl.when(i > 0)
    def _():
        ext_ref[:, 0:halo, :] = ext_ref[:, tm:tm + halo, :]

    dt_ref[...] = _softplus(_dot(h, wdt_ref[...]) + dtb_ref[...])
    kw = cw_ref.shape[0]
    outs = ((xs_ref, 0, di), (b_ref, di, di + gn), (c_ref, di + gn, cdim))
    for o_ref, lo_c, hi_c in outs:
        for q0 in range(lo_c, hi_c, chunk):
            q = q0 // chunk
            ext_ref[q, halo:halo + tm, :] = _dot(h, wxbc_ref[:, q0:q0 + chunk])
            if q0 < di:
                z_ref[:, q0:q0 + chunk] = _dot(h, wz_ref[:, q0:q0 + chunk]).astype(z_ref.dtype)
            for c0 in range(q0, q0 + chunk, LANES):
                cs = slice(c0, c0 + LANES)
                es = slice(c0 - q0, c0 - q0 + LANES)
                u = ext_ref[q, halo:halo + tm, es] * cw_ref[kw - 1:kw, cs] + cb_ref[:, cs]
                for k in range(kw - 1):
                    off = halo - (kw - 1) + k
                    u = u + ext_ref[q, off:off + tm, es] * cw_ref[k:k + 1, cs]
                o_ref[:, c0 - lo_c:c0 - lo_c + LANES] = _silu(u).astype(o_ref.dtype)


def _ssd_in(x, nw, wz, wxbc, wdt, cw, cb, dtb, *, tm=256):
    b, s, d = x.shape
    di = wz.shape[1]
    cdim = wxbc.shape[1]
    gn = (cdim - di) // 2
    halo = 8
    chunk = 512
    row = lambda bi, i: (bi, i, 0)
    return pl.pallas_call(
        functools.partial(_ssd_in_body, tm=tm, halo=halo, chunk=chunk),
        grid=(b, s // tm),
        in_specs=[
            pl.BlockSpec((None, tm, d), row),
            _const_spec((1, d)),
            _const_spec(wz.shape),
            _const_spec(wxbc.shape),
            _const_spec(wdt.shape),
            _const_spec(cw.shape),
            _const_spec(cb.shape),
            _const_spec(dtb.shape),
        ],
        out_specs=[
            pl.BlockSpec((None, tm, di), row),
            pl.BlockSpec((None, tm, di), row),
            pl.BlockSpec((None, tm, gn), row),
            pl.BlockSpec((None, tm, gn), row),
            pl.BlockSpec((None, tm, LANES), row),
        ],
        out_shape=[
            jax.ShapeDtypeStruct((b, s, di), BF16),
            jax.ShapeDtypeStruct((b, s, di), BF16),
            jax.ShapeDtypeStruct((b, s, gn), BF16),
            jax.ShapeDtypeStruct((b, s, gn), BF16),
            jax.ShapeDtypeStruct((b, s, LANES), F32),
        ],
        scratch_shapes=[pltpu.VMEM((cdim // chunk, halo + tm, chunk), F32)],
        compiler_params=_params(("arbitrary", "arbitrary")),
        name="ssd_in",
    )(x, nw, wz, wxbc, wdt, cw, cb, dtb)


def _ssd_scan_body(xs_ref, b_ref, c_ref, dt_ref, alog_ref, dskip_ref, eh_ref, y_ref, st_ref,
                   *, n_heads):
    ci = pl.program_id(1)
    L = xs_ref.shape[0]
    P, N, G = SSM_HEAD_DIM, SSM_STATE, SSM_GROUPS
    hpg = n_heads // G
    gw = hpg * P

    @pl.when(ci == 0)
    def _():
        st_ref[...] = jnp.zeros_like(st_ref)

    lane_h = lax.broadcasted_iota(jnp.int32, (1, LANES), 1)
    a = jnp.where(lane_h < n_heads, -jnp.exp(alog_ref[...]), 0.0)
    dt = dt_ref[...]
    dtT = dt.T
    acumT = _lane_cumsum((dt * a).T)
    acum = acumT.T
    a_last = acum[L - 1:L, :]

    eh = eh_ref[...]
    sdt_b = _dot((jnp.exp(a_last - acum) * dt).astype(BF16), eh)
    ea_b = _dot(jnp.exp(acum).astype(BF16), eh)
    dl = jnp.broadcast_to(jnp.exp(a_last), (16, LANES))
    dl_hi = dl.astype(BF16)
    dl_lo = (dl - dl_hi.astype(F32)).astype(BF16)
    decay_last = (_dot(dl_hi, eh) + _dot(dl_lo, eh))[0:1, :]
    xs16 = xs_ref[...]
    xs = xs16.astype(F32)
    xw = (xs * sdt_b).astype(BF16)

    row = lax.broadcasted_iota(jnp.int32, (L, L), 0)
    col = lax.broadcasted_iota(jnp.int32, (L, L), 1)
    tril = row >= col
    head_of_lane = lax.broadcasted_iota(jnp.int32, (L, gw), 1) // P
    head_mask = [jnp.where(head_of_lane == k, 1.0, 0.0).astype(BF16) for k in range(hpg)]

    for g in range(G):
        bg = b_ref[:, g * N:(g + 1) * N]
        cg = c_ref[:, g * N:(g + 1) * N]
        cb = lax.dot_general(cg, bg, (((1,), (1,)), ((), ())), preferred_element_type=F32)
        gs = slice(g * gw, (g + 1) * gw)
        st = st_ref[g]
        y_off = _dot(cg, st.astype(BF16)) * ea_b[:, gs]
        contrib = lax.dot_general(bg, xw[:, gs], (((0,), (0,)), ((), ())),
                                  preferred_element_type=F32)
        st_ref[g] = st * decay_last[:, gs] + contrib
        ws = []
        for k in range(hpg):
            hidx = g * hpg + k
            seg = acum[:, hidx:hidx + 1] - acumT[hidx:hidx + 1, :]
            decay = jnp.exp(jnp.where(tril, seg, NEG_BIG))
            ws.append((cb * decay * dtT[hidx:hidx + 1, :]).astype(BF16))
        xg = xs16[:, gs]
        xblk = jnp.concatenate([xg * head_mask[k] for k in range(hpg)], axis=0)
        y_diag = _dot(jnp.concatenate(ws, axis=1), xblk)
        y_ref[:, gs] = (y_diag + y_off + dskip_ref[:, gs] * xs[:, gs]).astype(y_ref.dtype)


def _ssd_scan(xs, bm, cm, dt, alog, dskip, eh, *, n_heads):
    b, s, di = xs.shape
    gn = bm.shape[-1]
    L = SSM_CHUNK
    row = lambda bi, i: (bi, i, 0)
    return pl.pallas_call(
        functools.partial(_ssd_scan_body, n_heads=n_heads),
        grid=(b, s // L),
        in_specs=[
            pl.BlockSpec((None, L, di), row),
            pl.BlockSpec((None, L, gn), row),
            pl.BlockSpec((None, L, gn), row),
            pl.BlockSpec((None, L, LANES), row),
            _const_spec(alog.shape),
            _const_spec(dskip.shape),
            _const_spec(eh.shape),
        ],
        out_specs=pl.BlockSpec((None, L, di), row),
        out_shape=jax.ShapeDtypeStruct((b, s, di), BF16),
        scratch_shapes=[pltpu.VMEM((SSM_GROUPS, SSM_STATE, di // SSM_GROUPS), F32)],
        compiler_params=_params(("arbitrary", "arbitrary")),
        name="ssd_scan",
    )(xs, bm, cm, dt, alog, dskip, eh)


def _ssd_out_body(y_ref, z_ref, nw_ref, w_ref, x_ref, o_ref, yn_ref):
    di = y_ref.shape[-1]
    gw = di // SSM_GROUPS
    for g in range(SSM_GROUPS):
        gs = slice(g * gw, (g + 1) * gw)
        yg = y_ref[:, gs].astype(F32) * _silu(z_ref[:, gs].astype(F32))
        yn_ref[:, gs] = _rmsnorm(yg, nw_ref[:, gs]).astype(BF16)
    o_ref[...] = x_ref[...] + _dot(yn_ref[...], w_ref[...])


def _ssd_out(y, z, nw, w, x, *, tm=512):
    m, di = y.shape
    d = w.shape[1]
    return pl.pallas_call(
        _ssd_out_body,
        grid=(m // tm,),
        in_specs=[
            pl.BlockSpec((tm, di), lambda i: (i, 0)),
            pl.BlockSpec((tm, di), lambda i: (i, 0)),
            _const_spec(nw.shape),
            _const_spec(w.shape),
            pl.BlockSpec((tm, d), lambda i: (i, 0)),
        ],
        out_specs=pl.BlockSpec((tm, d), lambda i: (i, 0)),
        out_shape=jax.ShapeDtypeStruct((m, d), F32),
        scratch_shapes=[pltpu.VMEM((tm, di), BF16)],
        compiler_params=_params(("arbitrary",)),
        name="ssd_out",
    )(y, z, nw, w, x)


def _pad_lanes(a, width=LANES):
    return jnp.pad(a, ((0, 0), (0, width - a.shape[-1])))


def _conv_layer(x, nw, w_in, w_dw, w_out):
    return _conv_mixer(x, nw[None], w_in.astype(BF16), w_dw, w_out.astype(BF16))


def _fox_layer(x, nw, w_in, b_f, q_gain, k_gain, w_out, *, t=256):
    b, s, d = x.shape
    n_heads = b_f.shape[0]
    hd = ATTN_HEAD_DIM
    aw = n_heads * hd
    wqkv = w_in[:, :3 * aw].astype(BF16)
    wf = _pad_lanes(w_in[:, 3 * aw:]).astype(BF16)
    bf = _pad_lanes(b_f[None])
    qg = jnp.tile(q_gain, n_heads)[None] * (hd ** -0.5 * LOG2E)
    kg = jnp.tile(k_gain, n_heads)[None]
    smax = 1.02 * LOG2E * hd ** 0.5 * jnp.max(jnp.abs(q_gain)) * jnp.max(jnp.abs(k_gain))
    fast = (2.0 * smax <= FAST_PATH_MAX_LOG2).astype(F32)
    thr = jnp.stack([-(SKIP_LOG2 + 2.0 * smax), fast]).astype(F32)
    head_of = jnp.arange(aw) // hd
    e = (head_of[:, None] == jnp.arange(LANES)[None, :]).astype(F32)
    et = e.T.astype(BF16)
    e = (e / hd).astype(BF16)
    q, k, v, ct = _fox_in(x, nw[None], wqkv, wf, bf, qg, kg, e, et, n_heads=n_heads)
    ck = ct.reshape(b, n_heads // 2, 2, s // t, t).transpose(0, 1, 3, 2, 4)
    cs = ct[:, :, 0::t].reshape(b * n_heads, s // t)
    ce = ct[:, :, t - 1::t].reshape(b * n_heads, s // t)
    attn = _fox_attn(thr, cs, ce, q, k, v, ck, t=t, n_heads=n_heads)
    return _proj_res(attn.reshape(b * s, aw), w_out.astype(BF16), x.reshape(b * s, d)).reshape(b, s, d)


def _ssd_layer(x, nw, w_in, conv_w, conv_b, dt_bias, a_log, d_skip, norm_w, w_out):
    b, s, d = x.shape
    n_heads = a_log.shape[0]
    di = n_heads * SSM_HEAD_DIM
    cdim = conv_w.shape[1]
    wz = w_in[:, :di].astype(BF16)
    wxbc = w_in[:, di:di + cdim].astype(BF16)
    wdt = _pad_lanes(w_in[:, di + cdim:]).astype(BF16)
    z, xs, bm, cm, dt = _ssd_in(x, nw[None], wz, wxbc, wdt, conv_w, conv_b[None],
                                _pad_lanes(dt_bias[None]))
    head_of = jnp.arange(di) // SSM_HEAD_DIM
    eh = (jnp.arange(LANES)[:, None] == head_of[None, :]).astype(BF16)
    dskip = jnp.repeat(d_skip, SSM_HEAD_DIM)[None]
    y = _ssd_scan(xs, bm, cm, dt, _pad_lanes(a_log[None]), dskip, eh, n_heads=n_heads)
    m = b * s
    return _ssd_out(y.reshape(m, di), z.reshape(m, di), norm_w[None], w_out.astype(BF16),
                    x.reshape(m, d)).reshape(b, s, d)


def kernel(x, mix_norm, ffn_norm, ffn_w_gu, ffn_w_down, conv_w_in, conv_w_dw, conv_w_out, fox_w_in, fox_b_f, fox_q_gain, fox_k_gain, fox_w_out, ssd_w_in, ssd_conv_w, ssd_conv_b, ssd_dt_bias, ssd_a_log, ssd_d, ssd_norm_w, ssd_w_out):
    b, s, d = x.shape
    depth = mix_norm.shape[0]
    dff = ffn_w_down.shape[1]
    for i in range(depth):
        kind, j = i % 3, i // 3
        if kind == 0:
            x = _conv_layer(x, mix_norm[i], conv_w_in[j], conv_w_dw[j], conv_w_out[j])
        elif kind == 1:
            x = _fox_layer(x, mix_norm[i], fox_w_in[j], fox_b_f[j], fox_q_gain[j], fox_k_gain[j],
                           fox_w_out[j])
        else:
            x = _ssd_layer(x, mix_norm[i], ssd_w_in[j], ssd_conv_w[j], ssd_conv_b[j],
                           ssd_dt_bias[j], ssd_a_log[j], ssd_d[j], ssd_norm_w[j], ssd_w_out[j])
        wg = ffn_w_gu[i][:, :dff].astype(BF16)
        wu = ffn_w_gu[i][:, dff:].astype(BF16)
        x = _ffn(x.reshape(b * s, d), ffn_norm[i][None], wg, wu,
                 ffn_w_down[i].astype(BF16)).reshape(b, s, d)
    return x
```

```python
import functools

import jax
import jax.numpy as jnp
from jax import lax
from jax.experimental import pallas as pl
from jax.experimental.pallas import tpu as pltpu

F32 = jnp.float32
BF16 = jnp.bfloat16

RMS_EPS = 1e-6
ATTN_HEAD_DIM = 64
SSM_HEAD_DIM = 64
SSM_GROUPS = 8
SSM_STATE = 128
SSM_CHUNK = 128
LANES = 128
NEG_BIG = -1e30
LOG2E = 1.4426950408889634
SKIP_LOG2 = 64.0
FAST_PATH_MAX_LOG2 = 100.0

VMEM_LIMIT = 56 * 1024 * 1024


def _params(sem):
    return pltpu.CompilerParams(dimension_semantics=sem, vmem_limit_bytes=VMEM_LIMIT)


def _const_spec(shape):
    nd = len(shape)
    return pl.BlockSpec(shape, lambda *_: (0,) * nd)


def _rmsnorm(x, w):
    return x * lax.rsqrt(jnp.mean(x * x, axis=-1, keepdims=True) + RMS_EPS) * w


def _dot(a, b):
    return jnp.dot(a, b, preferred_element_type=F32)


def _softplus(x):
    return jnp.maximum(x, 0.0) + jnp.log1p(jnp.exp(-jnp.abs(x)))


def _silu(x):
    return x * jax.nn.sigmoid(x)


def _lane_cumsum(x):
    n = x.shape[-1]
    lane = lax.broadcasted_iota(jnp.int32, x.shape, x.ndim - 1)
    s = 1
    while s < n:
        x = x + jnp.where(lane >= s, pltpu.roll(x, s, x.ndim - 1), 0.0)
        s *= 2
    return x


def _ffn_body(x_ref, nw_ref, wg_ref, wu_ref, wd_ref, o_ref, a_ref, *, chunk):
    x = x_ref[...]
    h = _rmsnorm(x, nw_ref[...]).astype(BF16)
    for c in range(wg_ref.shape[1] // chunk):
        sl = slice(c * chunk, (c + 1) * chunk)
        g = _dot(h, wg_ref[:, sl])
        u = _dot(h, wu_ref[:, sl])
        a_ref[:, sl] = (_silu(g) * u).astype(BF16)
    o_ref[...] = x + _dot(a_ref[...], wd_ref[...])


def _ffn(x, nw, wg, wu, wd, *, tm=512, chunk=256):
    m, d = x.shape
    dff = wg.shape[1]
    return pl.pallas_call(
        functools.partial(_ffn_body, chunk=chunk),
        grid=(m // tm,),
        in_specs=[
            pl.BlockSpec((tm, d), lambda i: (i, 0)),
            _const_spec((1, d)),
            _const_spec((d, dff)),
            _const_spec((d, dff)),
            _const_spec((dff, d)),
        ],
        out_specs=pl.BlockSpec((tm, d), lambda i: (i, 0)),
        out_shape=jax.ShapeDtypeStruct((m, d), F32),
        scratch_shapes=[pltpu.VMEM((tm, dff), BF16)],
        compiler_params=_params(("arbitrary",)),
        name="ffn",
    )(x, nw, wg, wu, wd)


def _conv_mixer_body(x_ref, nw_ref, win_ref, wdw_ref, wout_ref, o_ref, ext_ref, y_ref,
                     *, tm, halo, chunk):
    i = pl.program_id(1)
    d = x_ref.shape[-1]
    x = x_ref[...]
    h = _rmsnorm(x, nw_ref[...]).astype(BF16)

    @pl.when(i == 0)
    def _():
        ext_ref[:, 0:halo, :] = jnp.zeros((d // LANES, halo, LANES), F32)

    @pl.when(i > 0)
    def _():
        ext_ref[:, 0:halo, :] = ext_ref[:, tm:tm + halo, :]

    kw = wdw_ref.shape[0]
    for q0 in range(0, d, chunk):
        bq = _dot(h, win_ref[:, q0:q0 + chunk])
        cq = _dot(h, win_ref[:, d + q0:d + q0 + chunk])
        vq = _dot(h, win_ref[:, 2 * d + q0:2 * d + q0 + chunk])
        for r0 in range(0, chunk, LANES):
            c0 = q0 + r0
            sl = c0 // LANES
            cs = slice(c0, c0 + LANES)
            rs = slice(r0, r0 + LANES)
            cv = cq[:, rs] * vq[:, rs]
            ext_ref[sl, halo:halo + tm, :] = cv
            u = cv * wdw_ref[kw - 1:kw, cs]
            for k in range(kw - 1):
                off = halo - (kw - 1) + k
                u = u + ext_ref[sl, off:off + tm, :] * wdw_ref[k:k + 1, cs]
            y_ref[:, cs] = (bq[:, rs] * u).astype(BF16)
    o_ref[...] = x + _dot(y_ref[...], wout_ref[...])


def _conv_mixer(x, nw, win, wdw, wout, *, tm=512):
    b, s, d = x.shape
    halo = 8
    return pl.pallas_call(
        functools.partial(_conv_mixer_body, tm=tm, halo=halo, chunk=256),
        grid=(b, s // tm),
        in_specs=[
            pl.BlockSpec((None, tm, d), lambda bi, i: (bi, i, 0)),
            _const_spec((1, d)),
            _const_spec(win.shape),
            _const_spec(wdw.shape),
            _const_spec(wout.shape),
        ],
        out_specs=pl.BlockSpec((None, tm, d), lambda bi, i: (bi, i, 0)),
        out_shape=jax.ShapeDtypeStruct((b, s, d), F32),
        scratch_shapes=[pltpu.VMEM((d // LANES, halo + tm, LANES), F32),
                        pltpu.VMEM((tm, d), BF16)],
        compiler_params=_params(("arbitrary", "arbitrary")),
        name="conv_mixer",
    )(x, nw, win, wdw, wout)


def _fox_in_body(x_ref, nw_ref, wqkv_ref, wf_ref, bf_ref, qg_ref, kg_ref, e_ref, et_ref,
                 q_ref, k_ref, v_ref, ct_ref, carry_ref, *, n_heads):
    i = pl.program_id(1)
    aw = q_ref.shape[-1]
    x = x_ref[...]
    h = _rmsnorm(x, nw_ref[...]).astype(BF16)
    qkv = _dot(h, wqkv_ref[...])

    def head_norm(t, gain):
        ms = _dot((t * t).astype(BF16), e_ref[...])
        r = lax.rsqrt(ms + RMS_EPS)
        r_hi = r.astype(BF16)
        r_lo = (r - r_hi.astype(F32)).astype(BF16)
        rb = _dot(r_hi, et_ref[...]) + _dot(r_lo, et_ref[...])
        return t * rb * gain

    q_ref[...] = head_norm(qkv[:, :aw], qg_ref[...]).astype(BF16)
    k_ref[...] = head_norm(qkv[:, aw:2 * aw], kg_ref[...]).astype(BF16)
    v_ref[...] = qkv[:, 2 * aw:].astype(BF16)

    fl = _dot(h, wf_ref[...]) + bf_ref[...]
    logf = -_softplus(-fl) * LOG2E
    local = _lane_cumsum(logf.T[0:n_heads, :])

    @pl.when(i == 0)
    def _():
        carry_ref[...] = jnp.zeros_like(carry_ref)

    cum = local + carry_ref[:, 0:1]
    ct_ref[...] = cum
    tm = cum.shape[-1]
    carry_ref[...] = jnp.broadcast_to(cum[:, tm - 1:tm], carry_ref.shape)


def _fox_in(x, nw, wqkv, wf, bf, qg, kg, e, et, *, n_heads, tm=512):
    b, s, d = x.shape
    aw = wqkv.shape[1] // 3
    row = lambda bi, i: (bi, i, 0)
    return pl.pallas_call(
        functools.partial(_fox_in_body, n_heads=n_heads),
        grid=(b, s // tm),
        in_specs=[
            pl.BlockSpec((None, tm, d), row),
            _const_spec((1, d)),
            _const_spec(wqkv.shape),
            _const_spec(wf.shape),
            _const_spec(bf.shape),
            _const_spec(qg.shape),
            _const_spec(kg.shape),
            _const_spec(e.shape),
            _const_spec(et.shape),
        ],
        out_specs=[
            pl.BlockSpec((None, tm, aw), row),
            pl.BlockSpec((None, tm, aw), row),
            pl.BlockSpec((None, tm, aw), row),
            pl.BlockSpec((None, n_heads, tm), lambda bi, i: (bi, 0, i)),
        ],
        out_shape=[
            jax.ShapeDtypeStruct((b, s, aw), BF16),
            jax.ShapeDtypeStruct((b, s, aw), BF16),
            jax.ShapeDtypeStruct((b, s, aw), BF16),
            jax.ShapeDtypeStruct((b, n_heads, s), F32),
        ],
        scratch_shapes=[pltpu.VMEM((n_heads, LANES), F32)],
        compiler_params=_params(("arbitrary", "arbitrary")),
        name="fox_in",
    )(x, nw, wqkv, wf, bf, qg, kg, e, et)


def _fox_attn_body(thr_ref, cs_ref, ce_ref, q_ref, k_ref, v_ref, ck_ref, o_ref, *, t, n_heads):
    bi, pr, i = pl.program_id(0), pl.program_id(1), pl.program_id(2)
    hd = ATTN_HEAD_DIM
    q2 = q_ref[...]
    lane = lax.broadcasted_iota(jnp.int32, (t, 2 * hd), 1)
    lo = lane < hd
    zero = jnp.zeros_like(q2)
    qs = (jnp.where(lo, q2, zero), jnp.where(lo, zero, q2))

    r0 = bi * n_heads + 2 * pr
    thr = thr_ref[0]
    cs0, cs1 = cs_ref[r0, i], cs_ref[r0 + 1, i]

    def needed(j):
        jj = jnp.maximum(j, 0)
        near = (cs0 - ce_ref[r0, jj] >= thr) | (cs1 - ce_ref[r0 + 1, jj] >= thr)
        return (j >= 0) & near

    j0 = lax.while_loop(needed, lambda j: j - 1, i - 1) + 1

    def scores(j):
        k2 = k_ref[pl.ds(pl.multiple_of(j * t, t), t), :]
        ck = ck_ref[j]
        return tuple(lax.dot_general(qs[hh], k2, (((1,), (1,)), ((), ())),
                                     preferred_element_type=F32) - ck[hh:hh + 1, :]
                     for hh in range(2))

    def update(j, s, ms, ls, acc, masked):
        v2 = v_ref[pl.ds(pl.multiple_of(j * t, t), t), :]
        if masked:
            row = lax.broadcasted_iota(jnp.int32, (t, t), 0)
            col = lax.broadcasted_iota(jnp.int32, (t, t), 1)
            s = tuple(jnp.where(row >= col, sh, NEG_BIG) for sh in s)
        new_m, new_l, alphas, pvs = [], [], [], []
        for hh in range(2):
            m_new = jnp.maximum(ms[hh], jnp.max(s[hh], axis=1, keepdims=True))
            alpha = jnp.exp2(ms[hh] - m_new)
            p = jnp.exp2(s[hh] - m_new)
            new_l.append(alpha * ls[hh] + jnp.sum(p, axis=1, keepdims=True))
            new_m.append(m_new)
            alphas.append(alpha)
            pvs.append(_dot(p.astype(BF16), v2))
        acc = acc * jnp.where(lo, alphas[0], alphas[1]) + jnp.where(lo, pvs[0], pvs[1])
        return tuple(new_m), tuple(new_l), acc

    @pl.when(thr_ref[1] <= 0.0)
    def _():
        def body(j, carry):
            s, ms, ls, acc = carry
            s_next = scores(j + 1)
            ms, ls, acc = update(j, s, ms, ls, acc, False)
            return s_next, ms, ls, acc

        m0 = jnp.full((t, 1), NEG_BIG, F32)
        l0 = jnp.zeros((t, 1), F32)
        carry = (scores(j0), (m0, m0), (l0, l0), jnp.zeros((t, 2 * hd), F32))
        s, ms, ls, acc = lax.fori_loop(j0, i, body, carry)
        _, ls, acc = update(i, s, ms, ls, acc, True)
        o_ref[...] = (acc / jnp.where(lo, ls[0], ls[1])).astype(o_ref.dtype)

    @pl.when(thr_ref[1] > 0.0)
    def _():
        e_lo = jnp.where(lo, 1.0, 0.0).astype(BF16)
        e_hi = jnp.where(lo, 0.0, 1.0).astype(BF16)

        def pv(j, p16):
            v2 = v_ref[pl.ds(pl.multiple_of(j * t, t), t), :]
            vb = jnp.concatenate(
                [jnp.concatenate([v2 * e_lo, e_lo], axis=1),
                 jnp.concatenate([v2 * e_hi, e_hi], axis=1)], axis=0)
            return _dot(p16, vb)

        ck_i = ck_ref[i]
        offs = tuple(thr_ref[2] - jnp.broadcast_to(ck_i[hh:hh + 1, :], (LANES, t)).T[:, 0:1]
                     for hh in range(2))

        def weights(s):
            return jnp.concatenate([jnp.exp2(s[hh] - offs[hh]).astype(BF16) for hh in range(2)], axis=1)

        row = lax.broadcasted_iota(jnp.int32, (t, t), 0)
        col = lax.broadcasted_iota(jnp.int32, (t, t), 1)
        sd = tuple(jnp.where(row >= col, sh, NEG_BIG) for sh in scores(i))

        def body(j, carry):
            jprev, p16, acc = carry
            p_new = weights(scores(j))
            return j, p_new, acc + pv(jprev, p16)

        carry = (i, weights(sd), jnp.zeros((t, 4 * hd), F32))
        jprev, p16, acc = lax.fori_loop(j0, i, body, carry)
        acc = acc + pv(jprev, p16)
        o_ref[...] = (acc[:, :2 * hd] / acc[:, 2 * hd:]).astype(o_ref.dtype)


def _fox_attn(thr, cs, ce, q, k, v, ck, *, t, n_heads):
    b, s, aw = q.shape
    pairs = aw // LANES
    smem = pl.BlockSpec(memory_space=pltpu.SMEM)
    return pl.pallas_call(
        functools.partial(_fox_attn_body, t=t, n_heads=n_heads),
        grid=(b, pairs, s // t),
        in_specs=[
            smem, smem, smem,
            pl.BlockSpec((None, t, LANES), lambda bi, p, i: (bi, i, p)),
            pl.BlockSpec((None, s, LANES), lambda bi, p, i: (bi, 0, p)),
            pl.BlockSpec((None, s, LANES), lambda bi, p, i: (bi, 0, p)),
            pl.BlockSpec((None, None, s // t, 2, t), lambda bi, p, i: (bi, p, 0, 0, 0)),
        ],
        out_specs=pl.BlockSpec((None, t, LANES), lambda bi, p, i: (bi, i, p)),
        out_shape=jax.ShapeDtypeStruct((b, s, aw), BF16),
        compiler_params=_params(("arbitrary", "arbitrary", "arbitrary")),
        name="fox_attn",
    )(thr, cs, ce, q, k, v, ck)


def _proj_res_body(y_ref, w_ref, x_ref, o_ref):
    o_ref[...] = x_ref[...] + _dot(y_ref[...], w_ref[...])


def _proj_res(y, w, x, *, tm=512):
    m, kdim = y.shape
    d = w.shape[1]
    return pl.pallas_call(
        _proj_res_body,
        grid=(m // tm,),
        in_specs=[
            pl.BlockSpec((tm, kdim), lambda i: (i, 0)),
            _const_spec(w.shape),
            pl.BlockSpec((tm, d), lambda i: (i, 0)),
        ],
        out_specs=pl.BlockSpec((tm, d), lambda i: (i, 0)),
        out_shape=jax.ShapeDtypeStruct((m, d), F32),
        compiler_params=_params(("arbitrary",)),
        name="proj_res",
    )(y, w, x)


def _ssd_in_body(x_ref, nw_ref, wz_ref, wxbc_ref, wdt_ref, cw_ref, cb_ref, dtb_ref,
                 z_ref, xs_ref, b_ref, c_ref, dt_ref, ext_ref, *, tm, halo, chunk):
    i = pl.program_id(1)
    di = xs_ref.shape[-1]
    gn = b_ref.shape[-1]
    cdim = wxbc_ref.shape[-1]
    x = x_ref[...]
    h = _rmsnorm(x, nw_ref[...]).astype(BF16)

    @pl.when(i == 0)
    def _():
        ext_ref[:, 0:halo, :] = jnp.zeros((cdim // LANES, halo, LANES), F32)

    @pl.when(i > 0)
    def _():
        ext_ref[:, 0:halo, :] = ext_ref[:, tm:tm + halo, :]

    dt_ref[...] = _softplus(_dot(h, wdt_ref[...]) + dtb_ref[...])
    kw = cw_ref.shape[0]
    outs = ((xs_ref, 0, di), (b_ref, di, di + gn), (c_ref, di + gn, cdim))
    for o_ref, lo_c, hi_c in outs:
        for q0 in range(lo_c, hi_c, chunk):
            raw = _dot(h, wxbc_ref[:, q0:q0 + chunk])
            if q0 < di:
                z_ref[:, q0:q0 + chunk] = _dot(h, wz_ref[:, q0:q0 + chunk]).astype(z_ref.dtype)
            for r0 in range(0, chunk, LANES):
                c0 = q0 + r0
                sl = c0 // LANES
                cs = slice(c0, c0 + LANES)
                cur = raw[:, r0:r0 + LANES]
                ext_ref[sl, halo:halo + tm, :] = cur
                u = cur * cw_ref[kw - 1:kw, cs] + cb_ref[:, cs]
                for k in range(kw - 1):
                    off = halo - (kw - 1) + k
                    u = u + ext_ref[sl, off:off + tm, :] * cw_ref[k:k + 1, cs]
                o_ref[:, c0 - lo_c:c0 - lo_c + LANES] = _silu(u).astype(o_ref.dtype)


def _ssd_in(x, nw, wz, wxbc, wdt, cw, cb, dtb, *, tm=256):
    b, s, d = x.shape
    di = wz.shape[1]
    cdim = wxbc.shape[1]
    gn = (cdim - di) // 2
    halo = 8
    chunk = 512
    row = lambda bi, i: (bi, i, 0)
    return pl.pallas_call(
        functools.partial(_ssd_in_body, tm=tm, halo=halo, chunk=chunk),
        grid=(b, s // tm),
        in_specs=[
            pl.BlockSpec((None, tm, d), row),
            _const_spec((1, d)),
            _const_spec(wz.shape),
            _const_spec(wxbc.shape),
            _const_spec(wdt.shape),
            _const_spec(cw.shape),
            _const_spec(cb.shape),
            _const_spec(dtb.shape),
        ],
        out_specs=[
            pl.BlockSpec((None, tm, di), row),
            pl.BlockSpec((None, tm, di), row),
            pl.BlockSpec((None, tm, gn), row),
            pl.BlockSpec((None, tm, gn), row),
            pl.BlockSpec((None, tm, LANES), row),
        ],
        out_shape=[
            jax.ShapeDtypeStruct((b, s, di), BF16),
            jax.ShapeDtypeStruct((b, s, di), BF16),
            jax.ShapeDtypeStruct((b, s, gn), BF16),
            jax.ShapeDtypeStruct((b, s, gn), BF16),
            jax.ShapeDtypeStruct((b, s, LANES), F32),
        ],
        scratch_shapes=[pltpu.VMEM((cdim // LANES, halo + tm, LANES), F32)],
        compiler_params=_params(("arbitrary", "arbitrary")),
        name="ssd_in",
    )(x, nw, wz, wxbc, wdt, cw, cb, dtb)


def _ssd_scan_body(xs_ref, b_ref, c_ref, dt_ref, alog_ref, dskip_ref, eh_ref, y_ref, st_ref,
                   *, n_heads):
    ci = pl.program_id(1)
    L = xs_ref.shape[0]
    P, N, G = SSM_HEAD_DIM, SSM_STATE, SSM_GROUPS
    hpg = n_heads // G
    gw = hpg * P

    @pl.when(ci == 0)
    def _():
        st_ref[...] = jnp.zeros_like(st_ref)

    lane_h = lax.broadcasted_iota(jnp.int32, (1, LANES), 1)
    a = jnp.where(lane_h < n_heads, -jnp.exp(alog_ref[...]), 0.0)
    dt = dt_ref[...]
    dtT = dt.T
    acumT = _lane_cumsum((dt * a).T)
    acum = acumT.T
    a_last = acum[L - 1:L, :]

    eh = eh_ref[...]
    sdt_b = _dot((jnp.exp(a_last - acum) * dt).astype(BF16), eh)
    ea_b = _dot(jnp.exp(acum).astype(BF16), eh)
    dl = jnp.broadcast_to(jnp.exp(a_last), (16, LANES))
    dl_hi = dl.astype(BF16)
    dl_lo = (dl - dl_hi.astype(F32)).astype(BF16)
    decay_last = (_dot(dl_hi, eh) + _dot(dl_lo, eh))[0:1, :]
    xs16 = xs_ref[...]
    xs = xs16.astype(F32)
    xw = (xs * sdt_b).astype(BF16)

    row = lax.broadcasted_iota(jnp.int32, (L, L), 0)
    col = lax.broadcasted_iota(jnp.int32, (L, L), 1)
    tril = row >= col
    head_of_lane = lax.broadcasted_iota(jnp.int32, (L, gw), 1) // P
    head_mask = [jnp.where(head_of_lane == k, 1.0, 0.0).astype(BF16) for k in range(hpg)]

    for g in range(G):
        bg = b_ref[:, g * N:(g + 1) * N]
        cg = c_ref[:, g * N:(g + 1) * N]
        cb = lax.dot_general(cg, bg, (((1,), (1,)), ((), ())), preferred_element_type=F32)
        gs = slice(g * gw, (g + 1) * gw)
        st = st_ref[g]
        y_off = _dot(cg, st.astype(BF16)) * ea_b[:, gs]
        contrib = lax.dot_general(bg, xw[:, gs], (((0,), (0,)), ((), ())),
                                  preferred_element_type=F32)
        st_ref[g] = st * decay_last[:, gs] + contrib
        ws = []
        for k in range(hpg):
            hidx = g * hpg + k
            seg = acum[:, hidx:hidx + 1] - acumT[hidx:hidx + 1, :]
            decay = jnp.exp(jnp.where(tril, seg, NEG_BIG))
            ws.append((cb * decay * dtT[hidx:hidx + 1, :]).astype(BF16))
        xg = xs16[:, gs]
        xblk = jnp.concatenate([xg * head_mask[k] for k in range(hpg)], axis=0)
        y_diag = _dot(jnp.concatenate(ws, axis=1), xblk)
        y_ref[:, gs] = (y_diag + y_off + dskip_ref[:, gs] * xs[:, gs]).astype(y_ref.dtype)


def _ssd_scan(xs, bm, cm, dt, alog, dskip, eh, *, n_heads):
    b, s, di = xs.shape
    gn = bm.shape[-1]
    L = SSM_CHUNK
    row = lambda bi, i: (bi, i, 0)
    return pl.pallas_call(
        functools.partial(_ssd_scan_body, n_heads=n_heads),
        grid=(b, s // L),
        in_specs=[
            pl.BlockSpec((None, L, di), row),
            pl.BlockSpec((None, L, gn), row),
            pl.BlockSpec((None, L, gn), row),
            pl.BlockSpec((None, L, LANES), row),
            _const_spec(alog.shape),
            _const_spec(dskip.shape),
            _const_spec(eh.shape),
        ],
        out_specs=pl.BlockSpec((None, L, di), row),
        out_shape=jax.ShapeDtypeStruct((b, s, di), BF16),
        scratch_shapes=[pltpu.VMEM((SSM_GROUPS, SSM_STATE, di // SSM_GROUPS), F32)],
        compiler_params=_params(("arbitrary", "arbitrary")),
        name="ssd_scan",
    )(xs, bm, cm, dt, alog, dskip, eh)


def _ssd_out_body(y_ref, z_ref, nw_ref, w_ref, x_ref, o_ref, yn_ref):
    di = y_ref.shape[-1]
    gw = di // SSM_GROUPS
    for g in range(SSM_GROUPS):
        gs = slice(g * gw, (g + 1) * gw)
        yg = y_ref[:, gs].astype(F32) * _silu(z_ref[:, gs].astype(F32))
        yn_ref[:, gs] = _rmsnorm(yg, nw_ref[:, gs]).astype(BF16)
    o_ref[...] = x_ref[...] + _dot(yn_ref[...], w_ref[...])


def _ssd_out(y, z, nw, w, x, *, tm=512):
    m, di = y.shape
    d = w.shape[1]
    return pl.pallas_call(
        _ssd_out_body,
        grid=(m // tm,),
        in_specs=[
            pl.BlockSpec((tm, di), lambda i: (i, 0)),
            pl.BlockSpec((tm, di), lambda i: (i, 0)),
            _const_spec(nw.shape),
            _const_spec(w.shape),
            pl.BlockSpec((tm, d), lambda i: (i, 0)),
        ],
        out_specs=pl.BlockSpec((tm, d), lambda i: (i, 0)),
        out_shape=jax.ShapeDtypeStruct((m, d), F32),
        scratch_shapes=[pltpu.VMEM((tm, di), BF16)],
        compiler_params=_params(("arbitrary",)),
        name="ssd_out",
    )(y, z, nw, w, x)


def _pad_lanes(a, width=LANES):
    return jnp.pad(a, ((0, 0), (0, width - a.shape[-1])))


def _conv_layer(x, nw, w_in, w_dw, w_out):
    return _conv_mixer(x, nw[None], w_in.astype(BF16), w_dw, w_out.astype(BF16))


def _fox_layer(x, nw, w_in, b_f, q_gain, k_gain, w_out, *, t=256):
    b, s, d = x.shape
    n_heads = b_f.shape[0]
    hd = ATTN_HEAD_DIM
    aw = n_heads * hd
    wqkv = w_in[:, :3 * aw].astype(BF16)
    wf = _pad_lanes(w_in[:, 3 * aw:]).astype(BF16)
    bf = _pad_lanes(b_f[None])
    qg = jnp.tile(q_gain, n_heads)[None] * (hd ** -0.5 * LOG2E)
    kg = jnp.tile(k_gain, n_heads)[None]
    smax = 1.02 * LOG2E * hd ** 0.5 * jnp.max(jnp.abs(q_gain)) * jnp.max(jnp.abs(k_gain))
    fast = (2.0 * smax <= FAST_PATH_MAX_LOG2).astype(F32)
    thr = jnp.stack([-(SKIP_LOG2 + 2.0 * smax), fast, smax]).astype(F32)
    head_of = jnp.arange(aw) // hd
    e = (head_of[:, None] == jnp.arange(LANES)[None, :]).astype(F32)
    et = e.T.astype(BF16)
    e = (e / hd).astype(BF16)
    q, k, v, ct = _fox_in(x, nw[None], wqkv, wf, bf, qg, kg, e, et, n_heads=n_heads)
    ck = ct.reshape(b, n_heads // 2, 2, s // t, t).transpose(0, 1, 3, 2, 4)
    cs = ct[:, :, 0::t].reshape(b * n_heads, s // t)
    ce = ct[:, :, t - 1::t].reshape(b * n_heads, s // t)
    attn = _fox_attn(thr, cs, ce, q, k, v, ck, t=t, n_heads=n_heads)
    return _proj_res(attn.reshape(b * s, aw), w_out.astype(BF16), x.reshape(b * s, d)).reshape(b, s, d)


def _ssd_layer(x, nw, w_in, conv_w, conv_b, dt_bias, a_log, d_skip, norm_w, w_out):
    b, s, d = x.shape
    n_heads = a_log.shape[0]
    di = n_heads * SSM_HEAD_DIM
    cdim = conv_w.shape[1]
    wz = w_in[:, :di].astype(BF16)
    wxbc = w_in[:, di:di + cdim].astype(BF16)
    wdt = _pad_lanes(w_in[:, di + cdim:]).astype(BF16)
    z, xs, bm, cm, dt = _ssd_in(x, nw[None], wz, wxbc, wdt, conv_w, conv_b[None],
                                _pad_lanes(dt_bias[None]))
    head_of = jnp.arange(di) // SSM_HEAD_DIM
    eh = (jnp.arange(LANES)[:, None] == head_of[None, :]).astype(BF16)
    dskip = jnp.repeat(d_skip, SSM_HEAD_DIM)[None]
    y = _ssd_scan(xs, bm, cm, dt, _pad_lanes(a_log[None]), dskip, eh, n_heads=n_heads)
    m = b * s
    return _ssd_out(y.reshape(m, di), z.reshape(m, di), norm_w[None], w_out.astype(BF16),
                    x.reshape(m, d)).reshape(b, s, d)


def kernel(x, mix_norm, ffn_norm, ffn_w_gu, ffn_w_down, conv_w_in, conv_w_dw, conv_w_out, fox_w_in, fox_b_f, fox_q_gain, fox_k_gain, fox_w_out, ssd_w_in, ssd_conv_w, ssd_conv_b, ssd_dt_bias, ssd_a_log, ssd_d, ssd_norm_w, ssd_w_out):
    b, s, d = x.shape
    depth = mix_norm.shape[0]
    dff = ffn_w_down.shape[1]
    for i in range(depth):
        kind, j = i % 3, i // 3
        if kind == 0:
            x = _conv_layer(x, mix_norm[i], conv_w_in[j], conv_w_dw[j], conv_w_out[j])
        elif kind == 1:
            x = _fox_layer(x, mix_norm[i], fox_w_in[j], fox_b_f[j], fox_q_gain[j], fox_k_gain[j],
                           fox_w_out[j])
        else:
            x = _ssd_layer(x, mix_norm[i], ssd_w_in[j], ssd_conv_w[j], ssd_conv_b[j],
                           ssd_dt_bias[j], ssd_a_log[j], ssd_d[j], ssd_norm_w[j], ssd_w_out[j])
        wg = ffn_w_gu[i][:, :dff].astype(BF16)
        wu = ffn_w_gu[i][:, dff:].astype(BF16)
        x = _ffn(x.reshape(b * s, d), ffn_norm[i][None], wg, wu,
                 ffn_w_down[i].astype(BF16)).reshape(b, s, d)
    return x
```

```python
import functools

import jax
import jax.numpy as jnp
from jax import lax
from jax.experimental import pallas as pl
from jax.experimental.pallas import tpu as pltpu

F32 = jnp.float32
BF16 = jnp.bfloat16

RMS_EPS = 1e-6
ATTN_HEAD_DIM = 64
SSM_HEAD_DIM = 64
SSM_GROUPS = 8
SSM_STATE = 128
SSM_CHUNK = 128
LANES = 128
NEG_BIG = -1e30
LOG2E = 1.4426950408889634
SKIP_LOG2 = 64.0
FAST_PATH_MAX_LOG2 = 100.0

VMEM_LIMIT = 56 * 1024 * 1024


def _params(sem):
    return pltpu.CompilerParams(dimension_semantics=sem, vmem_limit_bytes=VMEM_LIMIT)


def _const_spec(shape):
    nd = len(shape)
    return pl.BlockSpec(shape, lambda *_: (0,) * nd)


def _rmsnorm(x, w):
    return x * lax.rsqrt(jnp.mean(x * x, axis=-1, keepdims=True) + RMS_EPS) * w


def _dot(a, b):
    return jnp.dot(a, b, preferred_element_type=F32)


def _softplus(x):
    return jnp.maximum(x, 0.0) + jnp.log1p(jnp.exp(-jnp.abs(x)))


def _silu(x):
    return x * jax.nn.sigmoid(x)


def _lane_cumsum(x):
    n = x.shape[-1]
    lane = lax.broadcasted_iota(jnp.int32, x.shape, x.ndim - 1)
    s = 1
    while s < n:
        x = x + jnp.where(lane >= s, pltpu.roll(x, s, x.ndim - 1), 0.0)
        s *= 2
    return x


def _ffn_body(x_ref, nw_ref, wg_ref, wu_ref, wd_ref, o_ref, a_ref, *, chunk):
    x = x_ref[...]
    h = _rmsnorm(x, nw_ref[...]).astype(BF16)
    for c in range(wg_ref.shape[1] // chunk):
        sl = slice(c * chunk, (c + 1) * chunk)
        g = _dot(h, wg_ref[:, sl])
        u = _dot(h, wu_ref[:, sl])
        a_ref[:, sl] = (_silu(g) * u).astype(BF16)
    o_ref[...] = x + _dot(a_ref[...], wd_ref[...])


def _ffn(x, nw, wg, wu, wd, *, tm=512, chunk=256):
    m, d = x.shape
    dff = wg.shape[1]
    return pl.pallas_call(
        functools.partial(_ffn_body, chunk=chunk),
        grid=(m // tm,),
        in_specs=[
            pl.BlockSpec((tm, d), lambda i: (i, 0)),
            _const_spec((1, d)),
            _const_spec((d, dff)),
            _const_spec((d, dff)),
            _const_spec((dff, d)),
        ],
        out_specs=pl.BlockSpec((tm, d), lambda i: (i, 0)),
        out_shape=jax.ShapeDtypeStruct((m, d), F32),
        scratch_shapes=[pltpu.VMEM((tm, dff), BF16)],
        compiler_params=_params(("arbitrary",)),
        name="ffn",
    )(x, nw, wg, wu, wd)


def _conv_mixer_body(x_ref, nw_ref, win_ref, wdw_ref, wout_ref, o_ref, ext_ref, y_ref,
                     *, tm, halo, chunk):
    i = pl.program_id(1)
    d = x_ref.shape[-1]
    x = x_ref[...]
    h = _rmsnorm(x, nw_ref[...]).astype(BF16)

    @pl.when(i == 0)
    def _():
        ext_ref[:, 0:halo, :] = jnp.zeros((d // LANES, halo, LANES), F32)

    @pl.when(i > 0)
    def _():
        ext_ref[:, 0:halo, :] = ext_ref[:, tm:tm + halo, :]

    kw = wdw_ref.shape[0]
    for q0 in range(0, d, chunk):
        bq = _dot(h, win_ref[:, q0:q0 + chunk])
        cq = _dot(h, win_ref[:, d + q0:d + q0 + chunk])
        vq = _dot(h, win_ref[:, 2 * d + q0:2 * d + q0 + chunk])
        for r0 in range(0, chunk, LANES):
            c0 = q0 + r0
            sl = c0 // LANES
            cs = slice(c0, c0 + LANES)
            rs = slice(r0, r0 + LANES)
            cv = cq[:, rs] * vq[:, rs]
            ext_ref[sl, halo:halo + tm, :] = cv
            u = cv * wdw_ref[kw - 1:kw, cs]
            for k in range(kw - 1):
                off = halo - (kw - 1) + k
                u = u + ext_ref[sl, off:off + tm, :] * wdw_ref[k:k + 1, cs]
            y_ref[:, cs] = (bq[:, rs] * u).astype(BF16)
    o_ref[...] = x + _dot(y_ref[...], wout_ref[...])


def _conv_mixer(x, nw, win, wdw, wout, *, tm=512):
    b, s, d = x.shape
    halo = 8
    return pl.pallas_call(
        functools.partial(_conv_mixer_body, tm=tm, halo=halo, chunk=256),
        grid=(b, s // tm),
        in_specs=[
            pl.BlockSpec((None, tm, d), lambda bi, i: (bi, i, 0)),
            _const_spec((1, d)),
            _const_spec(win.shape),
            _const_spec(wdw.shape),
            _const_spec(wout.shape),
        ],
        out_specs=pl.BlockSpec((None, tm, d), lambda bi, i: (bi, i, 0)),
        out_shape=jax.ShapeDtypeStruct((b, s, d), F32),
        scratch_shapes=[pltpu.VMEM((d // LANES, halo + tm, LANES), F32),
                        pltpu.VMEM((tm, d), BF16)],
        compiler_params=_params(("arbitrary", "arbitrary")),
        name="conv_mixer",
    )(x, nw, win, wdw, wout)


def _fox_in_body(x_ref, nw_ref, wqkv_ref, wf_ref, bf_ref, qg_ref, kg_ref, e_ref, et_ref,
                 q_ref, k_ref, v_ref, ct_ref, carry_ref, *, n_heads):
    i = pl.program_id(1)
    aw = q_ref.shape[-1]
    x = x_ref[...]
    h = _rmsnorm(x, nw_ref[...]).astype(BF16)
    qkv = _dot(h, wqkv_ref[...])

    def head_norm(t, gain):
        ms = _dot((t * t).astype(BF16), e_ref[...])
        r = lax.rsqrt(ms + RMS_EPS)
        r_hi = r.astype(BF16)
        r_lo = (r - r_hi.astype(F32)).astype(BF16)
        rb = _dot(r_hi, et_ref[...]) + _dot(r_lo, et_ref[...])
        return t * rb * gain

    q_ref[...] = head_norm(qkv[:, :aw], qg_ref[...]).astype(BF16)
    k_ref[...] = head_norm(qkv[:, aw:2 * aw], kg_ref[...]).astype(BF16)
    v_ref[...] = qkv[:, 2 * aw:].astype(BF16)

    fl = _dot(h, wf_ref[...]) + bf_ref[...]
    logf = -_softplus(-fl) * LOG2E
    local = _lane_cumsum(logf.T[0:n_heads, :])

    @pl.when(i == 0)
    def _():
        carry_ref[...] = jnp.zeros_like(carry_ref)

    cum = local + carry_ref[:, 0:1]
    ct_ref[...] = cum
    tm = cum.shape[-1]
    carry_ref[...] = jnp.broadcast_to(cum[:, tm - 1:tm], carry_ref.shape)


def _fox_in(x, nw, wqkv, wf, bf, qg, kg, e, et, *, n_heads, tm=512):
    b, s, d = x.shape
    aw = wqkv.shape[1] // 3
    row = lambda bi, i: (bi, i, 0)
    return pl.pallas_call(
        functools.partial(_fox_in_body, n_heads=n_heads),
        grid=(b, s // tm),
        in_specs=[
            pl.BlockSpec((None, tm, d), row),
            _const_spec((1, d)),
            _const_spec(wqkv.shape),
            _const_spec(wf.shape),
            _const_spec(bf.shape),
            _const_spec(qg.shape),
            _const_spec(kg.shape),
            _const_spec(e.shape),
            _const_spec(et.shape),
        ],
        out_specs=[
            pl.BlockSpec((None, tm, aw), row),
            pl.BlockSpec((None, tm, aw), row),
            pl.BlockSpec((None, tm, aw), row),
            pl.BlockSpec((None, n_heads, tm), lambda bi, i: (bi, 0, i)),
        ],
        out_shape=[
            jax.ShapeDtypeStruct((b, s, aw), BF16),
            jax.ShapeDtypeStruct((b, s, aw), BF16),
            jax.ShapeDtypeStruct((b, s, aw), BF16),
            jax.ShapeDtypeStruct((b, n_heads, s), F32),
        ],
        scratch_shapes=[pltpu.VMEM((n_heads, LANES), F32)],
        compiler_params=_params(("arbitrary", "arbitrary")),
        name="fox_in",
    )(x, nw, wqkv, wf, bf, qg, kg, e, et)


def _fox_attn_body(thr_ref, cs_ref, ce_ref, q_ref, k_ref, v_ref, ck_ref, o_ref, p_ref, acc_ref,
                   *, t, n_heads):
    bi, pr, i = pl.program_id(0), pl.program_id(1), pl.program_id(2)
    hd = ATTN_HEAD_DIM
    q2 = q_ref[...]
    lane = lax.broadcasted_iota(jnp.int32, (t, 2 * hd), 1)
    lo = lane < hd
    zero = jnp.zeros_like(q2)
    qs = (jnp.where(lo, q2, zero), jnp.where(lo, zero, q2))

    r0 = bi * n_heads + 2 * pr
    thr = thr_ref[0]
    cs0, cs1 = cs_ref[r0, i], cs_ref[r0 + 1, i]

    def needed(j):
        jj = jnp.maximum(j, 0)
        near = (cs0 - ce_ref[r0, jj] >= thr) | (cs1 - ce_ref[r0 + 1, jj] >= thr)
        return (j >= 0) & near

    j0 = lax.while_loop(needed, lambda j: j - 1, i - 1) + 1

    def scores(j):
        k2 = k_ref[pl.ds(pl.multiple_of(j * t, t), t), :]
        ck = ck_ref[j]
        return tuple(lax.dot_general(qs[hh], k2, (((1,), (1,)), ((), ())),
                                     preferred_element_type=F32) - ck[hh:hh + 1, :]
                     for hh in range(2))

    def update(j, s, ms, ls, acc, masked):
        v2 = v_ref[pl.ds(pl.multiple_of(j * t, t), t), :]
        if masked:
            row = lax.broadcasted_iota(jnp.int32, (t, t), 0)
            col = lax.broadcasted_iota(jnp.int32, (t, t), 1)
            s = tuple(jnp.where(row >= col, sh, NEG_BIG) for sh in s)
        new_m, new_l, alphas, pvs = [], [], [], []
        for hh in range(2):
            m_new = jnp.maximum(ms[hh], jnp.max(s[hh], axis=1, keepdims=True))
            alpha = jnp.exp2(ms[hh] - m_new)
            p = jnp.exp2(s[hh] - m_new)
            new_l.append(alpha * ls[hh] + jnp.sum(p, axis=1, keepdims=True))
            new_m.append(m_new)
            alphas.append(alpha)
            pvs.append(_dot(p.astype(BF16), v2))
        acc = acc * jnp.where(lo, alphas[0], alphas[1]) + jnp.where(lo, pvs[0], pvs[1])
        return tuple(new_m), tuple(new_l), acc

    @pl.when(thr_ref[1] <= 0.0)
    def _():
        def body(j, carry):
            s, ms, ls, acc = carry
            s_next = scores(j + 1)
            ms, ls, acc = update(j, s, ms, ls, acc, False)
            return s_next, ms, ls, acc

        m0 = jnp.full((t, 1), NEG_BIG, F32)
        l0 = jnp.zeros((t, 1), F32)
        carry = (scores(j0), (m0, m0), (l0, l0), jnp.zeros((t, 2 * hd), F32))
        s, ms, ls, acc = lax.fori_loop(j0, i, body, carry)
        _, ls, acc = update(i, s, ms, ls, acc, True)
        o_ref[...] = (acc / jnp.where(lo, ls[0], ls[1])).astype(o_ref.dtype)

    @pl.when(thr_ref[1] > 0.0)
    def _():
        e_lo = jnp.where(lo, 1.0, 0.0).astype(BF16)
        e_hi = jnp.where(lo, 0.0, 1.0).astype(BF16)

        def pv(j, p16):
            v2 = v_ref[pl.ds(pl.multiple_of(j * t, t), t), :]
            vb = jnp.concatenate(
                [jnp.concatenate([v2 * e_lo, e_lo], axis=1),
                 jnp.concatenate([v2 * e_hi, e_hi], axis=1)], axis=0)
            return _dot(p16, vb)

        ck_i = ck_ref[i]
        offs = tuple(thr_ref[2] - jnp.broadcast_to(ck_i[hh:hh + 1, :], (LANES, t)).T[:, 0:1]
                     for hh in range(2))

        def weights(s):
            return jnp.concatenate([jnp.exp2(s[hh] - offs[hh]).astype(BF16) for hh in range(2)], axis=1)

        row = lax.broadcasted_iota(jnp.int32, (t, t), 0)
        col = lax.broadcasted_iota(jnp.int32, (t, t), 1)
        sd = tuple(jnp.where(row >= col, sh, NEG_BIG) for sh in scores(i))

        p_ref[0] = weights(sd)
        acc_ref[...] = jnp.zeros((t, 4 * hd), F32)

        def body(j, jprev):
            slot = (j - j0) & 1
            p_new = weights(scores(j))
            acc_ref[...] += pv(jprev, p_ref[slot])
            p_ref[1 - slot] = p_new
            return j

        jprev = lax.fori_loop(j0, i, body, i)
        acc = acc_ref[...] + pv(jprev, p_ref[(i - j0) & 1])
        o_ref[...] = (acc[:, :2 * hd] / acc[:, 2 * hd:]).astype(o_ref.dtype)


def _fox_attn(thr, cs, ce, q, k, v, ck, *, t, n_heads):
    b, s, aw = q.shape
    pairs = aw // LANES
    smem = pl.BlockSpec(memory_space=pltpu.SMEM)
    return pl.pallas_call(
        functools.partial(_fox_attn_body, t=t, n_heads=n_heads),
        grid=(b, pairs, s // t),
        in_specs=[
            smem, smem, smem,
            pl.BlockSpec((None, t, LANES), lambda bi, p, i: (bi, i, p)),
            pl.BlockSpec((None, s, LANES), lambda bi, p, i: (bi, 0, p)),
            pl.BlockSpec((None, s, LANES), lambda bi, p, i: (bi, 0, p)),
            pl.BlockSpec((None, None, s // t, 2, t), lambda bi, p, i: (bi, p, 0, 0, 0)),
        ],
        out_specs=pl.BlockSpec((None, t, LANES), lambda bi, p, i: (bi, i, p)),
        out_shape=jax.ShapeDtypeStruct((b, s, aw), BF16),
        scratch_shapes=[pltpu.VMEM((2, t, 2 * t), BF16), pltpu.VMEM((t, 2 * LANES), F32)],
        compiler_params=_params(("arbitrary", "arbitrary", "arbitrary")),
        name="fox_attn",
    )(thr, cs, ce, q, k, v, ck)


def _proj_res_body(y_ref, w_ref, x_ref, o_ref):
    o_ref[...] = x_ref[...] + _dot(y_ref[...], w_ref[...])


def _proj_res(y, w, x, *, tm=512):
    m, kdim = y.shape
    d = w.shape[1]
    return pl.pallas_call(
        _proj_res_body,
        grid=(m // tm,),
        in_specs=[
            pl.BlockSpec((tm, kdim), lambda i: (i, 0)),
            _const_spec(w.shape),
            pl.BlockSpec((tm, d), lambda i: (i, 0)),
        ],
        out_specs=pl.BlockSpec((tm, d), lambda i: (i, 0)),
        out_shape=jax.ShapeDtypeStruct((m, d), F32),
        compiler_params=_params(("arbitrary",)),
        name="proj_res",
    )(y, w, x)


def _ssd_in_body(x_ref, nw_ref, wz_ref, wxbc_ref, wdt_ref, cw_ref, cb_ref, dtb_ref,
                 z_ref, xs_ref, b_ref, c_ref, dt_ref, ext_ref, *, tm, halo, chunk):
    i = pl.program_id(1)
    di = xs_ref.shape[-1]
    gn = b_ref.shape[-1]
    cdim = wxbc_ref.shape[-1]
    x = x_ref[...]
    h = _rmsnorm(x, nw_ref[...]).astype(BF16)

    @pl.when(i == 0)
    def _():
        ext_ref[:, 0:halo, :] = jnp.zeros((cdim // LANES, halo, LANES), F32)

    @pl.when(i > 0)
    def _():
        ext_ref[:, 0:halo, :] = ext_ref[:, tm:tm + halo, :]

    dt_ref[...] = _softplus(_dot(h, wdt_ref[...]) + dtb_ref[...])
    kw = cw_ref.shape[0]
    outs = ((xs_ref, 0, di), (b_ref, di, di + gn), (c_ref, di + gn, cdim))
    for o_ref, lo_c, hi_c in outs:
        for q0 in range(lo_c, hi_c, chunk):
            raw = _dot(h, wxbc_ref[:, q0:q0 + chunk])
            if q0 < di:
                z_ref[:, q0:q0 + chunk] = _dot(h, wz_ref[:, q0:q0 + chunk]).astype(z_ref.dtype)
            for r0 in range(0, chunk, LANES):
                c0 = q0 + r0
                sl = c0 // LANES
                cs = slice(c0, c0 + LANES)
                cur = raw[:, r0:r0 + LANES]
                ext_ref[sl, halo:halo + tm, :] = cur
                u = cur * cw_ref[kw - 1:kw, cs] + cb_ref[:, cs]
                for k in range(kw - 1):
                    off = halo - (kw - 1) + k
                    u = u + ext_ref[sl, off:off + tm, :] * cw_ref[k:k + 1, cs]
                o_ref[:, c0 - lo_c:c0 - lo_c + LANES] = _silu(u).astype(o_ref.dtype)


def _ssd_in(x, nw, wz, wxbc, wdt, cw, cb, dtb, *, tm=256):
    b, s, d = x.shape
    di = wz.shape[1]
    cdim = wxbc.shape[1]
    gn = (cdim - di) // 2
    halo = 8
    chunk = 512
    row = lambda bi, i: (bi, i, 0)
    return pl.pallas_call(
        functools.partial(_ssd_in_body, tm=tm, halo=halo, chunk=chunk),
        grid=(b, s // tm),
        in_specs=[
            pl.BlockSpec((None, tm, d), row),
            _const_spec((1, d)),
            _const_spec(wz.shape),
            _const_spec(wxbc.shape),
            _const_spec(wdt.shape),
            _const_spec(cw.shape),
            _const_spec(cb.shape),
            _const_spec(dtb.shape),
        ],
        out_specs=[
            pl.BlockSpec((None, tm, di), row),
            pl.BlockSpec((None, tm, di), row),
            pl.BlockSpec((None, tm, gn), row),
            pl.BlockSpec((None, tm, gn), row),
            pl.BlockSpec((None, tm, LANES), row),
        ],
        out_shape=[
            jax.ShapeDtypeStruct((b, s, di), BF16),
            jax.ShapeDtypeStruct((b, s, di), BF16),
            jax.ShapeDtypeStruct((b, s, gn), BF16),
            jax.ShapeDtypeStruct((b, s, gn), BF16),
            jax.ShapeDtypeStruct((b, s, LANES), F32),
        ],
        scratch_shapes=[pltpu.VMEM((cdim // LANES, halo + tm, LANES), F32)],
        compiler_params=_params(("arbitrary", "arbitrary")),
        name="ssd_in",
    )(x, nw, wz, wxbc, wdt, cw, cb, dtb)


def _ssd_scan_body(xs_ref, b_ref, c_ref, dt_ref, alog_ref, dskip_ref, eh_ref, y_ref, st_ref,
                   *, n_heads):
    ci = pl.program_id(1)
    L = xs_ref.shape[0]
    P, N, G = SSM_HEAD_DIM, SSM_STATE, SSM_GROUPS
    hpg = n_heads // G
    gw = hpg * P

    @pl.when(ci == 0)
    def _():
        st_ref[...] = jnp.zeros_like(st_ref)

    lane_h = lax.broadcasted_iota(jnp.int32, (1, LANES), 1)
    a = jnp.where(lane_h < n_heads, -jnp.exp(alog_ref[...]), 0.0)
    dt = dt_ref[...]
    dtT = dt.T
    row = lax.broadcasted_iota(jnp.int32, (L, L), 0)
    col = lax.broadcasted_iota(jnp.int32, (L, L), 1)
    tril = row >= col
    da = dt * a
    da_hi = da.astype(BF16)
    rest = da - da_hi.astype(F32)
    da_mid = rest.astype(BF16)
    da_lo = (rest - da_mid.astype(F32)).astype(BF16)
    parts = _dot(jnp.where(tril, 1.0, 0.0).astype(BF16),
                 jnp.concatenate([da_hi, da_mid, da_lo], axis=1))
    acum = parts[:, :LANES] + parts[:, LANES:2 * LANES] + parts[:, 2 * LANES:]
    acumT = acum.T
    a_last = acum[L - 1:L, :]

    eh = eh_ref[...]
    sdt_b = _dot((jnp.exp(a_last - acum) * dt).astype(BF16), eh)
    ea_b = _dot(jnp.exp(acum).astype(BF16), eh)
    dl = jnp.broadcast_to(jnp.exp(a_last), (16, LANES))
    dl_hi = dl.astype(BF16)
    dl_lo = (dl - dl_hi.astype(F32)).astype(BF16)
    decay_last = (_dot(dl_hi, eh) + _dot(dl_lo, eh))[0:1, :]
    xs16 = xs_ref[...]
    xs = xs16.astype(F32)
    xw = (xs * sdt_b).astype(BF16)

    head_of_lane = lax.broadcasted_iota(jnp.int32, (L, gw), 1) // P
    head_mask = [jnp.where(head_of_lane == k, 1.0, 0.0).astype(BF16) for k in range(hpg)]

    cbs, y_offs = [], []
    for g in range(G):
        bg = b_ref[:, g * N:(g + 1) * N]
        cg = c_ref[:, g * N:(g + 1) * N]
        cbs.append(lax.dot_general(cg, bg, (((1,), (1,)), ((), ())), preferred_element_type=F32))
        gs = slice(g * gw, (g + 1) * gw)
        st = st_ref[g]
        y_offs.append(_dot(cg, st.astype(BF16)) * ea_b[:, gs])
        contrib = lax.dot_general(bg, xw[:, gs], (((0,), (0,)), ((), ())),
                                  preferred_element_type=F32)
        st_ref[g] = st * decay_last[:, gs] + contrib

    for g in range(G):
        gs = slice(g * gw, (g + 1) * gw)
        cb, y_off = cbs[g], y_offs[g]
        ws = []
        for k in range(hpg):
            hidx = g * hpg + k
            seg = acum[:, hidx:hidx + 1] - acumT[hidx:hidx + 1, :]
            decay = jnp.exp(jnp.where(tril, seg, NEG_BIG))
            ws.append((cb * decay * dtT[hidx:hidx + 1, :]).astype(BF16))
        xg = xs16[:, gs]
        xblk = jnp.concatenate([xg * head_mask[k] for k in range(hpg)], axis=0)
        y_diag = _dot(jnp.concatenate(ws, axis=1), xblk)
        y_ref[:, gs] = (y_diag + y_off + dskip_ref[:, gs] * xs[:, gs]).astype(y_ref.dtype)


def _ssd_scan(xs, bm, cm, dt, alog, dskip, eh, *, n_heads):
    b, s, di = xs.shape
    gn = bm.shape[-1]
    L = SSM_CHUNK
    row = lambda bi, i: (bi, i, 0)
    return pl.pallas_call(
        functools.partial(_ssd_scan_body, n_heads=n_heads),
        grid=(b, s // L),
        in_specs=[
            pl.BlockSpec((None, L, di), row),
            pl.BlockSpec((None, L, gn), row),
            pl.BlockSpec((None, L, gn), row),
            pl.BlockSpec((None, L, LANES), row),
            _const_spec(alog.shape),
            _const_spec(dskip.shape),
            _const_spec(eh.shape),
        ],
        out_specs=pl.BlockSpec((None, L, di), row),
        out_shape=jax.ShapeDtypeStruct((b, s, di), BF16),
        scratch_shapes=[pltpu.VMEM((SSM_GROUPS, SSM_STATE, di // SSM_GROUPS), F32)],
        compiler_params=_params(("arbitrary", "arbitrary")),
        name="ssd_scan",
    )(xs, bm, cm, dt, alog, dskip, eh)


def _ssd_out_body(y_ref, z_ref, nw_ref, w_ref, x_ref, o_ref, yn_ref):
    di = y_ref.shape[-1]
    gw = di // SSM_GROUPS
    for g in range(SSM_GROUPS):
        gs = slice(g * gw, (g + 1) * gw)
        yg = y_ref[:, gs].astype(F32) * _silu(z_ref[:, gs].astype(F32))
        yn_ref[:, gs] = _rmsnorm(yg, nw_ref[:, gs]).astype(BF16)
    o_ref[...] = x_ref[...] + _dot(yn_ref[...], w_ref[...])


def _ssd_out(y, z, nw, w, x, *, tm=512):
    m, di = y.shape
    d = w.shape[1]
    return pl.pallas_call(
        _ssd_out_body,
        grid=(m // tm,),
        in_specs=[
            pl.BlockSpec((tm, di), lambda i: (i, 0)),
            pl.BlockSpec((tm, di), lambda i: (i, 0)),
            _const_spec(nw.shape),
            _const_spec(w.shape),
            pl.BlockSpec((tm, d), lambda i: (i, 0)),
        ],
        out_specs=pl.BlockSpec((tm, d), lambda i: (i, 0)),
        out_shape=jax.ShapeDtypeStruct((m, d), F32),
        scratch_shapes=[pltpu.VMEM((tm, di), BF16)],
        compiler_params=_params(("arbitrary",)),
        name="ssd_out",
    )(y, z, nw, w, x)


def _pad_lanes(a, width=LANES):
    return jnp.pad(a, ((0, 0), (0, width - a.shape[-1])))


def _conv_layer(x, nw, w_in, w_dw, w_out):
    return _conv_mixer(x, nw[None], w_in.astype(BF16), w_dw, w_out.astype(BF16))


def _fox_layer(x, nw, w_in, b_f, q_gain, k_gain, w_out, *, t=256):
    b, s, d = x.shape
    n_heads = b_f.shape[0]
    hd = ATTN_HEAD_DIM
    aw = n_heads * hd
    wqkv = w_in[:, :3 * aw].astype(BF16)
    wf = _pad_lanes(w_in[:, 3 * aw:]).astype(BF16)
    bf = _pad_lanes(b_f[None])
    qg = jnp.tile(q_gain, n_heads)[None] * (hd ** -0.5 * LOG2E)
    kg = jnp.tile(k_gain, n_heads)[None]
    smax = 1.02 * LOG2E * hd ** 0.5 * jnp.max(jnp.abs(q_gain)) * jnp.max(jnp.abs(k_gain))
    fast = (2.0 * smax <= FAST_PATH_MAX_LOG2).astype(F32)
    thr = jnp.stack([-(SKIP_LOG2 + 2.0 * smax), fast, smax]).astype(F32)
    head_of = jnp.arange(aw) // hd
    e = (head_of[:, None] == jnp.arange(LANES)[None, :]).astype(F32)
    et = e.T.astype(BF16)
    e = (e / hd).astype(BF16)
    q, k, v, ct = _fox_in(x, nw[None], wqkv, wf, bf, qg, kg, e, et, n_heads=n_heads)
    ck = ct.reshape(b, n_heads // 2, 2, s // t, t).transpose(0, 1, 3, 2, 4)
    cs = ct[:, :, 0::t].reshape(b * n_heads, s // t)
    ce = ct[:, :, t - 1::t].reshape(b * n_heads, s // t)
    attn = _fox_attn(thr, cs, ce, q, k, v, ck, t=t, n_heads=n_heads)
    return _proj_res(attn.reshape(b * s, aw), w_out.astype(BF16), x.reshape(b * s, d)).reshape(b, s, d)


def _ssd_layer(x, nw, w_in, conv_w, conv_b, dt_bias, a_log, d_skip, norm_w, w_out):
    b, s, d = x.shape
    n_heads = a_log.shape[0]
    di = n_heads * SSM_HEAD_DIM
    cdim = conv_w.shape[1]
    wz = w_in[:, :di].astype(BF16)
    wxbc = w_in[:, di:di + cdim].astype(BF16)
    wdt = _pad_lanes(w_in[:, di + cdim:]).astype(BF16)
    z, xs, bm, cm, dt = _ssd_in(x, nw[None], wz, wxbc, wdt, conv_w, conv_b[None],
                                _pad_lanes(dt_bias[None]))
    head_of = jnp.arange(di) // SSM_HEAD_DIM
    eh = (jnp.arange(LANES)[:, None] == head_of[None, :]).astype(BF16)
    dskip = jnp.repeat(d_skip, SSM_HEAD_DIM)[None]
    y = _ssd_scan(xs, bm, cm, dt, _pad_lanes(a_log[None]), dskip, eh, n_heads=n_heads)
    m = b * s
    return _ssd_out(y.reshape(m, di), z.reshape(m, di), norm_w[None], w_out.astype(BF16),
                    x.reshape(m, d)).reshape(b, s, d)


def kernel(x, mix_norm, ffn_norm, ffn_w_gu, ffn_w_down, conv_w_in, conv_w_dw, conv_w_out, fox_w_in, fox_b_f, fox_q_gain, fox_k_gain, fox_w_out, ssd_w_in, ssd_conv_w, ssd_conv_b, ssd_dt_bias, ssd_a_log, ssd_d, ssd_norm_w, ssd_w_out):
    b, s, d = x.shape
    depth = mix_norm.shape[0]
    dff = ffn_w_down.shape[1]
    for i in range(depth):
        kind, j = i % 3, i // 3
        if kind == 0:
            x = _conv_layer(x, mix_norm[i], conv_w_in[j], conv_w_dw[j], conv_w_out[j])
        elif kind == 1:
            x = _fox_layer(x, mix_norm[i], fox_w_in[j], fox_b_f[j], fox_q_gain[j], fox_k_gain[j],
                           fox_w_out[j])
        else:
            x = _ssd_layer(x, mix_norm[i], ssd_w_in[j], ssd_conv_w[j], ssd_conv_b[j],
                           ssd_dt_bias[j], ssd_a_log[j], ssd_d[j], ssd_norm_w[j], ssd_w_out[j])
        wg = ffn_w_gu[i][:, :dff].astype(BF16)
        wu = ffn_w_gu[i][:, dff:].astype(BF16)
        x = _ffn(x.reshape(b * s, d), ffn_norm[i][None], wg, wu,
                 ffn_w_down[i].astype(BF16)).reshape(b, s, d)
    return x
```

```python
import functools

import jax
import jax.numpy as jnp
from jax import lax
from jax.experimental import pallas as pl
from jax.experimental.pallas import tpu as pltpu

F32 = jnp.float32
BF16 = jnp.bfloat16

RMS_EPS = 1e-6
ATTN_HEAD_DIM = 64
SSM_HEAD_DIM = 64
SSM_GROUPS = 8
SSM_STATE = 128
SSM_CHUNK = 128
LANES = 128
NEG_BIG = -1e30
LOG2E = 1.4426950408889634
SKIP_LOG2 = 64.0
FAST_PATH_MAX_LOG2 = 100.0

VMEM_LIMIT = 56 * 1024 * 1024


def _params(sem):
    return pltpu.CompilerParams(dimension_semantics=sem, vmem_limit_bytes=VMEM_LIMIT)


def _const_spec(shape):
    nd = len(shape)
    return pl.BlockSpec(shape, lambda *_: (0,) * nd)


def _rmsnorm(x, w):
    return x * lax.rsqrt(jnp.mean(x * x, axis=-1, keepdims=True) + RMS_EPS) * w


def _dot(a, b):
    return jnp.dot(a, b, preferred_element_type=F32)


def _softplus(x):
    return jnp.maximum(x, 0.0) + jnp.log1p(jnp.exp(-jnp.abs(x)))


def _silu(x):
    return x * jax.nn.sigmoid(x)


def _lane_cumsum(x):
    n = x.shape[-1]
    lane = lax.broadcasted_iota(jnp.int32, x.shape, x.ndim - 1)
    s = 1
    while s < n:
        x = x + jnp.where(lane >= s, pltpu.roll(x, s, x.ndim - 1), 0.0)
        s *= 2
    return x


def _ffn_body(x_ref, nw_ref, wg_ref, wu_ref, wd_ref, o_ref, a_ref, *, chunk):
    x = x_ref[...]
    h = _rmsnorm(x, nw_ref[...]).astype(BF16)
    for c in range(wg_ref.shape[1] // chunk):
        sl = slice(c * chunk, (c + 1) * chunk)
        g = _dot(h, wg_ref[:, sl])
        u = _dot(h, wu_ref[:, sl])
        a_ref[:, sl] = (_silu(g) * u).astype(BF16)
    o_ref[...] = x + _dot(a_ref[...], wd_ref[...])


def _ffn(x, nw, wg, wu, wd, *, tm=512, chunk=256):
    m, d = x.shape
    dff = wg.shape[1]
    return pl.pallas_call(
        functools.partial(_ffn_body, chunk=chunk),
        grid=(m // tm,),
        in_specs=[
            pl.BlockSpec((tm, d), lambda i: (i, 0)),
            _const_spec((1, d)),
            _const_spec((d, dff)),
            _const_spec((d, dff)),
            _const_spec((dff, d)),
        ],
        out_specs=pl.BlockSpec((tm, d), lambda i: (i, 0)),
        out_shape=jax.ShapeDtypeStruct((m, d), F32),
        scratch_shapes=[pltpu.VMEM((tm, dff), BF16)],
        compiler_params=_params(("arbitrary",)),
        name="ffn",
    )(x, nw, wg, wu, wd)


def _conv_mixer_body(x_ref, nw_ref, win_ref, wdw_ref, wout_ref, o_ref, ext_ref, y_ref,
                     *, tm, halo, chunk):
    i = pl.program_id(1)
    d = x_ref.shape[-1]
    x = x_ref[...]
    h = _rmsnorm(x, nw_ref[...]).astype(BF16)

    @pl.when(i == 0)
    def _():
        ext_ref[:, 0:halo, :] = jnp.zeros((d // LANES, halo, LANES), F32)

    @pl.when(i > 0)
    def _():
        ext_ref[:, 0:halo, :] = ext_ref[:, tm:tm + halo, :]

    kw = wdw_ref.shape[0]
    for q0 in range(0, d, chunk):
        bq = _dot(h, win_ref[:, q0:q0 + chunk])
        cq = _dot(h, win_ref[:, d + q0:d + q0 + chunk])
        vq = _dot(h, win_ref[:, 2 * d + q0:2 * d + q0 + chunk])
        for r0 in range(0, chunk, LANES):
            c0 = q0 + r0
            sl = c0 // LANES
            cs = slice(c0, c0 + LANES)
            rs = slice(r0, r0 + LANES)
            cv = cq[:, rs] * vq[:, rs]
            ext_ref[sl, halo:halo + tm, :] = cv
            u = cv * wdw_ref[kw - 1:kw, cs]
            for k in range(kw - 1):
                off = halo - (kw - 1) + k
                u = u + ext_ref[sl, off:off + tm, :] * wdw_ref[k:k + 1, cs]
            y_ref[:, cs] = (bq[:, rs] * u).astype(BF16)
    o_ref[...] = x + _dot(y_ref[...], wout_ref[...])


def _conv_mixer(x, nw, win, wdw, wout, *, tm=512):
    b, s, d = x.shape
    halo = 8
    return pl.pallas_call(
        functools.partial(_conv_mixer_body, tm=tm, halo=halo, chunk=256),
        grid=(b, s // tm),
        in_specs=[
            pl.BlockSpec((None, tm, d), lambda bi, i: (bi, i, 0)),
            _const_spec((1, d)),
            _const_spec(win.shape),
            _const_spec(wdw.shape),
            _const_spec(wout.shape),
        ],
        out_specs=pl.BlockSpec((None, tm, d), lambda bi, i: (bi, i, 0)),
        out_shape=jax.ShapeDtypeStruct((b, s, d), F32),
        scratch_shapes=[pltpu.VMEM((d // LANES, halo + tm, LANES), F32),
                        pltpu.VMEM((tm, d), BF16)],
        compiler_params=_params(("arbitrary", "arbitrary")),
        name="conv_mixer",
    )(x, nw, win, wdw, wout)


def _fox_in_body(x_ref, nw_ref, wqkv_ref, wf_ref, bf_ref, qg_ref, kg_ref, e_ref, et_ref,
                 q_ref, k_ref, v_ref, ct_ref, carry_ref, *, n_heads):
    i = pl.program_id(1)
    aw = q_ref.shape[-1]
    x = x_ref[...]
    h = _rmsnorm(x, nw_ref[...]).astype(BF16)
    qkv = _dot(h, wqkv_ref[...])

    def head_norm(t, gain):
        ms = _dot((t * t).astype(BF16), e_ref[...])
        r = lax.rsqrt(ms + RMS_EPS)
        r_hi = r.astype(BF16)
        r_lo = (r - r_hi.astype(F32)).astype(BF16)
        rb = _dot(r_hi, et_ref[...]) + _dot(r_lo, et_ref[...])
        return t * rb * gain

    q_ref[...] = head_norm(qkv[:, :aw], qg_ref[...]).astype(BF16)
    k_ref[...] = head_norm(qkv[:, aw:2 * aw], kg_ref[...]).astype(BF16)
    v_ref[...] = qkv[:, 2 * aw:].astype(BF16)

    fl = _dot(h, wf_ref[...]) + bf_ref[...]
    logf = -_softplus(-fl) * LOG2E
    local = _lane_cumsum(logf.T[0:n_heads, :])

    @pl.when(i == 0)
    def _():
        carry_ref[...] = jnp.zeros_like(carry_ref)

    cum = local + carry_ref[:, 0:1]
    ct_ref[...] = cum
    tm = cum.shape[-1]
    carry_ref[...] = jnp.broadcast_to(cum[:, tm - 1:tm], carry_ref.shape)


def _fox_in(x, nw, wqkv, wf, bf, qg, kg, e, et, *, n_heads, tm=512):
    b, s, d = x.shape
    aw = wqkv.shape[1] // 3
    row = lambda bi, i: (bi, i, 0)
    return pl.pallas_call(
        functools.partial(_fox_in_body, n_heads=n_heads),
        grid=(b, s // tm),
        in_specs=[
            pl.BlockSpec((None, tm, d), row),
            _const_spec((1, d)),
            _const_spec(wqkv.shape),
            _const_spec(wf.shape),
            _const_spec(bf.shape),
            _const_spec(qg.shape),
            _const_spec(kg.shape),
            _const_spec(e.shape),
            _const_spec(et.shape),
        ],
        out_specs=[
            pl.BlockSpec((None, tm, aw), row),
            pl.BlockSpec((None, tm, aw), row),
            pl.BlockSpec((None, tm, aw), row),
            pl.BlockSpec((None, n_heads, tm), lambda bi, i: (bi, 0, i)),
        ],
        out_shape=[
            jax.ShapeDtypeStruct((b, s, aw), BF16),
            jax.ShapeDtypeStruct((b, s, aw), BF16),
            jax.ShapeDtypeStruct((b, s, aw), BF16),
            jax.ShapeDtypeStruct((b, n_heads, s), F32),
        ],
        scratch_shapes=[pltpu.VMEM((n_heads, LANES), F32)],
        compiler_params=_params(("arbitrary", "arbitrary")),
        name="fox_in",
    )(x, nw, wqkv, wf, bf, qg, kg, e, et)


def _fox_attn_body(thr_ref, cs_ref, ce_ref, q_ref, k_ref, v_ref, ck_ref, o_ref, p_ref, acc_ref,
                   *, t, n_heads, nsub):
    bi, pr, step = pl.program_id(0), pl.program_id(1), pl.program_id(2)
    hd = ATTN_HEAD_DIM
    lane = lax.broadcasted_iota(jnp.int32, (t, 2 * hd), 1)
    lo = lane < hd
    r0 = bi * n_heads + 2 * pr
    thr = thr_ref[0]
    tiles = [nsub * step + sub for sub in range(nsub)]

    def first_block(i):
        cs0, cs1 = cs_ref[r0, i], cs_ref[r0 + 1, i]

        def needed(j):
            jj = jnp.maximum(j, 0)
            near = (cs0 - ce_ref[r0, jj] >= thr) | (cs1 - ce_ref[r0 + 1, jj] >= thr)
            return (j >= 0) & near

        return lax.while_loop(needed, lambda j: j - 1, i - 1) + 1

    j0s = [first_block(i) for i in tiles]

    def split_heads(sub):
        q2 = q_ref[sub * t:(sub + 1) * t, :]
        zero = jnp.zeros_like(q2)
        return jnp.where(lo, q2, zero), jnp.where(lo, zero, q2)

    def scores(qs, j):
        k2 = k_ref[pl.ds(pl.multiple_of(j * t, t), t), :]
        ck = ck_ref[j]
        return tuple(lax.dot_general(qs[hh], k2, (((1,), (1,)), ((), ())),
                                     preferred_element_type=F32) - ck[hh:hh + 1, :]
                     for hh in range(2))

    def causal(s):
        row = lax.broadcasted_iota(jnp.int32, (t, t), 0)
        col = lax.broadcasted_iota(jnp.int32, (t, t), 1)
        return tuple(jnp.where(row >= col, sh, NEG_BIG) for sh in s)

    @pl.when(thr_ref[1] <= 0.0)
    def _():
        for sub in range(nsub):
            i, j0, qs = tiles[sub], j0s[sub], split_heads(sub)

            def update(j, s, ms, ls, acc):
                v2 = v_ref[pl.ds(pl.multiple_of(j * t, t), t), :]
                new_m, new_l, alphas, pvs = [], [], [], []
                for hh in range(2):
                    m_new = jnp.maximum(ms[hh], jnp.max(s[hh], axis=1, keepdims=True))
                    alpha = jnp.exp2(ms[hh] - m_new)
                    p = jnp.exp2(s[hh] - m_new)
                    new_l.append(alpha * ls[hh] + jnp.sum(p, axis=1, keepdims=True))
                    new_m.append(m_new)
                    alphas.append(alpha)
                    pvs.append(_dot(p.astype(BF16), v2))
                acc = acc * jnp.where(lo, alphas[0], alphas[1]) + jnp.where(lo, pvs[0], pvs[1])
                return tuple(new_m), tuple(new_l), acc

            def body(j, carry, qs=qs, update=update):
                s, ms, ls, acc = carry
                s_next = scores(qs, j + 1)
                ms, ls, acc = update(j, s, ms, ls, acc)
                return s_next, ms, ls, acc

            m0 = jnp.full((t, 1), NEG_BIG, F32)
            l0 = jnp.zeros((t, 1), F32)
            carry = (scores(qs, j0), (m0, m0), (l0, l0), jnp.zeros((t, 2 * hd), F32))
            s, ms, ls, acc = lax.fori_loop(j0, i, body, carry)
            _, ls, acc = update(i, causal(s), ms, ls, acc)
            o_ref[sub * t:(sub + 1) * t, :] = (acc / jnp.where(lo, ls[0], ls[1])).astype(o_ref.dtype)

    @pl.when(thr_ref[1] > 0.0)
    def _():
        e_lo = jnp.where(lo, 1.0, 0.0).astype(BF16)
        e_hi = jnp.where(lo, 0.0, 1.0).astype(BF16)

        def pv(j, p16):
            v2 = v_ref[pl.ds(pl.multiple_of(j * t, t), t), :]
            vb = jnp.concatenate(
                [jnp.concatenate([v2 * e_lo, e_lo], axis=1),
                 jnp.concatenate([v2 * e_hi, e_hi], axis=1)], axis=0)
            return _dot(p16, vb)

        def weights(s, offs):
            return jnp.concatenate([jnp.exp2(s[hh] - offs[hh]).astype(BF16) for hh in range(2)], axis=1)

        qss, offss = [], []
        for sub in range(nsub):
            i, qs = tiles[sub], split_heads(sub)
            ck_i = ck_ref[i]
            offs = tuple(thr_ref[2] - jnp.broadcast_to(ck_i[hh:hh + 1, :], (LANES, t)).T[:, 0:1]
                         for hh in range(2))
            p_ref[sub, 0] = weights(causal(scores(qs, i)), offs)
            acc_ref[sub] = jnp.zeros((t, 4 * hd), F32)
            qss.append(qs)
            offss.append(offs)

        jprevs = []
        for sub in range(nsub):
            i, j0 = tiles[sub], j0s[sub]

            def body(j, jprev, sub=sub, j0=j0):
                slot = (j - j0) & 1
                p_new = weights(scores(qss[sub], j), offss[sub])
                acc_ref[sub] += pv(jprev, p_ref[sub, slot])
                p_ref[sub, 1 - slot] = p_new
                return j

            jprevs.append(lax.fori_loop(j0, i, body, i))

        for sub in range(nsub):
            i, j0 = tiles[sub], j0s[sub]
            acc = acc_ref[sub] + pv(jprevs[sub], p_ref[sub, (i - j0) & 1])
            o_ref[sub * t:(sub + 1) * t, :] = (acc[:, :2 * hd] / acc[:, 2 * hd:]).astype(o_ref.dtype)


def _fox_attn(thr, cs, ce, q, k, v, ck, *, t, n_heads, nsub=4):
    b, s, aw = q.shape
    pairs = aw // LANES
    smem = pl.BlockSpec(memory_space=pltpu.SMEM)
    return pl.pallas_call(
        functools.partial(_fox_attn_body, t=t, n_heads=n_heads, nsub=nsub),
        grid=(b, pairs, s // (nsub * t)),
        in_specs=[
            smem, smem, smem,
            pl.BlockSpec((None, nsub * t, LANES), lambda bi, p, i: (bi, i, p)),
            pl.BlockSpec((None, s, LANES), lambda bi, p, i: (bi, 0, p)),
            pl.BlockSpec((None, s, LANES), lambda bi, p, i: (bi, 0, p)),
            pl.BlockSpec((None, None, s // t, 2, t), lambda bi, p, i: (bi, p, 0, 0, 0)),
        ],
        out_specs=pl.BlockSpec((None, nsub * t, LANES), lambda bi, p, i: (bi, i, p)),
        out_shape=jax.ShapeDtypeStruct((b, s, aw), BF16),
        scratch_shapes=[pltpu.VMEM((nsub, 2, t, 2 * t), BF16), pltpu.VMEM((nsub, t, 2 * LANES), F32)],
        compiler_params=_params(("arbitrary", "arbitrary", "arbitrary")),
        name="fox_attn",
    )(thr, cs, ce, q, k, v, ck)


def _proj_res_body(y_ref, w_ref, x_ref, o_ref):
    o_ref[...] = x_ref[...] + _dot(y_ref[...], w_ref[...])


def _proj_res(y, w, x, *, tm=512):
    m, kdim = y.shape
    d = w.shape[1]
    return pl.pallas_call(
        _proj_res_body,
        grid=(m // tm,),
        in_specs=[
            pl.BlockSpec((tm, kdim), lambda i: (i, 0)),
            _const_spec(w.shape),
            pl.BlockSpec((tm, d), lambda i: (i, 0)),
        ],
        out_specs=pl.BlockSpec((tm, d), lambda i: (i, 0)),
        out_shape=jax.ShapeDtypeStruct((m, d), F32),
        compiler_params=_params(("arbitrary",)),
        name="proj_res",
    )(y, w, x)


def _ssd_in_body(x_ref, nw_ref, wz_ref, wxbc_ref, wdt_ref, cw_ref, cb_ref, dtb_ref,
                 z_ref, xs_ref, b_ref, c_ref, dt_ref, ext_ref, *, tm, halo, chunk):
    i = pl.program_id(1)
    di = xs_ref.shape[-1]
    gn = b_ref.shape[-1]
    cdim = wxbc_ref.shape[-1]
    x = x_ref[...]
    h = _rmsnorm(x, nw_ref[...]).astype(BF16)

    @pl.when(i == 0)
    def _():
        ext_ref[:, 0:halo, :] = jnp.zeros((cdim // LANES, halo, LANES), F32)

    @pl.when(i > 0)
    def _():
        ext_ref[:, 0:halo, :] = ext_ref[:, tm:tm + halo, :]

    dt_ref[...] = _softplus(_dot(h, wdt_ref[...]) + dtb_ref[...])
    kw = cw_ref.shape[0]
    outs = ((xs_ref, 0, di), (b_ref, di, di + gn), (c_ref, di + gn, cdim))
    for o_ref, lo_c, hi_c in outs:
        for q0 in range(lo_c, hi_c, chunk):
            raw = _dot(h, wxbc_ref[:, q0:q0 + chunk])
            if q0 < di:
                z_ref[:, q0:q0 + chunk] = _dot(h, wz_ref[:, q0:q0 + chunk]).astype(z_ref.dtype)
            for r0 in range(0, chunk, LANES):
                c0 = q0 + r0
                sl = c0 // LANES
                cs = slice(c0, c0 + LANES)
                cur = raw[:, r0:r0 + LANES]
                ext_ref[sl, halo:halo + tm, :] = cur
                u = cur * cw_ref[kw - 1:kw, cs] + cb_ref[:, cs]
                for k in range(kw - 1):
                    off = halo - (kw - 1) + k
                    u = u + ext_ref[sl, off:off + tm, :] * cw_ref[k:k + 1, cs]
                o_ref[:, c0 - lo_c:c0 - lo_c + LANES] = _silu(u).astype(o_ref.dtype)


def _ssd_in(x, nw, wz, wxbc, wdt, cw, cb, dtb, *, tm=256):
    b, s, d = x.shape
    di = wz.shape[1]
    cdim = wxbc.shape[1]
    gn = (cdim - di) // 2
    halo = 8
    chunk = 512
    row = lambda bi, i: (bi, i, 0)
    return pl.pallas_call(
        functools.partial(_ssd_in_body, tm=tm, halo=halo, chunk=chunk),
        grid=(b, s // tm),
        in_specs=[
            pl.BlockSpec((None, tm, d), row),
            _const_spec((1, d)),
            _const_spec(wz.shape),
            _const_spec(wxbc.shape),
            _const_spec(wdt.shape),
            _const_spec(cw.shape),
            _const_spec(cb.shape),
            _const_spec(dtb.shape),
        ],
        out_specs=[
            pl.BlockSpec((None, tm, di), row),
            pl.BlockSpec((None, tm, di), row),
            pl.BlockSpec((None, tm, gn), row),
            pl.BlockSpec((None, tm, gn), row),
            pl.BlockSpec((None, tm, LANES), row),
        ],
        out_shape=[
            jax.ShapeDtypeStruct((b, s, di), BF16),
            jax.ShapeDtypeStruct((b, s, di), BF16),
            jax.ShapeDtypeStruct((b, s, gn), BF16),
            jax.ShapeDtypeStruct((b, s, gn), BF16),
            jax.ShapeDtypeStruct((b, s, LANES), F32),
        ],
        scratch_shapes=[pltpu.VMEM((cdim // LANES, halo + tm, LANES), F32)],
        compiler_params=_params(("arbitrary", "arbitrary")),
        name="ssd_in",
    )(x, nw, wz, wxbc, wdt, cw, cb, dtb)


def _ssd_scan_body(xs_ref, b_ref, c_ref, dt_ref, alog_ref, dskip_ref, eh_ref, y_ref, st_ref,
                   *, n_heads):
    ci = pl.program_id(1)
    L = xs_ref.shape[0]
    P, N, G = SSM_HEAD_DIM, SSM_STATE, SSM_GROUPS
    hpg = n_heads // G
    gw = hpg * P

    @pl.when(ci == 0)
    def _():
        st_ref[...] = jnp.zeros_like(st_ref)

    lane_h = lax.broadcasted_iota(jnp.int32, (1, LANES), 1)
    a = jnp.where(lane_h < n_heads, -jnp.exp(alog_ref[...]), 0.0)
    dt = dt_ref[...]
    dtT = dt.T
    row = lax.broadcasted_iota(jnp.int32, (L, L), 0)
    col = lax.broadcasted_iota(jnp.int32, (L, L), 1)
    tril = row >= col
    da = dt * a
    da_hi = da.astype(BF16)
    rest = da - da_hi.astype(F32)
    da_mid = rest.astype(BF16)
    da_lo = (rest - da_mid.astype(F32)).astype(BF16)
    parts = _dot(jnp.where(tril, 1.0, 0.0).astype(BF16),
                 jnp.concatenate([da_hi, da_mid, da_lo], axis=1))
    acum = parts[:, :LANES] + parts[:, LANES:2 * LANES] + parts[:, 2 * LANES:]
    acumT = acum.T
    a_last = acum[L - 1:L, :]

    eh = eh_ref[...]
    sdt_b = _dot((jnp.exp(a_last - acum) * dt).astype(BF16), eh)
    ea_b = _dot(jnp.exp(acum).astype(BF16), eh)
    dl = jnp.broadcast_to(jnp.exp(a_last), (16, LANES))
    dl_hi = dl.astype(BF16)
    dl_lo = (dl - dl_hi.astype(F32)).astype(BF16)
    decay_last = (_dot(dl_hi, eh) + _dot(dl_lo, eh))[0:1, :]
    xs16 = xs_ref[...]
    xs = xs16.astype(F32)
    xw = (xs * sdt_b).astype(BF16)

    head_of_lane = lax.broadcasted_iota(jnp.int32, (L, gw), 1) // P
    head_mask = [jnp.where(head_of_lane == k, 1.0, 0.0).astype(BF16) for k in range(hpg)]

    cbs, y_offs = [], []
    for g in range(G):
        bg = b_ref[:, g * N:(g + 1) * N]
        cg = c_ref[:, g * N:(g + 1) * N]
        cbs.append(lax.dot_general(cg, bg, (((1,), (1,)), ((), ())), preferred_element_type=F32))
        gs = slice(g * gw, (g + 1) * gw)
        st = st_ref[g]
        y_offs.append(_dot(cg, st.astype(BF16)) * ea_b[:, gs])
        contrib = lax.dot_general(bg, xw[:, gs], (((0,), (0,)), ((), ())),
                                  preferred_element_type=F32)
        st_ref[g] = st * decay_last[:, gs] + contrib

    for g in range(G):
        gs = slice(g * gw, (g + 1) * gw)
        cb, y_off = cbs[g], y_offs[g]
        ws = []
        for k in range(hpg):
            hidx = g * hpg + k
            seg = acum[:, hidx:hidx + 1] - acumT[hidx:hidx + 1, :]
            decay = jnp.exp(jnp.where(tril, seg, NEG_BIG))
            ws.append((cb * decay * dtT[hidx:hidx + 1, :]).astype(BF16))
        xg = xs16[:, gs]
        xblk = jnp.concatenate([xg * head_mask[k] for k in range(hpg)], axis=0)
        y_diag = _dot(jnp.concatenate(ws, axis=1), xblk)
        y_ref[:, gs] = (y_diag + y_off + dskip_ref[:, gs] * xs[:, gs]).astype(y_ref.dtype)


def _ssd_scan(xs, bm, cm, dt, alog, dskip, eh, *, n_heads):
    b, s, di = xs.shape
    gn = bm.shape[-1]
    L = SSM_CHUNK
    row = lambda bi, i: (bi, i, 0)
    return pl.pallas_call(
        functools.partial(_ssd_scan_body, n_heads=n_heads),
        grid=(b, s // L),
        in_specs=[
            pl.BlockSpec((None, L, di), row),
            pl.BlockSpec((None, L, gn), row),
            pl.BlockSpec((None, L, gn), row),
            pl.BlockSpec((None, L, LANES), row),
            _const_spec(alog.shape),
            _const_spec(dskip.shape),
            _const_spec(eh.shape),
        ],
        out_specs=pl.BlockSpec((None, L, di), row),
        out_shape=jax.ShapeDtypeStruct((b, s, di), BF16),
        scratch_shapes=[pltpu.VMEM((SSM_GROUPS, SSM_STATE, di // SSM_GROUPS), F32)],
        compiler_params=_params(("arbitrary", "arbitrary")),
        name="ssd_scan",
    )(xs, bm, cm, dt, alog, dskip, eh)


def _ssd_out_body(y_ref, z_ref, nw_ref, w_ref, x_ref, o_ref, yn_ref):
    di = y_ref.shape[-1]
    gw = di // SSM_GROUPS
    for g in range(SSM_GROUPS):
        gs = slice(g * gw, (g + 1) * gw)
        yg = y_ref[:, gs].astype(F32) * _silu(z_ref[:, gs].astype(F32))
        yn_ref[:, gs] = _rmsnorm(yg, nw_ref[:, gs]).astype(BF16)
    o_ref[...] = x_ref[...] + _dot(yn_ref[...], w_ref[...])


def _ssd_out(y, z, nw, w, x, *, tm=512):
    m, di = y.shape
    d = w.shape[1]
    return pl.pallas_call(
        _ssd_out_body,
        grid=(m // tm,),
        in_specs=[
            pl.BlockSpec((tm, di), lambda i: (i, 0)),
            pl.BlockSpec((tm, di), lambda i: (i, 0)),
            _const_spec(nw.shape),
            _const_spec(w.shape),
            pl.BlockSpec((tm, d), lambda i: (i, 0)),
        ],
        out_specs=pl.BlockSpec((tm, d), lambda i: (i, 0)),
        out_shape=jax.ShapeDtypeStruct((m, d), F32),
        scratch_shapes=[pltpu.VMEM((tm, di), BF16)],
        compiler_params=_params(("arbitrary",)),
        name="ssd_out",
    )(y, z, nw, w, x)


def _pad_lanes(a, width=LANES):
    return jnp.pad(a, ((0, 0), (0, width - a.shape[-1])))


def _conv_layer(x, nw, w_in, w_dw, w_out):
    return _conv_mixer(x, nw[None], w_in.astype(BF16), w_dw, w_out.astype(BF16))


def _fox_layer(x, nw, w_in, b_f, q_gain, k_gain, w_out, *, t=256):
    b, s, d = x.shape
    n_heads = b_f.shape[0]
    hd = ATTN_HEAD_DIM
    aw = n_heads * hd
    wqkv = w_in[:, :3 * aw].astype(BF16)
    wf = _pad_lanes(w_in[:, 3 * aw:]).astype(BF16)
    bf = _pad_lanes(b_f[None])
    qg = jnp.tile(q_gain, n_heads)[None] * (hd ** -0.5 * LOG2E)
    kg = jnp.tile(k_gain, n_heads)[None]
    smax = 1.02 * LOG2E * hd ** 0.5 * jnp.max(jnp.abs(q_gain)) * jnp.max(jnp.abs(k_gain))
    fast = (2.0 * smax <= FAST_PATH_MAX_LOG2).astype(F32)
    thr = jnp.stack([-(SKIP_LOG2 + 2.0 * smax), fast, smax]).astype(F32)
    head_of = jnp.arange(aw) // hd
    e = (head_of[:, None] == jnp.arange(LANES)[None, :]).astype(F32)
    et = e.T.astype(BF16)
    e = (e / hd).astype(BF16)
    q, k, v, ct = _fox_in(x, nw[None], wqkv, wf, bf, qg, kg, e, et, n_heads=n_heads)
    ck = ct.reshape(b, n_heads // 2, 2, s // t, t).transpose(0, 1, 3, 2, 4)
    cs = ct[:, :, 0::t].reshape(b * n_heads, s // t)
    ce = ct[:, :, t - 1::t].reshape(b * n_heads, s // t)
    attn = _fox_attn(thr, cs, ce, q, k, v, ck, t=t, n_heads=n_heads)
    return _proj_res(attn.reshape(b * s, aw), w_out.astype(BF16), x.reshape(b * s, d)).reshape(b, s, d)


def _ssd_layer(x, nw, w_in, conv_w, conv_b, dt_bias, a_log, d_skip, norm_w, w_out):
    b, s, d = x.shape
    n_heads = a_log.shape[0]
    di = n_heads * SSM_HEAD_DIM
    cdim = conv_w.shape[1]
    wz = w_in[:, :di].astype(BF16)
    wxbc = w_in[:, di:di + cdim].astype(BF16)
    wdt = _pad_lanes(w_in[:, di + cdim:]).astype(BF16)
    z, xs, bm, cm, dt = _ssd_in(x, nw[None], wz, wxbc, wdt, conv_w, conv_b[None],
                                _pad_lanes(dt_bias[None]))
    head_of = jnp.arange(di) // SSM_HEAD_DIM
    eh = (jnp.arange(LANES)[:, None] == head_of[None, :]).astype(BF16)
    dskip = jnp.repeat(d_skip, SSM_HEAD_DIM)[None]
    y = _ssd_scan(xs, bm, cm, dt, _pad_lanes(a_log[None]), dskip, eh, n_heads=n_heads)
    m = b * s
    return _ssd_out(y.reshape(m, di), z.reshape(m, di), norm_w[None], w_out.astype(BF16),
                    x.reshape(m, d)).reshape(b, s, d)


def kernel(x, mix_norm, ffn_norm, ffn_w_gu, ffn_w_down, conv_w_in, conv_w_dw, conv_w_out, fox_w_in, fox_b_f, fox_q_gain, fox_k_gain, fox_w_out, ssd_w_in, ssd_conv_w, ssd_conv_b, ssd_dt_bias, ssd_a_log, ssd_d, ssd_norm_w, ssd_w_out):
    b, s, d = x.shape
    depth = mix_norm.shape[0]
    dff = ffn_w_down.shape[1]
    for i in range(depth):
        kind, j = i % 3, i // 3
        if kind == 0:
            x = _conv_layer(x, mix_norm[i], conv_w_in[j], conv_w_dw[j], conv_w_out[j])
        elif kind == 1:
            x = _fox_layer(x, mix_norm[i], fox_w_in[j], fox_b_f[j], fox_q_gain[j], fox_k_gain[j],
                           fox_w_out[j])
        else:
            x = _ssd_layer(x, mix_norm[i], ssd_w_in[j], ssd_conv_w[j], ssd_conv_b[j],
                           ssd_dt_bias[j], ssd_a_log[j], ssd_d[j], ssd_norm_w[j], ssd_w_out[j])
        wg = ffn_w_gu[i][:, :dff].astype(BF16)
        wu = ffn_w_gu[i][:, dff:].astype(BF16)
        x = _ffn(x.reshape(b * s, d), ffn_norm[i][None], wg, wu,
                 ffn_w_down[i].astype(BF16)).reshape(b, s, d)
    return x
```

```python
import functools

import jax
import jax.numpy as jnp
from jax import lax
from jax.experimental import pallas as pl
from jax.experimental.pallas import tpu as pltpu

F32 = jnp.float32
BF16 = jnp.bfloat16

RMS_EPS = 1e-6
ATTN_HEAD_DIM = 64
SSM_HEAD_DIM = 64
SSM_GROUPS = 8
SSM_STATE = 128
SSM_CHUNK = 128
LANES = 128
NEG_BIG = -1e30
LOG2E = 1.4426950408889634
SKIP_LOG2 = 64.0
FAST_PATH_MAX_LOG2 = 100.0

VMEM_LIMIT = 56 * 1024 * 1024


def _params(sem):
    return pltpu.CompilerParams(dimension_semantics=sem, vmem_limit_bytes=VMEM_LIMIT)


def _const_spec(shape):
    nd = len(shape)
    return pl.BlockSpec(shape, lambda *_: (0,) * nd, pipeline_mode=pl.Buffered(1))


def _rmsnorm(x, w):
    return x * lax.rsqrt(jnp.mean(x * x, axis=-1, keepdims=True) + RMS_EPS) * w


def _dot(a, b):
    return jnp.dot(a, b, preferred_element_type=F32)


def _softplus(x):
    return jnp.maximum(x, 0.0) + jnp.log1p(jnp.exp(-jnp.abs(x)))


def _silu(x):
    return x * jax.nn.sigmoid(x)


def _lane_cumsum(x):
    n = x.shape[-1]
    lane = lax.broadcasted_iota(jnp.int32, x.shape, x.ndim - 1)
    s = 1
    while s < n:
        x = x + jnp.where(lane >= s, pltpu.roll(x, s, x.ndim - 1), 0.0)
        s *= 2
    return x


def _ffn_body(x_ref, nw_ref, wg_ref, wu_ref, wd_ref, o_ref, a_ref, *, chunk):
    x = x_ref[...]
    h = _rmsnorm(x, nw_ref[...]).astype(BF16)
    for c in range(wg_ref.shape[1] // chunk):
        sl = slice(c * chunk, (c + 1) * chunk)
        g = _dot(h, wg_ref[:, sl])
        u = _dot(h, wu_ref[:, sl])
        a_ref[:, sl] = (_silu(g) * u).astype(BF16)
    o_ref[...] = x + _dot(a_ref[...], wd_ref[...])


def _ffn(x, nw, wg, wu, wd, *, tm=1024, chunk=256):
    m, d = x.shape
    dff = wg.shape[1]
    return pl.pallas_call(
        functools.partial(_ffn_body, chunk=chunk),
        grid=(m // tm,),
        in_specs=[
            pl.BlockSpec((tm, d), lambda i: (i, 0)),
            _const_spec((1, d)),
            _const_spec((d, dff)),
            _const_spec((d, dff)),
            _const_spec((dff, d)),
        ],
        out_specs=pl.BlockSpec((tm, d), lambda i: (i, 0)),
        out_shape=jax.ShapeDtypeStruct((m, d), F32),
        scratch_shapes=[pltpu.VMEM((tm, dff), BF16)],
        compiler_params=_params(("arbitrary",)),
        name="ffn",
    )(x, nw, wg, wu, wd)


def _conv_mixer_body(x_ref, nw_ref, win_ref, wdw_ref, wout_ref, o_ref, ext_ref, y_ref,
                     *, tm, halo, chunk):
    i = pl.program_id(1)
    d = x_ref.shape[-1]
    x = x_ref[...]
    h = _rmsnorm(x, nw_ref[...]).astype(BF16)

    @pl.when(i == 0)
    def _():
        ext_ref[:, 0:halo, :] = jnp.zeros((d // LANES, halo, LANES), F32)

    @pl.when(i > 0)
    def _():
        ext_ref[:, 0:halo, :] = ext_ref[:, tm:tm + halo, :]

    kw = wdw_ref.shape[0]
    for q0 in range(0, d, chunk):
        bq = _dot(h, win_ref[:, q0:q0 + chunk])
        cq = _dot(h, win_ref[:, d + q0:d + q0 + chunk])
        vq = _dot(h, win_ref[:, 2 * d + q0:2 * d + q0 + chunk])
        for r0 in range(0, chunk, LANES):
            c0 = q0 + r0
            sl = c0 // LANES
            cs = slice(c0, c0 + LANES)
            rs = slice(r0, r0 + LANES)
            cv = cq[:, rs] * vq[:, rs]
            ext_ref[sl, halo:halo + tm, :] = cv
            u = cv * wdw_ref[kw - 1:kw, cs]
            for k in range(kw - 1):
                off = halo - (kw - 1) + k
                u = u + ext_ref[sl, off:off + tm, :] * wdw_ref[k:k + 1, cs]
            y_ref[:, cs] = (bq[:, rs] * u).astype(BF16)
    o_ref[...] = x + _dot(y_ref[...], wout_ref[...])


def _conv_mixer(x, nw, win, wdw, wout, *, tm=1024):
    b, s, d = x.shape
    halo = 8
    return pl.pallas_call(
        functools.partial(_conv_mixer_body, tm=tm, halo=halo, chunk=256),
        grid=(b, s // tm),
        in_specs=[
            pl.BlockSpec((None, tm, d), lambda bi, i: (bi, i, 0)),
            _const_spec((1, d)),
            _const_spec(win.shape),
            _const_spec(wdw.shape),
            _const_spec(wout.shape),
        ],
        out_specs=pl.BlockSpec((None, tm, d), lambda bi, i: (bi, i, 0)),
        out_shape=jax.ShapeDtypeStruct((b, s, d), F32),
        scratch_shapes=[pltpu.VMEM((d // LANES, halo + tm, LANES), F32),
                        pltpu.VMEM((tm, d), BF16)],
        compiler_params=_params(("arbitrary", "arbitrary")),
        name="conv_mixer",
    )(x, nw, win, wdw, wout)


def _fox_in_body(x_ref, nw_ref, wqkv_ref, wf_ref, bf_ref, qg_ref, kg_ref, e_ref, et_ref,
                 q_ref, k_ref, v_ref, ct_ref, carry_ref, *, n_heads):
    i = pl.program_id(1)
    aw = q_ref.shape[-1]
    x = x_ref[...]
    h = _rmsnorm(x, nw_ref[...]).astype(BF16)
    qkv = _dot(h, wqkv_ref[...])

    def head_norm(t, gain):
        ms = _dot((t * t).astype(BF16), e_ref[...])
        r = lax.rsqrt(ms + RMS_EPS)
        r_hi = r.astype(BF16)
        r_lo = (r - r_hi.astype(F32)).astype(BF16)
        rb = _dot(r_hi, et_ref[...]) + _dot(r_lo, et_ref[...])
        return t * rb * gain

    q_ref[...] = head_norm(qkv[:, :aw], qg_ref[...]).astype(BF16)
    k_ref[...] = head_norm(qkv[:, aw:2 * aw], kg_ref[...]).astype(BF16)
    v_ref[...] = qkv[:, 2 * aw:].astype(BF16)

    fl = _dot(h, wf_ref[...]) + bf_ref[...]
    logf = -_softplus(-fl) * LOG2E
    local = _lane_cumsum(logf.T[0:n_heads, :])

    @pl.when(i == 0)
    def _():
        carry_ref[...] = jnp.zeros_like(carry_ref)

    cum = local + carry_ref[:, 0:1]
    ct_ref[...] = cum
    tm = cum.shape[-1]
    carry_ref[...] = jnp.broadcast_to(cum[:, tm - 1:tm], carry_ref.shape)


def _fox_in(x, nw, wqkv, wf, bf, qg, kg, e, et, *, n_heads, tm=1024):
    b, s, d = x.shape
    aw = wqkv.shape[1] // 3
    row = lambda bi, i: (bi, i, 0)
    return pl.pallas_call(
        functools.partial(_fox_in_body, n_heads=n_heads),
        grid=(b, s // tm),
        in_specs=[
            pl.BlockSpec((None, tm, d), row),
            _const_spec((1, d)),
            _const_spec(wqkv.shape),
            _const_spec(wf.shape),
            _const_spec(bf.shape),
            _const_spec(qg.shape),
            _const_spec(kg.shape),
            _const_spec(e.shape),
            _const_spec(et.shape),
        ],
        out_specs=[
            pl.BlockSpec((None, tm, aw), row),
            pl.BlockSpec((None, tm, aw), row),
            pl.BlockSpec((None, tm, aw), row),
            pl.BlockSpec((None, n_heads, tm), lambda bi, i: (bi, 0, i)),
        ],
        out_shape=[
            jax.ShapeDtypeStruct((b, s, aw), BF16),
            jax.ShapeDtypeStruct((b, s, aw), BF16),
            jax.ShapeDtypeStruct((b, s, aw), BF16),
            jax.ShapeDtypeStruct((b, n_heads, s), F32),
        ],
        scratch_shapes=[pltpu.VMEM((n_heads, LANES), F32)],
        compiler_params=_params(("arbitrary", "arbitrary")),
        name="fox_in",
    )(x, nw, wqkv, wf, bf, qg, kg, e, et)


def _fox_attn_body(thr_ref, cs_ref, ce_ref, q_ref, k_ref, v_ref, ck_ref, o_ref, p_ref, acc_ref,
                   *, t, n_heads, nsub):
    bi, pr, step = pl.program_id(0), pl.program_id(1), pl.program_id(2)
    hd = ATTN_HEAD_DIM
    lane = lax.broadcasted_iota(jnp.int32, (t, 2 * hd), 1)
    lo = lane < hd
    r0 = bi * n_heads + 2 * pr
    thr = thr_ref[0]
    tiles = [nsub * step + sub for sub in range(nsub)]

    def first_block(i):
        cs0, cs1 = cs_ref[r0, i], cs_ref[r0 + 1, i]

        def needed(j):
            jj = jnp.maximum(j, 0)
            near = (cs0 - ce_ref[r0, jj] >= thr) | (cs1 - ce_ref[r0 + 1, jj] >= thr)
            return (j >= 0) & near

        return lax.while_loop(needed, lambda j: j - 1, i - 1) + 1

    j0s = [first_block(i) for i in tiles]

    def split_heads(sub):
        q2 = q_ref[sub * t:(sub + 1) * t, :]
        zero = jnp.zeros_like(q2)
        return jnp.where(lo, q2, zero), jnp.where(lo, zero, q2)

    def scores(qs, j):
        k2 = k_ref[pl.ds(pl.multiple_of(j * t, t), t), :]
        ck = ck_ref[j]
        return tuple(lax.dot_general(qs[hh], k2, (((1,), (1,)), ((), ())),
                                     preferred_element_type=F32) - ck[hh:hh + 1, :]
                     for hh in range(2))

    def causal(s):
        row = lax.broadcasted_iota(jnp.int32, (t, t), 0)
        col = lax.broadcasted_iota(jnp.int32, (t, t), 1)
        return tuple(jnp.where(row >= col, sh, NEG_BIG) for sh in s)

    @pl.when(thr_ref[1] <= 0.0)
    def _():
        for sub in range(nsub):
            i, j0, qs = tiles[sub], j0s[sub], split_heads(sub)

            def update(j, s, ms, ls, acc):
                v2 = v_ref[pl.ds(pl.multiple_of(j * t, t), t), :]
                new_m, new_l, alphas, pvs = [], [], [], []
                for hh in range(2):
                    m_new = jnp.maximum(ms[hh], jnp.max(s[hh], axis=1, keepdims=True))
                    alpha = jnp.exp2(ms[hh] - m_new)
                    p = jnp.exp2(s[hh] - m_new)
                    new_l.append(alpha * ls[hh] + jnp.sum(p, axis=1, keepdims=True))
                    new_m.append(m_new)
                    alphas.append(alpha)
                    pvs.append(_dot(p.astype(BF16), v2))
                acc = acc * jnp.where(lo, alphas[0], alphas[1]) + jnp.where(lo, pvs[0], pvs[1])
                return tuple(new_m), tuple(new_l), acc

            def body(j, carry, qs=qs, update=update):
                s, ms, ls, acc = carry
                s_next = scores(qs, j + 1)
                ms, ls, acc = update(j, s, ms, ls, acc)
                return s_next, ms, ls, acc

            m0 = jnp.full((t, 1), NEG_BIG, F32)
            l0 = jnp.zeros((t, 1), F32)
            carry = (scores(qs, j0), (m0, m0), (l0, l0), jnp.zeros((t, 2 * hd), F32))
            s, ms, ls, acc = lax.fori_loop(j0, i, body, carry)
            _, ls, acc = update(i, causal(s), ms, ls, acc)
            o_ref[sub * t:(sub + 1) * t, :] = (acc / jnp.where(lo, ls[0], ls[1])).astype(o_ref.dtype)

    @pl.when(thr_ref[1] > 0.0)
    def _():
        e_lo = jnp.where(lo, 1.0, 0.0).astype(BF16)
        e_hi = jnp.where(lo, 0.0, 1.0).astype(BF16)

        def pv(j, p16):
            v2 = v_ref[pl.ds(pl.multiple_of(j * t, t), t), :]
            vb = jnp.concatenate(
                [jnp.concatenate([v2 * e_lo, e_lo], axis=1),
                 jnp.concatenate([v2 * e_hi, e_hi], axis=1)], axis=0)
            return _dot(p16, vb)

        def weights(s, offs):
            return jnp.concatenate([jnp.exp2(s[hh] - offs[hh]).astype(BF16) for hh in range(2)], axis=1)

        qss, offss = [], []
        for sub in range(nsub):
            i, qs = tiles[sub], split_heads(sub)
            ck_i = ck_ref[i]
            offs = tuple(thr_ref[2] - jnp.broadcast_to(ck_i[hh:hh + 1, :], (LANES, t)).T[:, 0:1]
                         for hh in range(2))
            p_ref[sub, 0] = weights(causal(scores(qs, i)), offs)
            acc_ref[sub] = jnp.zeros((t, 4 * hd), F32)
            qss.append(qs)
            offss.append(offs)

        jprevs = []
        for sub in range(nsub):
            i, j0 = tiles[sub], j0s[sub]

            def body(j, jprev, sub=sub, j0=j0):
                slot = (j - j0) & 1
                p_new = weights(scores(qss[sub], j), offss[sub])
                acc_ref[sub] += pv(jprev, p_ref[sub, slot])
                p_ref[sub, 1 - slot] = p_new
                return j

            jprevs.append(lax.fori_loop(j0, i, body, i))

        for sub in range(nsub):
            i, j0 = tiles[sub], j0s[sub]
            acc = acc_ref[sub] + pv(jprevs[sub], p_ref[sub, (i - j0) & 1])
            o_ref[sub * t:(sub + 1) * t, :] = (acc[:, :2 * hd] / acc[:, 2 * hd:]).astype(o_ref.dtype)


def _fox_attn(thr, cs, ce, q, k, v, ck, *, t, n_heads, nsub=4):
    b, s, aw = q.shape
    pairs = aw // LANES
    smem = pl.BlockSpec(memory_space=pltpu.SMEM)
    return pl.pallas_call(
        functools.partial(_fox_attn_body, t=t, n_heads=n_heads, nsub=nsub),
        grid=(b, pairs, s // (nsub * t)),
        in_specs=[
            smem, smem, smem,
            pl.BlockSpec((None, nsub * t, LANES), lambda bi, p, i: (bi, i, p)),
            pl.BlockSpec((None, s, LANES), lambda bi, p, i: (bi, 0, p)),
            pl.BlockSpec((None, s, LANES), lambda bi, p, i: (bi, 0, p)),
            pl.BlockSpec((None, None, s // t, 2, t), lambda bi, p, i: (bi, p, 0, 0, 0)),
        ],
        out_specs=pl.BlockSpec((None, nsub * t, LANES), lambda bi, p, i: (bi, i, p)),
        out_shape=jax.ShapeDtypeStruct((b, s, aw), BF16),
        scratch_shapes=[pltpu.VMEM((nsub, 2, t, 2 * t), BF16), pltpu.VMEM((nsub, t, 2 * LANES), F32)],
        compiler_params=_params(("arbitrary", "arbitrary", "arbitrary")),
        name="fox_attn",
    )(thr, cs, ce, q, k, v, ck)


def _proj_res_body(y_ref, w_ref, x_ref, o_ref):
    o_ref[...] = x_ref[...] + _dot(y_ref[...], w_ref[...])


def _proj_res(y, w, x, *, tm=1024):
    m, kdim = y.shape
    d = w.shape[1]
    return pl.pallas_call(
        _proj_res_body,
        grid=(m // tm,),
        in_specs=[
            pl.BlockSpec((tm, kdim), lambda i: (i, 0)),
            _const_spec(w.shape),
            pl.BlockSpec((tm, d), lambda i: (i, 0)),
        ],
        out_specs=pl.BlockSpec((tm, d), lambda i: (i, 0)),
        out_shape=jax.ShapeDtypeStruct((m, d), F32),
        compiler_params=_params(("arbitrary",)),
        name="proj_res",
    )(y, w, x)


def _ssd_in_body(x_ref, nw_ref, wz_ref, wxbc_ref, wdt_ref, cw_ref, cb_ref, dtb_ref,
                 z_ref, xs_ref, b_ref, c_ref, dt_ref, ext_ref, *, tm, halo, chunk):
    i = pl.program_id(1)
    di = xs_ref.shape[-1]
    gn = b_ref.shape[-1]
    cdim = wxbc_ref.shape[-1]
    x = x_ref[...]
    h = _rmsnorm(x, nw_ref[...]).astype(BF16)

    @pl.when(i == 0)
    def _():
        ext_ref[:, 0:halo, :] = jnp.zeros((cdim // LANES, halo, LANES), F32)

    @pl.when(i > 0)
    def _():
        ext_ref[:, 0:halo, :] = ext_ref[:, tm:tm + halo, :]

    dt_ref[...] = _softplus(_dot(h, wdt_ref[...]) + dtb_ref[...])
    kw = cw_ref.shape[0]
    outs = ((xs_ref, 0, di), (b_ref, di, di + gn), (c_ref, di + gn, cdim))
    for o_ref, lo_c, hi_c in outs:
        for q0 in range(lo_c, hi_c, chunk):
            raw = _dot(h, wxbc_ref[:, q0:q0 + chunk])
            if q0 < di:
                z_ref[:, q0:q0 + chunk] = _dot(h, wz_ref[:, q0:q0 + chunk]).astype(z_ref.dtype)
            for r0 in range(0, chunk, LANES):
                c0 = q0 + r0
                sl = c0 // LANES
                cs = slice(c0, c0 + LANES)
                cur = raw[:, r0:r0 + LANES]
                ext_ref[sl, halo:halo + tm, :] = cur
                u = cur * cw_ref[kw - 1:kw, cs] + cb_ref[:, cs]
                for k in range(kw - 1):
                    off = halo - (kw - 1) + k
                    u = u + ext_ref[sl, off:off + tm, :] * cw_ref[k:k + 1, cs]
                o_ref[:, c0 - lo_c:c0 - lo_c + LANES] = _silu(u).astype(o_ref.dtype)


def _ssd_in(x, nw, wz, wxbc, wdt, cw, cb, dtb, *, tm=512):
    b, s, d = x.shape
    di = wz.shape[1]
    cdim = wxbc.shape[1]
    gn = (cdim - di) // 2
    halo = 8
    chunk = 512
    row = lambda bi, i: (bi, i, 0)
    return pl.pallas_call(
        functools.partial(_ssd_in_body, tm=tm, halo=halo, chunk=chunk),
        grid=(b, s // tm),
        in_specs=[
            pl.BlockSpec((None, tm, d), row),
            _const_spec((1, d)),
            _const_spec(wz.shape),
            _const_spec(wxbc.shape),
            _const_spec(wdt.shape),
            _const_spec(cw.shape),
            _const_spec(cb.shape),
            _const_spec(dtb.shape),
        ],
        out_specs=[
            pl.BlockSpec((None, tm, di), row),
            pl.BlockSpec((None, tm, di), row),
            pl.BlockSpec((None, tm, gn), row),
            pl.BlockSpec((None, tm, gn), row),
            pl.BlockSpec((None, tm, LANES), row),
        ],
        out_shape=[
            jax.ShapeDtypeStruct((b, s, di), BF16),
            jax.ShapeDtypeStruct((b, s, di), BF16),
            jax.ShapeDtypeStruct((b, s, gn), BF16),
            jax.ShapeDtypeStruct((b, s, gn), BF16),
            jax.ShapeDtypeStruct((b, s, LANES), F32),
        ],
        scratch_shapes=[pltpu.VMEM((cdim // LANES, halo + tm, LANES), F32)],
        compiler_params=_params(("arbitrary", "arbitrary")),
        name="ssd_in",
    )(x, nw, wz, wxbc, wdt, cw, cb, dtb)


def _ssd_scan_body(xs_ref, b_ref, c_ref, dt_ref, alog_ref, dskip_ref, eh_ref, y_ref, st_ref,
                   *, n_heads, nc):
    ci = pl.program_id(1)
    L = SSM_CHUNK
    P, N, G = SSM_HEAD_DIM, SSM_STATE, SSM_GROUPS
    hpg = n_heads // G
    gw = hpg * P

    @pl.when(ci == 0)
    def _():
        st_ref[...] = jnp.zeros_like(st_ref)

    lane_h = lax.broadcasted_iota(jnp.int32, (1, LANES), 1)
    a = jnp.where(lane_h < n_heads, -jnp.exp(alog_ref[...]), 0.0)
    row = lax.broadcasted_iota(jnp.int32, (L, L), 0)
    col = lax.broadcasted_iota(jnp.int32, (L, L), 1)
    tril = row >= col
    tril16 = jnp.where(tril, 1.0, 0.0).astype(BF16)
    eh = eh_ref[...]
    head_of_lane = lax.broadcasted_iota(jnp.int32, (L, gw), 1) // P
    head_mask = [jnp.where(head_of_lane == k, 1.0, 0.0).astype(BF16) for k in range(hpg)]

    def prologue(c):
        rs = slice(c * L, (c + 1) * L)
        dt = dt_ref[rs, :]
        dtT = dt.T
        da = dt * a
        da_hi = da.astype(BF16)
        rest = da - da_hi.astype(F32)
        da_mid = rest.astype(BF16)
        da_lo = (rest - da_mid.astype(F32)).astype(BF16)
        parts = _dot(tril16, jnp.concatenate([da_hi, da_mid, da_lo], axis=1))
        acum = parts[:, :LANES] + parts[:, LANES:2 * LANES] + parts[:, 2 * LANES:]
        acumT = acum.T
        a_last = acum[L - 1:L, :]
        sdt_b = _dot((jnp.exp(a_last - acum) * dt).astype(BF16), eh)
        ea_b = _dot(jnp.exp(acum).astype(BF16), eh)
        dl = jnp.broadcast_to(jnp.exp(a_last), (16, LANES))
        dl_hi = dl.astype(BF16)
        dl_lo = (dl - dl_hi.astype(F32)).astype(BF16)
        decay_last = (_dot(dl_hi, eh) + _dot(dl_lo, eh))[0:1, :]
        return dtT, acum, acumT, sdt_b, ea_b, decay_last

    def main(c, pro):
        dtT, acum, acumT, sdt_b, ea_b, decay_last = pro
        rs = slice(c * L, (c + 1) * L)
        xs16 = xs_ref[rs, :]
        xs = xs16.astype(F32)
        xw = (xs * sdt_b).astype(BF16)
        cbs, y_offs = [], []
        for g in range(G):
            bg = b_ref[rs, g * N:(g + 1) * N]
            cg = c_ref[rs, g * N:(g + 1) * N]
            cbs.append(lax.dot_general(cg, bg, (((1,), (1,)), ((), ())), preferred_element_type=F32))
            gs = slice(g * gw, (g + 1) * gw)
            st = st_ref[g]
            y_offs.append(_dot(cg, st.astype(BF16)) * ea_b[:, gs])
            contrib = lax.dot_general(bg, xw[:, gs], (((0,), (0,)), ((), ())),
                                      preferred_element_type=F32)
            st_ref[g] = st * decay_last[:, gs] + contrib

        for g in range(G):
            gs = slice(g * gw, (g + 1) * gw)
            cb, y_off = cbs[g], y_offs[g]
            ws = []
            for k in range(hpg):
                hidx = g * hpg + k
                seg = acum[:, hidx:hidx + 1] - acumT[hidx:hidx + 1, :]
                decay = jnp.exp(jnp.where(tril, seg, NEG_BIG))
                ws.append((cb * decay * dtT[hidx:hidx + 1, :]).astype(BF16))
            xg = xs16[:, gs]
            xblk = jnp.concatenate([xg * head_mask[k] for k in range(hpg)], axis=0)
            y_diag = _dot(jnp.concatenate(ws, axis=1), xblk)
            y_ref[rs, gs] = (y_diag + y_off + dskip_ref[:, gs] * xs[:, gs]).astype(y_ref.dtype)

    pros = [prologue(c) for c in range(nc)]
    for c in range(nc):
        main(c, pros[c])


def _ssd_scan(xs, bm, cm, dt, alog, dskip, eh, *, n_heads, nc=4):
    b, s, di = xs.shape
    gn = bm.shape[-1]
    L = nc * SSM_CHUNK
    row = lambda bi, i: (bi, i, 0)
    return pl.pallas_call(
        functools.partial(_ssd_scan_body, n_heads=n_heads, nc=nc),
        grid=(b, s // L),
        in_specs=[
            pl.BlockSpec((None, L, di), row),
            pl.BlockSpec((None, L, gn), row),
            pl.BlockSpec((None, L, gn), row),
            pl.BlockSpec((None, L, LANES), row),
            _const_spec(alog.shape),
            _const_spec(dskip.shape),
            _const_spec(eh.shape),
        ],
        out_specs=pl.BlockSpec((None, L, di), row),
        out_shape=jax.ShapeDtypeStruct((b, s, di), BF16),
        scratch_shapes=[pltpu.VMEM((SSM_GROUPS, SSM_STATE, di // SSM_GROUPS), F32)],
        compiler_params=_params(("arbitrary", "arbitrary")),
        name="ssd_scan",
    )(xs, bm, cm, dt, alog, dskip, eh)


def _ssd_out_body(y_ref, z_ref, nw_ref, w_ref, x_ref, o_ref, yn_ref):
    di = y_ref.shape[-1]
    gw = di // SSM_GROUPS
    for g in range(SSM_GROUPS):
        gs = slice(g * gw, (g + 1) * gw)
        yg = y_ref[:, gs].astype(F32) * _silu(z_ref[:, gs].astype(F32))
        yn_ref[:, gs] = _rmsnorm(yg, nw_ref[:, gs]).astype(BF16)
    o_ref[...] = x_ref[...] + _dot(yn_ref[...], w_ref[...])


def _ssd_out(y, z, nw, w, x, *, tm=1024):
    m, di = y.shape
    d = w.shape[1]
    return pl.pallas_call(
        _ssd_out_body,
        grid=(m // tm,),
        in_specs=[
            pl.BlockSpec((tm, di), lambda i: (i, 0)),
            pl.BlockSpec((tm, di), lambda i: (i, 0)),
            _const_spec(nw.shape),
            _const_spec(w.shape),
            pl.BlockSpec((tm, d), lambda i: (i, 0)),
        ],
        out_specs=pl.BlockSpec((tm, d), lambda i: (i, 0)),
        out_shape=jax.ShapeDtypeStruct((m, d), F32),
        scratch_shapes=[pltpu.VMEM((tm, di), BF16)],
        compiler_params=_params(("arbitrary",)),
        name="ssd_out",
    )(y, z, nw, w, x)


def _pad_lanes(a, width=LANES):
    return jnp.pad(a, ((0, 0), (0, width - a.shape[-1])))


def _conv_layer(x, nw, w_in, w_dw, w_out):
    return _conv_mixer(x, nw[None], w_in.astype(BF16), w_dw, w_out.astype(BF16))


def _fox_layer(x, nw, w_in, b_f, q_gain, k_gain, w_out, *, t=256):
    b, s, d = x.shape
    n_heads = b_f.shape[0]
    hd = ATTN_HEAD_DIM
    aw = n_heads * hd
    wqkv = w_in[:, :3 * aw].astype(BF16)
    wf = _pad_lanes(w_in[:, 3 * aw:]).astype(BF16)
    bf = _pad_lanes(b_f[None])
    qg = jnp.tile(q_gain, n_heads)[None] * (hd ** -0.5 * LOG2E)
    kg = jnp.tile(k_gain, n_heads)[None]
    smax = 1.02 * LOG2E * hd ** 0.5 * jnp.max(jnp.abs(q_gain)) * jnp.max(jnp.abs(k_gain))
    fast = (2.0 * smax <= FAST_PATH_MAX_LOG2).astype(F32)
    thr = jnp.stack([-(SKIP_LOG2 + 2.0 * smax), fast, smax]).astype(F32)
    head_of = jnp.arange(aw) // hd
    e = (head_of[:, None] == jnp.arange(LANES)[None, :]).astype(F32)
    et = e.T.astype(BF16)
    e = (e / hd).astype(BF16)
    q, k, v, ct = _fox_in(x, nw[None], wqkv, wf, bf, qg, kg, e, et, n_heads=n_heads)
    ck = ct.reshape(b, n_heads // 2, 2, s // t, t).transpose(0, 1, 3, 2, 4)
    cs = ct[:, :, 0::t].reshape(b * n_heads, s // t)
    ce = ct[:, :, t - 1::t].reshape(b * n_heads, s // t)
    attn = _fox_attn(thr, cs, ce, q, k, v, ck, t=t, n_heads=n_heads)
    return _proj_res(attn.reshape(b * s, aw), w_out.astype(BF16), x.reshape(b * s, d)).reshape(b, s, d)


def _ssd_layer(x, nw, w_in, conv_w, conv_b, dt_bias, a_log, d_skip, norm_w, w_out):
    b, s, d = x.shape
    n_heads = a_log.shape[0]
    di = n_heads * SSM_HEAD_DIM
    cdim = conv_w.shape[1]
    wz = w_in[:, :di].astype(BF16)
    wxbc = w_in[:, di:di + cdim].astype(BF16)
    wdt = _pad_lanes(w_in[:, di + cdim:]).astype(BF16)
    z, xs, bm, cm, dt = _ssd_in(x, nw[None], wz, wxbc, wdt, conv_w, conv_b[None],
                                _pad_lanes(dt_bias[None]))
    head_of = jnp.arange(di) // SSM_HEAD_DIM
    eh = (jnp.arange(LANES)[:, None] == head_of[None, :]).astype(BF16)
    dskip = jnp.repeat(d_skip, SSM_HEAD_DIM)[None]
    y = _ssd_scan(xs, bm, cm, dt, _pad_lanes(a_log[None]), dskip, eh, n_heads=n_heads)
    m = b * s
    return _ssd_out(y.reshape(m, di), z.reshape(m, di), norm_w[None], w_out.astype(BF16),
                    x.reshape(m, d)).reshape(b, s, d)


def kernel(x, mix_norm, ffn_norm, ffn_w_gu, ffn_w_down, conv_w_in, conv_w_dw, conv_w_out, fox_w_in, fox_b_f, fox_q_gain, fox_k_gain, fox_w_out, ssd_w_in, ssd_conv_w, ssd_conv_b, ssd_dt_bias, ssd_a_log, ssd_d, ssd_norm_w, ssd_w_out):
    b, s, d = x.shape
    depth = mix_norm.shape[0]
    dff = ffn_w_down.shape[1]
    for i in range(depth):
        kind, j = i % 3, i // 3
        if kind == 0:
            x = _conv_layer(x, mix_norm[i], conv_w_in[j], conv_w_dw[j], conv_w_out[j])
        elif kind == 1:
            x = _fox_layer(x, mix_norm[i], fox_w_in[j], fox_b_f[j], fox_q_gain[j], fox_k_gain[j],
                           fox_w_out[j])
        else:
            x = _ssd_layer(x, mix_norm[i], ssd_w_in[j], ssd_conv_w[j], ssd_conv_b[j],
                           ssd_dt_bias[j], ssd_a_log[j], ssd_d[j], ssd_norm_w[j], ssd_w_out[j])
        wg = ffn_w_gu[i][:, :dff].astype(BF16)
        wu = ffn_w_gu[i][:, dff:].astype(BF16)
        x = _ffn(x.reshape(b * s, d), ffn_norm[i][None], wg, wu,
                 ffn_w_down[i].astype(BF16)).reshape(b, s, d)
    return x
```

```python
import functools

import jax
import jax.numpy as jnp
from jax import lax
from jax.experimental import pallas as pl
from jax.experimental.pallas import tpu as pltpu

F32 = jnp.float32
BF16 = jnp.bfloat16

RMS_EPS = 1e-6
ATTN_HEAD_DIM = 64
SSM_HEAD_DIM = 64
SSM_GROUPS = 8
SSM_STATE = 128
SSM_CHUNK = 128
LANES = 128
NEG_BIG = -1e30
LOG2E = 1.4426950408889634
SKIP_LOG2 = 64.0
FAST_PATH_MAX_LOG2 = 100.0

VMEM_LIMIT = 56 * 1024 * 1024


def _params(sem):
    return pltpu.CompilerParams(dimension_semantics=sem, vmem_limit_bytes=VMEM_LIMIT)


def _const_spec(shape):
    nd = len(shape)
    return pl.BlockSpec(shape, lambda *_: (0,) * nd, pipeline_mode=pl.Buffered(1))


def _rmsnorm(x, w):
    return x * lax.rsqrt(jnp.mean(x * x, axis=-1, keepdims=True) + RMS_EPS) * w


def _dot(a, b):
    return jnp.dot(a, b, preferred_element_type=F32)


def _softplus(x):
    return jnp.maximum(x, 0.0) + jnp.log1p(jnp.exp(-jnp.abs(x)))


def _silu(x):
    return x * jax.nn.sigmoid(x)


def _lane_cumsum(x):
    n = x.shape[-1]
    lane = lax.broadcasted_iota(jnp.int32, x.shape, x.ndim - 1)
    s = 1
    while s < n:
        x = x + jnp.where(lane >= s, pltpu.roll(x, s, x.ndim - 1), 0.0)
        s *= 2
    return x


def _ffn_body(x_ref, nw_ref, wgu_ref, wd_ref, o_ref, a_ref, *, chunk):
    x = x_ref[...]
    h = _rmsnorm(x, nw_ref[...]).astype(BF16)
    dff = wd_ref.shape[0]
    for c0 in range(0, dff, chunk):
        g = _dot(h, wgu_ref[:, c0:c0 + chunk])
        u = _dot(h, wgu_ref[:, dff + c0:dff + c0 + chunk])
        a_ref[:, c0:c0 + chunk] = (_silu(g) * u).astype(BF16)
    o_ref[...] = x + _dot(a_ref[...], wd_ref[...])


def _ffn(x, nw, wgu, wd, *, tm=1024, chunk=256):
    m, d = x.shape
    dff = wd.shape[0]
    return pl.pallas_call(
        functools.partial(_ffn_body, chunk=chunk),
        grid=(m // tm,),
        in_specs=[
            pl.BlockSpec((tm, d), lambda i: (i, 0)),
            _const_spec((1, d)),
            _const_spec((d, 2 * dff)),
            _const_spec((dff, d)),
        ],
        out_specs=pl.BlockSpec((tm, d), lambda i: (i, 0)),
        out_shape=jax.ShapeDtypeStruct((m, d), F32),
        scratch_shapes=[pltpu.VMEM((tm, dff), BF16)],
        compiler_params=_params(("arbitrary",)),
        name="ffn",
    )(x, nw, wgu, wd)


def _conv_mixer_body(x_ref, nw_ref, win_ref, wdw_ref, wout_ref, o_ref, ext_ref, y_ref,
                     *, tm, halo, chunk):
    i = pl.program_id(1)
    d = x_ref.shape[-1]
    x = x_ref[...]
    h = _rmsnorm(x, nw_ref[...]).astype(BF16)

    @pl.when(i == 0)
    def _():
        ext_ref[:, 0:halo, :] = jnp.zeros((d // LANES, halo, LANES), F32)

    @pl.when(i > 0)
    def _():
        ext_ref[:, 0:halo, :] = ext_ref[:, tm:tm + halo, :]

    kw = wdw_ref.shape[0]
    for q0 in range(0, d, chunk):
        bq = _dot(h, win_ref[:, q0:q0 + chunk])
        cq = _dot(h, win_ref[:, d + q0:d + q0 + chunk])
        vq = _dot(h, win_ref[:, 2 * d + q0:2 * d + q0 + chunk])
        for r0 in range(0, chunk, LANES):
            c0 = q0 + r0
            sl = c0 // LANES
            cs = slice(c0, c0 + LANES)
            rs = slice(r0, r0 + LANES)
            cv = cq[:, rs] * vq[:, rs]
            ext_ref[sl, halo:halo + tm, :] = cv
            u = cv * wdw_ref[kw - 1:kw, cs]
            for k in range(kw - 1):
                off = halo - (kw - 1) + k
                u = u + ext_ref[sl, off:off + tm, :] * wdw_ref[k:k + 1, cs]
            y_ref[:, cs] = (bq[:, rs] * u).astype(BF16)
    o_ref[...] = x + _dot(y_ref[...], wout_ref[...])


def _conv_mixer(x, nw, win, wdw, wout, *, tm=1024):
    b, s, d = x.shape
    halo = 8
    return pl.pallas_call(
        functools.partial(_conv_mixer_body, tm=tm, halo=halo, chunk=256),
        grid=(b, s // tm),
        in_specs=[
            pl.BlockSpec((None, tm, d), lambda bi, i: (bi, i, 0)),
            _const_spec((1, d)),
            _const_spec(win.shape),
            _const_spec(wdw.shape),
            _const_spec(wout.shape),
        ],
        out_specs=pl.BlockSpec((None, tm, d), lambda bi, i: (bi, i, 0)),
        out_shape=jax.ShapeDtypeStruct((b, s, d), F32),
        scratch_shapes=[pltpu.VMEM((d // LANES, halo + tm, LANES), F32),
                        pltpu.VMEM((tm, d), BF16)],
        compiler_params=_params(("arbitrary", "arbitrary")),
        name="conv_mixer",
    )(x, nw, win, wdw, wout)


def _fox_in_body(x_ref, nw_ref, wqkv_ref, wf_ref, bf_ref, qg_ref, kg_ref, e_ref, et_ref,
                 q_ref, k_ref, v_ref, ct_ref, carry_ref, *, n_heads):
    i = pl.program_id(1)
    aw = q_ref.shape[-1]
    x = x_ref[...]
    h = _rmsnorm(x, nw_ref[...]).astype(BF16)
    qkv = _dot(h, wqkv_ref[...])

    def head_norm(t, gain):
        ms = _dot((t * t).astype(BF16), e_ref[...])
        r = lax.rsqrt(ms + RMS_EPS)
        r_hi = r.astype(BF16)
        r_lo = (r - r_hi.astype(F32)).astype(BF16)
        rb = _dot(jnp.concatenate([r_hi, r_lo], axis=1), et_ref[...])
        return t * rb * gain

    q_ref[...] = head_norm(qkv[:, :aw], qg_ref[...]).astype(BF16)
    k_ref[...] = head_norm(qkv[:, aw:2 * aw], kg_ref[...]).astype(BF16)
    v_ref[...] = qkv[:, 2 * aw:].astype(BF16)

    fl = _dot(h, wf_ref[...]) + bf_ref[...]
    logf = -_softplus(-fl) * LOG2E
    local = _lane_cumsum(logf.T[0:n_heads, :])

    @pl.when(i == 0)
    def _():
        carry_ref[...] = jnp.zeros_like(carry_ref)

    cum = local + carry_ref[:, 0:1]
    ct_ref[...] = cum
    tm = cum.shape[-1]
    carry_ref[...] = jnp.broadcast_to(cum[:, tm - 1:tm], carry_ref.shape)


def _fox_in(x, nw, wqkv, wf, bf, qg, kg, e, et, *, n_heads, tm=1024):
    b, s, d = x.shape
    aw = wqkv.shape[1] // 3
    row = lambda bi, i: (bi, i, 0)
    return pl.pallas_call(
        functools.partial(_fox_in_body, n_heads=n_heads),
        grid=(b, s // tm),
        in_specs=[
            pl.BlockSpec((None, tm, d), row),
            _const_spec((1, d)),
            _const_spec(wqkv.shape),
            _const_spec(wf.shape),
            _const_spec(bf.shape),
            _const_spec(qg.shape),
            _const_spec(kg.shape),
            _const_spec(e.shape),
            _const_spec(et.shape),
        ],
        out_specs=[
            pl.BlockSpec((None, tm, aw), row),
            pl.BlockSpec((None, tm, aw), row),
            pl.BlockSpec((None, tm, aw), row),
            pl.BlockSpec((None, n_heads, tm), lambda bi, i: (bi, 0, i)),
        ],
        out_shape=[
            jax.ShapeDtypeStruct((b, s, aw), BF16),
            jax.ShapeDtypeStruct((b, s, aw), BF16),
            jax.ShapeDtypeStruct((b, s, aw), BF16),
            jax.ShapeDtypeStruct((b, n_heads, s), F32),
        ],
        scratch_shapes=[pltpu.VMEM((n_heads, LANES), F32)],
        compiler_params=_params(("arbitrary", "arbitrary")),
        name="fox_in",
    )(x, nw, wqkv, wf, bf, qg, kg, e, et)


def _fox_attn_body(thr_ref, cs_ref, ce_ref, q_ref, k_ref, v_ref, ck_ref, o_ref, p_ref, acc_ref,
                   *, t, n_heads, nsub):
    bi, pr, step = pl.program_id(0), pl.program_id(1), pl.program_id(2)
    hd = ATTN_HEAD_DIM
    lane = lax.broadcasted_iota(jnp.int32, (t, 2 * hd), 1)
    lo = lane < hd
    r0 = bi * n_heads + 2 * pr
    thr = thr_ref[0]
    tiles = [nsub * step + sub for sub in range(nsub)]

    def first_block(i):
        cs0, cs1 = cs_ref[r0, i], cs_ref[r0 + 1, i]

        def needed(j):
            jj = jnp.maximum(j, 0)
            near = (cs0 - ce_ref[r0, jj] >= thr) | (cs1 - ce_ref[r0 + 1, jj] >= thr)
            return (j >= 0) & near

        return lax.while_loop(needed, lambda j: j - 1, i - 1) + 1

    j0s = [first_block(i) for i in tiles]

    def split_heads(sub):
        q2 = q_ref[sub * t:(sub + 1) * t, :]
        zero = jnp.zeros_like(q2)
        return jnp.where(lo, q2, zero), jnp.where(lo, zero, q2)

    def scores(qs, j):
        k2 = k_ref[pl.ds(pl.multiple_of(j * t, t), t), :]
        ck = ck_ref[j]
        return tuple(lax.dot_general(qs[hh], k2, (((1,), (1,)), ((), ())),
                                     preferred_element_type=F32) - ck[hh:hh + 1, :]
                     for hh in range(2))

    def causal(s):
        row = lax.broadcasted_iota(jnp.int32, (t, t), 0)
        col = lax.broadcasted_iota(jnp.int32, (t, t), 1)
        return tuple(jnp.where(row >= col, sh, NEG_BIG) for sh in s)

    @pl.when(thr_ref[1] <= 0.0)
    def _():
        for sub in range(nsub):
            i, j0, qs = tiles[sub], j0s[sub], split_heads(sub)

            def update(j, s, ms, ls, acc):
                v2 = v_ref[pl.ds(pl.multiple_of(j * t, t), t), :]
                new_m, new_l, alphas, pvs = [], [], [], []
                for hh in range(2):
                    m_new = jnp.maximum(ms[hh], jnp.max(s[hh], axis=1, keepdims=True))
                    alpha = jnp.exp2(ms[hh] - m_new)
                    p = jnp.exp2(s[hh] - m_new)
                    new_l.append(alpha * ls[hh] + jnp.sum(p, axis=1, keepdims=True))
                    new_m.append(m_new)
                    alphas.append(alpha)
                    pvs.append(_dot(p.astype(BF16), v2))
                acc = acc * jnp.where(lo, alphas[0], alphas[1]) + jnp.where(lo, pvs[0], pvs[1])
                return tuple(new_m), tuple(new_l), acc

            def body(j, carry, qs=qs, update=update):
                s, ms, ls, acc = carry
                s_next = scores(qs, j + 1)
                ms, ls, acc = update(j, s, ms, ls, acc)
                return s_next, ms, ls, acc

            m0 = jnp.full((t, 1), NEG_BIG, F32)
            l0 = jnp.zeros((t, 1), F32)
            carry = (scores(qs, j0), (m0, m0), (l0, l0), jnp.zeros((t, 2 * hd), F32))
            s, ms, ls, acc = lax.fori_loop(j0, i, body, carry)
            _, ls, acc = update(i, causal(s), ms, ls, acc)
            o_ref[sub * t:(sub + 1) * t, :] = (acc / jnp.where(lo, ls[0], ls[1])).astype(o_ref.dtype)

    @pl.when(thr_ref[1] > 0.0)
    def _():
        e_lo = jnp.where(lo, 1.0, 0.0).astype(BF16)
        e_hi = jnp.where(lo, 0.0, 1.0).astype(BF16)

        def pv(j, p16):
            v2 = v_ref[pl.ds(pl.multiple_of(j * t, t), t), :]
            vb = jnp.concatenate(
                [jnp.concatenate([v2 * e_lo, e_lo], axis=1),
                 jnp.concatenate([v2 * e_hi, e_hi], axis=1)], axis=0)
            return _dot(p16, vb)

        def weights(s, offs):
            return jnp.concatenate([jnp.exp2(s[hh] - offs[hh]).astype(BF16) for hh in range(2)], axis=1)

        qss, offss = [], []
        for sub in range(nsub):
            i, qs = tiles[sub], split_heads(sub)
            ck_i = ck_ref[i]
            offs = tuple(thr_ref[2] - jnp.broadcast_to(ck_i[hh:hh + 1, :], (LANES, t)).T[:, 0:1]
                         for hh in range(2))
            p_ref[sub, 0] = weights(causal(scores(qs, i)), offs)
            acc_ref[sub] = jnp.zeros((t, 4 * hd), F32)
            qss.append(qs)
            offss.append(offs)

        jprevs = []
        for sub in range(nsub):
            i, j0 = tiles[sub], j0s[sub]

            def body(j, jprev, sub=sub, j0=j0):
                slot = (j - j0) & 1
                p_new = weights(scores(qss[sub], j), offss[sub])
                acc_ref[sub] += pv(jprev, p_ref[sub, slot])
                p_ref[sub, 1 - slot] = p_new
                return j

            jprevs.append(lax.fori_loop(j0, i, body, i))

        for sub in range(nsub):
            i, j0 = tiles[sub], j0s[sub]
            acc = acc_ref[sub] + pv(jprevs[sub], p_ref[sub, (i - j0) & 1])
            o_ref[sub * t:(sub + 1) * t, :] = (acc[:, :2 * hd] / acc[:, 2 * hd:]).astype(o_ref.dtype)


def _fox_attn(thr, cs, ce, q, k, v, ck, *, t, n_heads, nsub=4):
    b, s, aw = q.shape
    pairs = aw // LANES
    smem = pl.BlockSpec(memory_space=pltpu.SMEM)
    return pl.pallas_call(
        functools.partial(_fox_attn_body, t=t, n_heads=n_heads, nsub=nsub),
        grid=(b, pairs, s // (nsub * t)),
        in_specs=[
            smem, smem, smem,
            pl.BlockSpec((None, nsub * t, LANES), lambda bi, p, i: (bi, i, p)),
            pl.BlockSpec((None, s, LANES), lambda bi, p, i: (bi, 0, p)),
            pl.BlockSpec((None, s, LANES), lambda bi, p, i: (bi, 0, p)),
            pl.BlockSpec((None, None, s // t, 2, t), lambda bi, p, i: (bi, p, 0, 0, 0)),
        ],
        out_specs=pl.BlockSpec((None, nsub * t, LANES), lambda bi, p, i: (bi, i, p)),
        out_shape=jax.ShapeDtypeStruct((b, s, aw), BF16),
        scratch_shapes=[pltpu.VMEM((nsub, 2, t, 2 * t), BF16), pltpu.VMEM((nsub, t, 2 * LANES), F32)],
        compiler_params=_params(("arbitrary", "arbitrary", "arbitrary")),
        name="fox_attn",
    )(thr, cs, ce, q, k, v, ck)


def _proj_res_body(y_ref, w_ref, x_ref, o_ref):
    o_ref[...] = x_ref[...] + _dot(y_ref[...], w_ref[...])


def _proj_res(y, w, x, *, tm=1024):
    m, kdim = y.shape
    d = w.shape[1]
    return pl.pallas_call(
        _proj_res_body,
        grid=(m // tm,),
        in_specs=[
            pl.BlockSpec((tm, kdim), lambda i: (i, 0)),
            _const_spec(w.shape),
            pl.BlockSpec((tm, d), lambda i: (i, 0)),
        ],
        out_specs=pl.BlockSpec((tm, d), lambda i: (i, 0)),
        out_shape=jax.ShapeDtypeStruct((m, d), F32),
        compiler_params=_params(("arbitrary",)),
        name="proj_res",
    )(y, w, x)


def _ssd_in_body(x_ref, nw_ref, w_ref, wdt_ref, cw_ref, cb_ref, dtb_ref,
                 z_ref, xs_ref, b_ref, c_ref, dt_ref, ext_ref, *, tm, halo, chunk):
    i = pl.program_id(1)
    di = xs_ref.shape[-1]
    gn = b_ref.shape[-1]
    cdim = cw_ref.shape[-1]
    x = x_ref[...]
    h = _rmsnorm(x, nw_ref[...]).astype(BF16)

    @pl.when(i == 0)
    def _():
        ext_ref[:, 0:halo, :] = jnp.zeros((cdim // LANES, halo, LANES), F32)

    @pl.when(i > 0)
    def _():
        ext_ref[:, 0:halo, :] = ext_ref[:, tm:tm + halo, :]

    dt_ref[...] = _softplus(_dot(h, wdt_ref[...]) + dtb_ref[...])
    kw = cw_ref.shape[0]
    outs = ((xs_ref, 0, di), (b_ref, di, di + gn), (c_ref, di + gn, cdim))
    for o_ref, lo_c, hi_c in outs:
        for q0 in range(lo_c, hi_c, chunk):
            raw = _dot(h, w_ref[:, di + q0:di + q0 + chunk])
            if q0 < di:
                z_ref[:, q0:q0 + chunk] = _dot(h, w_ref[:, q0:q0 + chunk]).astype(z_ref.dtype)
            for r0 in range(0, chunk, LANES):
                c0 = q0 + r0
                sl = c0 // LANES
                cs = slice(c0, c0 + LANES)
                cur = raw[:, r0:r0 + LANES]
                ext_ref[sl, halo:halo + tm, :] = cur
                u = cur * cw_ref[kw - 1:kw, cs] + cb_ref[:, cs]
                for k in range(kw - 1):
                    off = halo - (kw - 1) + k
                    u = u + ext_ref[sl, off:off + tm, :] * cw_ref[k:k + 1, cs]
                o_ref[:, c0 - lo_c:c0 - lo_c + LANES] = _silu(u).astype(o_ref.dtype)


def _ssd_in(x, nw, w, wdt, cw, cb, dtb, *, di, tm=512):
    b, s, d = x.shape
    cdim = cw.shape[1]
    gn = (cdim - di) // 2
    halo = 8
    chunk = 512
    row = lambda bi, i: (bi, i, 0)
    return pl.pallas_call(
        functools.partial(_ssd_in_body, tm=tm, halo=halo, chunk=chunk),
        grid=(b, s // tm),
        in_specs=[
            pl.BlockSpec((None, tm, d), row),
            _const_spec((1, d)),
            _const_spec(w.shape),
            _const_spec(wdt.shape),
            _const_spec(cw.shape),
            _const_spec(cb.shape),
            _const_spec(dtb.shape),
        ],
        out_specs=[
            pl.BlockSpec((None, tm, di), row),
            pl.BlockSpec((None, tm, di), row),
            pl.BlockSpec((None, tm, gn), row),
            pl.BlockSpec((None, tm, gn), row),
            pl.BlockSpec((None, tm, LANES), row),
        ],
        out_shape=[
            jax.ShapeDtypeStruct((b, s, di), BF16),
            jax.ShapeDtypeStruct((b, s, di), BF16),
            jax.ShapeDtypeStruct((b, s, gn), BF16),
            jax.ShapeDtypeStruct((b, s, gn), BF16),
            jax.ShapeDtypeStruct((b, s, LANES), F32),
        ],
        scratch_shapes=[pltpu.VMEM((cdim // LANES, halo + tm, LANES), F32)],
        compiler_params=_params(("arbitrary", "arbitrary")),
        name="ssd_in",
    )(x, nw, w, wdt, cw, cb, dtb)


def _ssd_scan_body(xs_ref, b_ref, c_ref, dt_ref, alog_ref, dskip_ref, eh_ref, y_ref, st_ref,
                   *, n_heads, nc):
    ci = pl.program_id(1)
    L = SSM_CHUNK
    P, N, G = SSM_HEAD_DIM, SSM_STATE, SSM_GROUPS
    hpg = n_heads // G
    gw = hpg * P

    @pl.when(ci == 0)
    def _():
        st_ref[...] = jnp.zeros_like(st_ref)

    lane_h = lax.broadcasted_iota(jnp.int32, (1, LANES), 1)
    a = jnp.where(lane_h < n_heads, -jnp.exp(alog_ref[...]), 0.0)
    row = lax.broadcasted_iota(jnp.int32, (L, L), 0)
    col = lax.broadcasted_iota(jnp.int32, (L, L), 1)
    tril = row >= col
    tril16 = jnp.where(tril, 1.0, 0.0).astype(BF16)
    eh = eh_ref[...]
    head_of_lane = lax.broadcasted_iota(jnp.int32, (L, gw), 1) // P
    head_mask = [jnp.where(head_of_lane == k, 1.0, 0.0).astype(BF16) for k in range(hpg)]

    def prologue(c):
        rs = slice(c * L, (c + 1) * L)
        dt = dt_ref[rs, :]
        dtT = dt.T
        da = dt * a
        da_hi = da.astype(BF16)
        rest = da - da_hi.astype(F32)
        da_mid = rest.astype(BF16)
        da_lo = (rest - da_mid.astype(F32)).astype(BF16)
        parts = _dot(tril16, jnp.concatenate([da_hi, da_mid, da_lo], axis=1))
        acum = parts[:, :LANES] + parts[:, LANES:2 * LANES] + parts[:, 2 * LANES:]
        acumT = acum.T
        a_last = acum[L - 1:L, :]
        sdt_b = _dot((jnp.exp(a_last - acum) * dt).astype(BF16), eh)
        ea_b = _dot(jnp.exp(acum).astype(BF16), eh)
        dl = jnp.broadcast_to(jnp.exp(a_last), (16, LANES))
        dl_hi = dl.astype(BF16)
        dl_lo = (dl - dl_hi.astype(F32)).astype(BF16)
        decay_last = (_dot(dl_hi, eh) + _dot(dl_lo, eh))[0:1, :]
        return dtT, acum, acumT, sdt_b, ea_b, decay_last

    def main(c, pro):
        dtT, acum, acumT, sdt_b, ea_b, decay_last = pro
        rs = slice(c * L, (c + 1) * L)
        xs16 = xs_ref[rs, :]
        xs = xs16.astype(F32)
        xw = (xs * sdt_b).astype(BF16)
        cbs, y_offs = [], []
        for g in range(G):
            bg = b_ref[rs, g * N:(g + 1) * N]
            cg = c_ref[rs, g * N:(g + 1) * N]
            cbs.append(lax.dot_general(cg, bg, (((1,), (1,)), ((), ())), preferred_element_type=F32))
            gs = slice(g * gw, (g + 1) * gw)
            st = st_ref[g]
            y_offs.append(_dot(cg, st.astype(BF16)) * ea_b[:, gs])
            contrib = lax.dot_general(bg, xw[:, gs], (((0,), (0,)), ((), ())),
                                      preferred_element_type=F32)
            st_ref[g] = st * decay_last[:, gs] + contrib

        for g in range(G):
            gs = slice(g * gw, (g + 1) * gw)
            cb, y_off = cbs[g], y_offs[g]
            ws = []
            for k in range(hpg):
                hidx = g * hpg + k
                seg = acum[:, hidx:hidx + 1] - acumT[hidx:hidx + 1, :]
                decay = jnp.exp(jnp.where(tril, seg, NEG_BIG))
                ws.append((cb * decay * dtT[hidx:hidx + 1, :]).astype(BF16))
            xg = xs16[:, gs]
            xblk = jnp.concatenate([xg * head_mask[k] for k in range(hpg)], axis=0)
            y_diag = _dot(jnp.concatenate(ws, axis=1), xblk)
            y_ref[rs, gs] = (y_diag + y_off + dskip_ref[:, gs] * xs[:, gs]).astype(y_ref.dtype)

    pros = [prologue(c) for c in range(nc)]
    for c in range(nc):
        main(c, pros[c])


def _ssd_scan(xs, bm, cm, dt, alog, dskip, eh, *, n_heads, nc=4):
    b, s, di = xs.shape
    gn = bm.shape[-1]
    L = nc * SSM_CHUNK
    row = lambda bi, i: (bi, i, 0)
    return pl.pallas_call(
        functools.partial(_ssd_scan_body, n_heads=n_heads, nc=nc),
        grid=(b, s // L),
        in_specs=[
            pl.BlockSpec((None, L, di), row),
            pl.BlockSpec((None, L, gn), row),
            pl.BlockSpec((None, L, gn), row),
            pl.BlockSpec((None, L, LANES), row),
            _const_spec(alog.shape),
            _const_spec(dskip.shape),
            _const_spec(eh.shape),
        ],
        out_specs=pl.BlockSpec((None, L, di), row),
        out_shape=jax.ShapeDtypeStruct((b, s, di), BF16),
        scratch_shapes=[pltpu.VMEM((SSM_GROUPS, SSM_STATE, di // SSM_GROUPS), F32)],
        compiler_params=_params(("arbitrary", "arbitrary")),
        name="ssd_scan",
    )(xs, bm, cm, dt, alog, dskip, eh)


def _ssd_out_body(y_ref, z_ref, nw_ref, w_ref, x_ref, o_ref, yn_ref):
    di = y_ref.shape[-1]
    gw = di // SSM_GROUPS
    for g in range(SSM_GROUPS):
        gs = slice(g * gw, (g + 1) * gw)
        yg = y_ref[:, gs].astype(F32) * _silu(z_ref[:, gs].astype(F32))
        yn_ref[:, gs] = _rmsnorm(yg, nw_ref[:, gs]).astype(BF16)
    o_ref[...] = x_ref[...] + _dot(yn_ref[...], w_ref[...])


def _ssd_out(y, z, nw, w, x, *, tm=1024):
    m, di = y.shape
    d = w.shape[1]
    return pl.pallas_call(
        _ssd_out_body,
        grid=(m // tm,),
        in_specs=[
            pl.BlockSpec((tm, di), lambda i: (i, 0)),
            pl.BlockSpec((tm, di), lambda i: (i, 0)),
            _const_spec(nw.shape),
            _const_spec(w.shape),
            pl.BlockSpec((tm, d), lambda i: (i, 0)),
        ],
        out_specs=pl.BlockSpec((tm, d), lambda i: (i, 0)),
        out_shape=jax.ShapeDtypeStruct((m, d), F32),
        scratch_shapes=[pltpu.VMEM((tm, di), BF16)],
        compiler_params=_params(("arbitrary",)),
        name="ssd_out",
    )(y, z, nw, w, x)


def _pad_lanes(a, width=LANES):
    return jnp.pad(a, ((0, 0), (0, width - a.shape[-1])))


def _conv_layer(x, nw, w_in, w_dw, w_out):
    return _conv_mixer(x, nw[None], w_in.astype(BF16), w_dw, w_out.astype(BF16))


def _fox_layer(x, nw, w_in, b_f, q_gain, k_gain, w_out, *, t=256):
    b, s, d = x.shape
    n_heads = b_f.shape[0]
    hd = ATTN_HEAD_DIM
    aw = n_heads * hd
    perm = jnp.argsort(b_f)

    def by_head(w):
        return w.reshape(d, n_heads, hd)[:, perm].reshape(d, aw)

    wqkv = jnp.concatenate([by_head(w_in[:, j * aw:(j + 1) * aw]) for j in range(3)],
                           axis=1).astype(BF16)
    wf = _pad_lanes(w_in[:, 3 * aw:][:, perm]).astype(BF16)
    bf = _pad_lanes(b_f[perm][None])
    w_out = w_out.reshape(n_heads, hd, d)[perm].reshape(aw, d)
    qg = jnp.tile(q_gain, n_heads)[None] * (hd ** -0.5 * LOG2E)
    kg = jnp.tile(k_gain, n_heads)[None]
    smax = 1.02 * LOG2E * hd ** 0.5 * jnp.max(jnp.abs(q_gain)) * jnp.max(jnp.abs(k_gain))
    fast = (2.0 * smax <= FAST_PATH_MAX_LOG2).astype(F32)
    thr = jnp.stack([-(SKIP_LOG2 + 2.0 * smax), fast, smax]).astype(F32)
    head_of = jnp.arange(aw) // hd
    e = (head_of[:, None] == jnp.arange(LANES)[None, :]).astype(F32)
    et = jnp.concatenate([e.T, e.T], axis=0).astype(BF16)
    e = (e / hd).astype(BF16)
    q, k, v, ct = _fox_in(x, nw[None], wqkv, wf, bf, qg, kg, e, et, n_heads=n_heads)
    ck = ct.reshape(b, n_heads // 2, 2, s // t, t).transpose(0, 1, 3, 2, 4)
    cs = ct[:, :, 0::t].reshape(b * n_heads, s // t)
    ce = ct[:, :, t - 1::t].reshape(b * n_heads, s // t)
    attn = _fox_attn(thr, cs, ce, q, k, v, ck, t=t, n_heads=n_heads)
    return _proj_res(attn.reshape(b * s, aw), w_out.astype(BF16), x.reshape(b * s, d)).reshape(b, s, d)


def _ssd_layer(x, nw, w_in, conv_w, conv_b, dt_bias, a_log, d_skip, norm_w, w_out):
    b, s, d = x.shape
    n_heads = a_log.shape[0]
    di = n_heads * SSM_HEAD_DIM
    cdim = conv_w.shape[1]
    wdt = _pad_lanes(w_in[:, di + cdim:]).astype(BF16)
    z, xs, bm, cm, dt = _ssd_in(x, nw[None], w_in.astype(BF16), wdt, conv_w, conv_b[None],
                                _pad_lanes(dt_bias[None]), di=di)
    head_of = jnp.arange(di) // SSM_HEAD_DIM
    eh = (jnp.arange(LANES)[:, None] == head_of[None, :]).astype(BF16)
    dskip = jnp.repeat(d_skip, SSM_HEAD_DIM)[None]
    y = _ssd_scan(xs, bm, cm, dt, _pad_lanes(a_log[None]), dskip, eh, n_heads=n_heads)
    m = b * s
    return _ssd_out(y.reshape(m, di), z.reshape(m, di), norm_w[None], w_out.astype(BF16),
                    x.reshape(m, d)).reshape(b, s, d)


def kernel(x, mix_norm, ffn_norm, ffn_w_gu, ffn_w_down, conv_w_in, conv_w_dw, conv_w_out, fox_w_in, fox_b_f, fox_q_gain, fox_k_gain, fox_w_out, ssd_w_in, ssd_conv_w, ssd_conv_b, ssd_dt_bias, ssd_a_log, ssd_d, ssd_norm_w, ssd_w_out):
    b, s, d = x.shape
    depth = mix_norm.shape[0]
    for i in range(depth):
        kind, j = i % 3, i // 3
        if kind == 0:
            x = _conv_layer(x, mix_norm[i], conv_w_in[j], conv_w_dw[j], conv_w_out[j])
        elif kind == 1:
            x = _fox_layer(x, mix_norm[i], fox_w_in[j], fox_b_f[j], fox_q_gain[j], fox_k_gain[j],
                           fox_w_out[j])
        else:
            x = _ssd_layer(x, mix_norm[i], ssd_w_in[j], ssd_conv_w[j], ssd_conv_b[j],
                           ssd_dt_bias[j], ssd_a_log[j], ssd_d[j], ssd_norm_w[j], ssd_w_out[j])
        x = _ffn(x.reshape(b * s, d), ffn_norm[i][None], ffn_w_gu[i].astype(BF16),
                 ffn_w_down[i].astype(BF16)).reshape(b, s, d)
    return x
```

```python
import functools

import jax
import jax.numpy as jnp
from jax import lax
from jax.experimental import pallas as pl
from jax.experimental.pallas import tpu as pltpu

F32 = jnp.float32
BF16 = jnp.bfloat16

RMS_EPS = 1e-6
ATTN_HEAD_DIM = 64
SSM_HEAD_DIM = 64
SSM_GROUPS = 8
SSM_STATE = 128
SSM_CHUNK = 128
LANES = 128
NEG_BIG = -1e30
LOG2E = 1.4426950408889634
SKIP_LOG2 = 64.0
FAST_PATH_MAX_LOG2 = 100.0

VMEM_LIMIT = 56 * 1024 * 1024


def _params(sem):
    return pltpu.CompilerParams(dimension_semantics=sem, vmem_limit_bytes=VMEM_LIMIT)


def _const_spec(shape):
    nd = len(shape)
    return pl.BlockSpec(shape, lambda *_: (0,) * nd, pipeline_mode=pl.Buffered(1))


def _layer_spec(shape, layer):
    return pl.BlockSpec((None,) + tuple(shape[1:]), lambda *_: (layer, 0, 0),
                        pipeline_mode=pl.Buffered(1))


def _rmsnorm(x, w):
    return x * lax.rsqrt(jnp.mean(x * x, axis=-1, keepdims=True) + RMS_EPS) * w


def _dot(a, b):
    return jnp.dot(a, b, preferred_element_type=F32)


def _softplus(x):
    return jnp.maximum(x, 0.0) + jnp.log1p(jnp.exp(-jnp.abs(x)))


def _silu(x):
    return x * jax.nn.sigmoid(x)


def _lane_cumsum(x):
    n = x.shape[-1]
    lane = lax.broadcasted_iota(jnp.int32, x.shape, x.ndim - 1)
    s = 1
    while s < n:
        x = x + jnp.where(lane >= s, pltpu.roll(x, s, x.ndim - 1), 0.0)
        s *= 2
    return x


def _ffn_body(x_ref, nw_ref, wgu_ref, wd_ref, o_ref, a_ref, *, chunk):
    x = x_ref[...]
    h = _rmsnorm(x, nw_ref[...]).astype(BF16)
    dff = wd_ref.shape[0]
    for c0 in range(0, dff, chunk):
        g = _dot(h, wgu_ref[:, c0:c0 + chunk])
        u = _dot(h, wgu_ref[:, dff + c0:dff + c0 + chunk])
        a_ref[:, c0:c0 + chunk] = (_silu(g) * u).astype(BF16)
    o_ref[...] = x + _dot(a_ref[...], wd_ref[...])


def _ffn(x, nw, wgu, wd, *, layer, tm=1024, chunk=256):
    m, d = x.shape
    dff = wd.shape[1]
    return pl.pallas_call(
        functools.partial(_ffn_body, chunk=chunk),
        grid=(m // tm,),
        in_specs=[
            pl.BlockSpec((tm, d), lambda i: (i, 0)),
            _const_spec((1, d)),
            _layer_spec(wgu.shape, layer),
            _layer_spec(wd.shape, layer),
        ],
        out_specs=pl.BlockSpec((tm, d), lambda i: (i, 0)),
        out_shape=jax.ShapeDtypeStruct((m, d), F32),
        scratch_shapes=[pltpu.VMEM((tm, dff), BF16)],
        compiler_params=_params(("arbitrary",)),
        name="ffn",
    )(x, nw, wgu, wd)


def _conv_mixer_body(x_ref, nw_ref, win_ref, wdw_ref, wout_ref, o_ref, ext_ref, y_ref,
                     *, tm, halo, chunk):
    i = pl.program_id(1)
    d = x_ref.shape[-1]
    x = x_ref[...]
    h = _rmsnorm(x, nw_ref[...]).astype(BF16)

    @pl.when(i == 0)
    def _():
        ext_ref[:, 0:halo, :] = jnp.zeros((d // LANES, halo, LANES), F32)

    @pl.when(i > 0)
    def _():
        ext_ref[:, 0:halo, :] = ext_ref[:, tm:tm + halo, :]

    kw = wdw_ref.shape[0]
    for q0 in range(0, d, chunk):
        bq = _dot(h, win_ref[:, q0:q0 + chunk])
        cq = _dot(h, win_ref[:, d + q0:d + q0 + chunk])
        vq = _dot(h, win_ref[:, 2 * d + q0:2 * d + q0 + chunk])
        for r0 in range(0, chunk, LANES):
            c0 = q0 + r0
            sl = c0 // LANES
            cs = slice(c0, c0 + LANES)
            rs = slice(r0, r0 + LANES)
            cv = cq[:, rs] * vq[:, rs]
            ext_ref[sl, halo:halo + tm, :] = cv
            u = cv * wdw_ref[kw - 1:kw, cs]
            for k in range(kw - 1):
                off = halo - (kw - 1) + k
                u = u + ext_ref[sl, off:off + tm, :] * wdw_ref[k:k + 1, cs]
            y_ref[:, cs] = (bq[:, rs] * u).astype(BF16)
    o_ref[...] = x + _dot(y_ref[...], wout_ref[...])


def _conv_mixer(x, nw, win, wdw, wout, *, layer, tm=1024):
    b, s, d = x.shape
    halo = 8
    return pl.pallas_call(
        functools.partial(_conv_mixer_body, tm=tm, halo=halo, chunk=256),
        grid=(b, s // tm),
        in_specs=[
            pl.BlockSpec((None, tm, d), lambda bi, i: (bi, i, 0)),
            _const_spec((1, d)),
            _layer_spec(win.shape, layer),
            _const_spec(wdw.shape),
            _layer_spec(wout.shape, layer),
        ],
        out_specs=pl.BlockSpec((None, tm, d), lambda bi, i: (bi, i, 0)),
        out_shape=jax.ShapeDtypeStruct((b, s, d), F32),
        scratch_shapes=[pltpu.VMEM((d // LANES, halo + tm, LANES), F32),
                        pltpu.VMEM((tm, d), BF16)],
        compiler_params=_params(("arbitrary", "arbitrary")),
        name="conv_mixer",
    )(x, nw, win, wdw, wout)


def _fox_in_body(x_ref, nw_ref, wqkv_ref, wf_ref, bf_ref, qg_ref, kg_ref, e_ref, et_ref,
                 q_ref, k_ref, v_ref, ct_ref, carry_ref, *, n_heads):
    i = pl.program_id(1)
    aw = q_ref.shape[-1]
    x = x_ref[...]
    h = _rmsnorm(x, nw_ref[...]).astype(BF16)
    qkv = _dot(h, wqkv_ref[...])

    def head_norm(t, gain):
        ms = _dot((t * t).astype(BF16), e_ref[...])
        r = lax.rsqrt(ms + RMS_EPS)
        r_hi = r.astype(BF16)
        r_lo = (r - r_hi.astype(F32)).astype(BF16)
        rb = _dot(jnp.concatenate([r_hi, r_lo], axis=1), et_ref[...])
        return t * rb * gain

    q_ref[...] = head_norm(qkv[:, :aw], qg_ref[...]).astype(BF16)
    k_ref[...] = head_norm(qkv[:, aw:2 * aw], kg_ref[...]).astype(BF16)
    v_ref[...] = qkv[:, 2 * aw:].astype(BF16)

    fl = _dot(h, wf_ref[...]) + bf_ref[...]
    logf = -_softplus(-fl) * LOG2E
    local = _lane_cumsum(logf.T[0:n_heads, :])

    @pl.when(i == 0)
    def _():
        carry_ref[...] = jnp.zeros_like(carry_ref)

    cum = local + carry_ref[:, 0:1]
    ct_ref[...] = cum
    tm = cum.shape[-1]
    carry_ref[...] = jnp.broadcast_to(cum[:, tm - 1:tm], carry_ref.shape)


def _fox_in(x, nw, wqkv, wf, bf, qg, kg, e, et, *, n_heads, tm=1024):
    b, s, d = x.shape
    aw = wqkv.shape[1] // 3
    row = lambda bi, i: (bi, i, 0)
    return pl.pallas_call(
        functools.partial(_fox_in_body, n_heads=n_heads),
        grid=(b, s // tm),
        in_specs=[
            pl.BlockSpec((None, tm, d), row),
            _const_spec((1, d)),
            _const_spec(wqkv.shape),
            _const_spec(wf.shape),
            _const_spec(bf.shape),
            _const_spec(qg.shape),
            _const_spec(kg.shape),
            _const_spec(e.shape),
            _const_spec(et.shape),
        ],
        out_specs=[
            pl.BlockSpec((None, tm, aw), row),
            pl.BlockSpec((None, tm, aw), row),
            pl.BlockSpec((None, tm, aw), row),
            pl.BlockSpec((None, n_heads, tm), lambda bi, i: (bi, 0, i)),
        ],
        out_shape=[
            jax.ShapeDtypeStruct((b, s, aw), BF16),
            jax.ShapeDtypeStruct((b, s, aw), BF16),
            jax.ShapeDtypeStruct((b, s, aw), BF16),
            jax.ShapeDtypeStruct((b, n_heads, s), F32),
        ],
        scratch_shapes=[pltpu.VMEM((n_heads, LANES), F32)],
        compiler_params=_params(("arbitrary", "arbitrary")),
        name="fox_in",
    )(x, nw, wqkv, wf, bf, qg, kg, e, et)


def _fox_attn_body(thr_ref, cs_ref, ce_ref, q_ref, k_ref, v_ref, ck_ref, o_ref, p_ref, acc_ref,
                   *, t, n_heads, nsub):
    bi, pr, step = pl.program_id(0), pl.program_id(1), pl.program_id(2)
    hd = ATTN_HEAD_DIM
    lane = lax.broadcasted_iota(jnp.int32, (t, 2 * hd), 1)
    lo = lane < hd
    r0 = bi * n_heads + 2 * pr
    thr = thr_ref[0]
    tiles = [nsub * step + sub for sub in range(nsub)]

    def first_block(i):
        cs0, cs1 = cs_ref[r0, i], cs_ref[r0 + 1, i]

        def needed(j):
            jj = jnp.maximum(j, 0)
            near = (cs0 - ce_ref[r0, jj] >= thr) | (cs1 - ce_ref[r0 + 1, jj] >= thr)
            return (j >= 0) & near

        return lax.while_loop(needed, lambda j: j - 1, i - 1) + 1

    j0s = [first_block(i) for i in tiles]

    def split_heads(sub):
        q2 = q_ref[sub * t:(sub + 1) * t, :]
        zero = jnp.zeros_like(q2)
        return jnp.where(lo, q2, zero), jnp.where(lo, zero, q2)

    def scores(qs, j):
        k2 = k_ref[pl.ds(pl.multiple_of(j * t, t), t), :]
        ck = ck_ref[j]
        return tuple(lax.dot_general(qs[hh], k2, (((1,), (1,)), ((), ())),
                                     preferred_element_type=F32) - ck[hh:hh + 1, :]
                     for hh in range(2))

    def causal(s):
        row = lax.broadcasted_iota(jnp.int32, (t, t), 0)
        col = lax.broadcasted_iota(jnp.int32, (t, t), 1)
        return tuple(jnp.where(row >= col, sh, NEG_BIG) for sh in s)

    @pl.when(thr_ref[1] <= 0.0)
    def _():
        for sub in range(nsub):
            i, j0, qs = tiles[sub], j0s[sub], split_heads(sub)

            def update(j, s, ms, ls, acc):
                v2 = v_ref[pl.ds(pl.multiple_of(j * t, t), t), :]
                new_m, new_l, alphas, pvs = [], [], [], []
                for hh in range(2):
                    m_new = jnp.maximum(ms[hh], jnp.max(s[hh], axis=1, keepdims=True))
                    alpha = jnp.exp2(ms[hh] - m_new)
                    p = jnp.exp2(s[hh] - m_new)
                    new_l.append(alpha * ls[hh] + jnp.sum(p, axis=1, keepdims=True))
                    new_m.append(m_new)
                    alphas.append(alpha)
                    pvs.append(_dot(p.astype(BF16), v2))
                acc = acc * jnp.where(lo, alphas[0], alphas[1]) + jnp.where(lo, pvs[0], pvs[1])
                return tuple(new_m), tuple(new_l), acc

            def body(j, carry, qs=qs, update=update):
                s, ms, ls, acc = carry
                s_next = scores(qs, j + 1)
                ms, ls, acc = update(j, s, ms, ls, acc)
                return s_next, ms, ls, acc

            m0 = jnp.full((t, 1), NEG_BIG, F32)
            l0 = jnp.zeros((t, 1), F32)
            carry = (scores(qs, j0), (m0, m0), (l0, l0), jnp.zeros((t, 2 * hd), F32))
            s, ms, ls, acc = lax.fori_loop(j0, i, body, carry)
            _, ls, acc = update(i, causal(s), ms, ls, acc)
            o_ref[sub * t:(sub + 1) * t, :] = (acc / jnp.where(lo, ls[0], ls[1])).astype(o_ref.dtype)

    @pl.when(thr_ref[1] > 0.0)
    def _():
        e_lo = jnp.where(lo, 1.0, 0.0).astype(BF16)
        e_hi = jnp.where(lo, 0.0, 1.0).astype(BF16)

        def pv(j, p16):
            v2 = v_ref[pl.ds(pl.multiple_of(j * t, t), t), :]
            vb = jnp.concatenate(
                [jnp.concatenate([v2 * e_lo, e_lo], axis=1),
                 jnp.concatenate([v2 * e_hi, e_hi], axis=1)], axis=0)
            return _dot(p16, vb)

        def weights(s, offs):
            return jnp.concatenate([jnp.exp2(s[hh] - offs[hh]).astype(BF16) for hh in range(2)], axis=1)

        qss, offss = [], []
        for sub in range(nsub):
            i, qs = tiles[sub], split_heads(sub)
            ck_i = ck_ref[i]
            offs = tuple(thr_ref[2] - jnp.broadcast_to(ck_i[hh:hh + 1, :], (LANES, t)).T[:, 0:1]
                         for hh in range(2))
            p_ref[sub, 0] = weights(causal(scores(qs, i)), offs)
            acc_ref[sub] = jnp.zeros((t, 4 * hd), F32)
            qss.append(qs)
            offss.append(offs)

        jprevs = []
        for sub in range(nsub):
            i, j0 = tiles[sub], j0s[sub]

            def body(j, jprev, sub=sub, j0=j0):
                slot = (j - j0) & 1
                p_new = weights(scores(qss[sub], j), offss[sub])
                acc_ref[sub] += pv(jprev, p_ref[sub, slot])
                p_ref[sub, 1 - slot] = p_new
                return j

            jprevs.append(lax.fori_loop(j0, i, body, i))

        for sub in range(nsub):
            i, j0 = tiles[sub], j0s[sub]
            acc = acc_ref[sub] + pv(jprevs[sub], p_ref[sub, (i - j0) & 1])
            o_ref[sub * t:(sub + 1) * t, :] = (acc[:, :2 * hd] / acc[:, 2 * hd:]).astype(o_ref.dtype)


def _fox_attn(thr, cs, ce, q, k, v, ck, *, t, n_heads, nsub=4):
    b, s, aw = q.shape
    pairs = aw // LANES
    smem = pl.BlockSpec(memory_space=pltpu.SMEM)
    return pl.pallas_call(
        functools.partial(_fox_attn_body, t=t, n_heads=n_heads, nsub=nsub),
        grid=(b, pairs, s // (nsub * t)),
        in_specs=[
            smem, smem, smem,
            pl.BlockSpec((None, nsub * t, LANES), lambda bi, p, i: (bi, i, p)),
            pl.BlockSpec((None, s, LANES), lambda bi, p, i: (bi, 0, p)),
            pl.BlockSpec((None, s, LANES), lambda bi, p, i: (bi, 0, p)),
            pl.BlockSpec((None, None, s // t, 2, t), lambda bi, p, i: (bi, p, 0, 0, 0)),
        ],
        out_specs=pl.BlockSpec((None, nsub * t, LANES), lambda bi, p, i: (bi, i, p)),
        out_shape=jax.ShapeDtypeStruct((b, s, aw), BF16),
        scratch_shapes=[pltpu.VMEM((nsub, 2, t, 2 * t), BF16), pltpu.VMEM((nsub, t, 2 * LANES), F32)],
        compiler_params=_params(("arbitrary", "arbitrary", "arbitrary")),
        name="fox_attn",
    )(thr, cs, ce, q, k, v, ck)


def _proj_res_body(y_ref, w_ref, x_ref, o_ref):
    o_ref[...] = x_ref[...] + _dot(y_ref[...], w_ref[...])


def _proj_res(y, w, x, *, tm=1024):
    m, kdim = y.shape
    d = w.shape[1]
    return pl.pallas_call(
        _proj_res_body,
        grid=(m // tm,),
        in_specs=[
            pl.BlockSpec((tm, kdim), lambda i: (i, 0)),
            _const_spec(w.shape),
            pl.BlockSpec((tm, d), lambda i: (i, 0)),
        ],
        out_specs=pl.BlockSpec((tm, d), lambda i: (i, 0)),
        out_shape=jax.ShapeDtypeStruct((m, d), F32),
        compiler_params=_params(("arbitrary",)),
        name="proj_res",
    )(y, w, x)


def _ssd_in_body(x_ref, nw_ref, w_ref, wdt_ref, cw_ref, cb_ref, dtb_ref,
                 z_ref, xs_ref, b_ref, c_ref, dt_ref, ext_ref, *, tm, halo, chunk):
    i = pl.program_id(1)
    di = xs_ref.shape[-1]
    gn = b_ref.shape[-1]
    cdim = cw_ref.shape[-1]
    x = x_ref[...]
    h = _rmsnorm(x, nw_ref[...]).astype(BF16)

    @pl.when(i == 0)
    def _():
        ext_ref[:, 0:halo, :] = jnp.zeros((cdim // LANES, halo, LANES), F32)

    @pl.when(i > 0)
    def _():
        ext_ref[:, 0:halo, :] = ext_ref[:, tm:tm + halo, :]

    dt_ref[...] = _softplus(_dot(h, wdt_ref[...]) + dtb_ref[...])
    kw = cw_ref.shape[0]
    outs = ((xs_ref, 0, di), (b_ref, di, di + gn), (c_ref, di + gn, cdim))
    for o_ref, lo_c, hi_c in outs:
        for q0 in range(lo_c, hi_c, chunk):
            raw = _dot(h, w_ref[:, di + q0:di + q0 + chunk])
            if q0 < di:
                z_ref[:, q0:q0 + chunk] = _dot(h, w_ref[:, q0:q0 + chunk]).astype(z_ref.dtype)
            for r0 in range(0, chunk, LANES):
                c0 = q0 + r0
                sl = c0 // LANES
                cs = slice(c0, c0 + LANES)
                cur = raw[:, r0:r0 + LANES]
                ext_ref[sl, halo:halo + tm, :] = cur
                u = cur * cw_ref[kw - 1:kw, cs] + cb_ref[:, cs]
                for k in range(kw - 1):
                    off = halo - (kw - 1) + k
                    u = u + ext_ref[sl, off:off + tm, :] * cw_ref[k:k + 1, cs]
                o_ref[:, c0 - lo_c:c0 - lo_c + LANES] = _silu(u).astype(o_ref.dtype)


def _ssd_in(x, nw, w, wdt, cw, cb, dtb, *, di, tm=512):
    b, s, d = x.shape
    cdim = cw.shape[1]
    gn = (cdim - di) // 2
    halo = 8
    chunk = 512
    row = lambda bi, i: (bi, i, 0)
    return pl.pallas_call(
        functools.partial(_ssd_in_body, tm=tm, halo=halo, chunk=chunk),
        grid=(b, s // tm),
        in_specs=[
            pl.BlockSpec((None, tm, d), row),
            _const_spec((1, d)),
            _const_spec(w.shape),
            _const_spec(wdt.shape),
            _const_spec(cw.shape),
            _const_spec(cb.shape),
            _const_spec(dtb.shape),
        ],
        out_specs=[
            pl.BlockSpec((None, tm, di), row),
            pl.BlockSpec((None, tm, di), row),
            pl.BlockSpec((None, tm, gn), row),
            pl.BlockSpec((None, tm, gn), row),
            pl.BlockSpec((None, tm, LANES), row),
        ],
        out_shape=[
            jax.ShapeDtypeStruct((b, s, di), BF16),
            jax.ShapeDtypeStruct((b, s, di), BF16),
            jax.ShapeDtypeStruct((b, s, gn), BF16),
            jax.ShapeDtypeStruct((b, s, gn), BF16),
            jax.ShapeDtypeStruct((b, s, LANES), F32),
        ],
        scratch_shapes=[pltpu.VMEM((cdim // LANES, halo + tm, LANES), F32)],
        compiler_params=_params(("arbitrary", "arbitrary")),
        name="ssd_in",
    )(x, nw, w, wdt, cw, cb, dtb)


def _ssd_scan_body(xs_ref, b_ref, c_ref, dt_ref, alog_ref, dskip_ref, eh_ref, y_ref, st_ref,
                   *, n_heads, nc):
    ci = pl.program_id(1)
    L = SSM_CHUNK
    P, N, G = SSM_HEAD_DIM, SSM_STATE, SSM_GROUPS
    hpg = n_heads // G
    gw = hpg * P

    @pl.when(ci == 0)
    def _():
        st_ref[...] = jnp.zeros_like(st_ref)

    lane_h = lax.broadcasted_iota(jnp.int32, (1, LANES), 1)
    a = jnp.where(lane_h < n_heads, -jnp.exp(alog_ref[...]), 0.0)
    row = lax.broadcasted_iota(jnp.int32, (L, L), 0)
    col = lax.broadcasted_iota(jnp.int32, (L, L), 1)
    tril = row >= col
    tril16 = jnp.where(tril, 1.0, 0.0).astype(BF16)
    eh = eh_ref[...]
    head_of_lane = lax.broadcasted_iota(jnp.int32, (L, gw), 1) // P
    head_mask = [jnp.where(head_of_lane == k, 1.0, 0.0).astype(BF16) for k in range(hpg)]

    def prologue(c):
        rs = slice(c * L, (c + 1) * L)
        dt = dt_ref[rs, :]
        dtT = dt.T
        da = dt * a
        da_hi = da.astype(BF16)
        rest = da - da_hi.astype(F32)
        da_mid = rest.astype(BF16)
        da_lo = (rest - da_mid.astype(F32)).astype(BF16)
        parts = _dot(tril16, jnp.concatenate([da_hi, da_mid, da_lo], axis=1))
        acum = parts[:, :LANES] + parts[:, LANES:2 * LANES] + parts[:, 2 * LANES:]
        acumT = acum.T
        a_last = acum[L - 1:L, :]
        sdt_b = _dot((jnp.exp(a_last - acum) * dt).astype(BF16), eh)
        ea_b = _dot(jnp.exp(acum).astype(BF16), eh)
        dl = jnp.broadcast_to(jnp.exp(a_last), (16, LANES))
        dl_hi = dl.astype(BF16)
        dl_lo = (dl - dl_hi.astype(F32)).astype(BF16)
        decay_last = (_dot(dl_hi, eh) + _dot(dl_lo, eh))[0:1, :]
        return dtT, acum, acumT, sdt_b, ea_b, decay_last

    def main(c, pro):
        dtT, acum, acumT, sdt_b, ea_b, decay_last = pro
        rs = slice(c * L, (c + 1) * L)
        xs16 = xs_ref[rs, :]
        xs = xs16.astype(F32)
        xw = (xs * sdt_b).astype(BF16)
        cbs, y_offs = [], []
        for g in range(G):
            bg = b_ref[rs, g * N:(g + 1) * N]
            cg = c_ref[rs, g * N:(g + 1) * N]
            cbs.append(lax.dot_general(cg, bg, (((1,), (1,)), ((), ())), preferred_element_type=F32))
            gs = slice(g * gw, (g + 1) * gw)
            st = st_ref[g]
            y_offs.append(_dot(cg, st.astype(BF16)) * ea_b[:, gs])
            contrib = lax.dot_general(bg, xw[:, gs], (((0,), (0,)), ((), ())),
                                      preferred_element_type=F32)
            st_ref[g] = st * decay_last[:, gs] + contrib

        for g in range(G):
            gs = slice(g * gw, (g + 1) * gw)
            cb, y_off = cbs[g], y_offs[g]
            ws = []
            for k in range(hpg):
                hidx = g * hpg + k
                seg = acum[:, hidx:hidx + 1] - acumT[hidx:hidx + 1, :]
                decay = jnp.exp(jnp.where(tril, seg, NEG_BIG))
                ws.append((cb * decay * dtT[hidx:hidx + 1, :]).astype(BF16))
            xg = xs16[:, gs]
            xblk = jnp.concatenate([xg * head_mask[k] for k in range(hpg)], axis=0)
            y_diag = _dot(jnp.concatenate(ws, axis=1), xblk)
            y_ref[rs, gs] = (y_diag + y_off + dskip_ref[:, gs] * xs[:, gs]).astype(y_ref.dtype)

    pros = [prologue(c) for c in range(nc)]
    for c in range(nc):
        main(c, pros[c])


def _ssd_scan(xs, bm, cm, dt, alog, dskip, eh, *, n_heads, nc=4):
    b, s, di = xs.shape
    gn = bm.shape[-1]
    L = nc * SSM_CHUNK
    row = lambda bi, i: (bi, i, 0)
    return pl.pallas_call(
        functools.partial(_ssd_scan_body, n_heads=n_heads, nc=nc),
        grid=(b, s // L),
        in_specs=[
            pl.BlockSpec((None, L, di), row),
            pl.BlockSpec((None, L, gn), row),
            pl.BlockSpec((None, L, gn), row),
            pl.BlockSpec((None, L, LANES), row),
            _const_spec(alog.shape),
            _const_spec(dskip.shape),
            _const_spec(eh.shape),
        ],
        out_specs=pl.BlockSpec((None, L, di), row),
        out_shape=jax.ShapeDtypeStruct((b, s, di), BF16),
        scratch_shapes=[pltpu.VMEM((SSM_GROUPS, SSM_STATE, di // SSM_GROUPS), F32)],
        compiler_params=_params(("arbitrary", "arbitrary")),
        name="ssd_scan",
    )(xs, bm, cm, dt, alog, dskip, eh)


def _ssd_out_body(y_ref, z_ref, nw_ref, w_ref, x_ref, o_ref, yn_ref):
    di = y_ref.shape[-1]
    gw = di // SSM_GROUPS
    for g in range(SSM_GROUPS):
        gs = slice(g * gw, (g + 1) * gw)
        yg = y_ref[:, gs].astype(F32) * _silu(z_ref[:, gs].astype(F32))
        yn_ref[:, gs] = _rmsnorm(yg, nw_ref[:, gs]).astype(BF16)
    o_ref[...] = x_ref[...] + _dot(yn_ref[...], w_ref[...])


def _ssd_out(y, z, nw, w, x, *, tm=1024):
    m, di = y.shape
    d = w.shape[1]
    return pl.pallas_call(
        _ssd_out_body,
        grid=(m // tm,),
        in_specs=[
            pl.BlockSpec((tm, di), lambda i: (i, 0)),
            pl.BlockSpec((tm, di), lambda i: (i, 0)),
            _const_spec(nw.shape),
            _const_spec(w.shape),
            pl.BlockSpec((tm, d), lambda i: (i, 0)),
        ],
        out_specs=pl.BlockSpec((tm, d), lambda i: (i, 0)),
        out_shape=jax.ShapeDtypeStruct((m, d), F32),
        scratch_shapes=[pltpu.VMEM((tm, di), BF16)],
        compiler_params=_params(("arbitrary",)),
        name="ssd_out",
    )(y, z, nw, w, x)


def _pad_lanes(a, width=LANES):
    return jnp.pad(a, ((0, 0), (0, width - a.shape[-1])))


def _conv_layer(x, nw, w_in, w_dw, w_out, *, layer):
    return _conv_mixer(x, nw[None], w_in, w_dw, w_out, layer=layer)


def _fox_layer(x, nw, w_in, b_f, q_gain, k_gain, w_out, *, t=256):
    b, s, d = x.shape
    n_heads = b_f.shape[0]
    hd = ATTN_HEAD_DIM
    aw = n_heads * hd
    perm = jnp.argsort(b_f)

    def by_head(w):
        return w.reshape(d, n_heads, hd)[:, perm].reshape(d, aw)

    wqkv = jnp.concatenate([by_head(w_in[:, j * aw:(j + 1) * aw]) for j in range(3)],
                           axis=1).astype(BF16)
    wf = _pad_lanes(w_in[:, 3 * aw:][:, perm]).astype(BF16)
    bf = _pad_lanes(b_f[perm][None])
    w_out = w_out.reshape(n_heads, hd, d)[perm].reshape(aw, d)
    qg = jnp.tile(q_gain, n_heads)[None] * (hd ** -0.5 * LOG2E)
    kg = jnp.tile(k_gain, n_heads)[None]
    smax = 1.02 * LOG2E * hd ** 0.5 * jnp.max(jnp.abs(q_gain)) * jnp.max(jnp.abs(k_gain))
    fast = (2.0 * smax <= FAST_PATH_MAX_LOG2).astype(F32)
    thr = jnp.stack([-(SKIP_LOG2 + 2.0 * smax), fast, smax]).astype(F32)
    head_of = jnp.arange(aw) // hd
    e = (head_of[:, None] == jnp.arange(LANES)[None, :]).astype(F32)
    et = jnp.concatenate([e.T, e.T], axis=0).astype(BF16)
    e = (e / hd).astype(BF16)
    q, k, v, ct = _fox_in(x, nw[None], wqkv, wf, bf, qg, kg, e, et, n_heads=n_heads)
    ck = ct.reshape(b, n_heads // 2, 2, s // t, t).transpose(0, 1, 3, 2, 4)
    cs = ct[:, :, 0::t].reshape(b * n_heads, s // t)
    ce = ct[:, :, t - 1::t].reshape(b * n_heads, s // t)
    attn = _fox_attn(thr, cs, ce, q, k, v, ck, t=t, n_heads=n_heads)
    return _proj_res(attn.reshape(b * s, aw), w_out.astype(BF16), x.reshape(b * s, d)).reshape(b, s, d)


def _ssd_layer(x, nw, w_in, conv_w, conv_b, dt_bias, a_log, d_skip, norm_w, w_out):
    b, s, d = x.shape
    n_heads = a_log.shape[0]
    di = n_heads * SSM_HEAD_DIM
    cdim = conv_w.shape[1]
    wdt = _pad_lanes(w_in[:, di + cdim:]).astype(BF16)
    z, xs, bm, cm, dt = _ssd_in(x, nw[None], w_in.astype(BF16), wdt, conv_w, conv_b[None],
                                _pad_lanes(dt_bias[None]), di=di)
    head_of = jnp.arange(di) // SSM_HEAD_DIM
    eh = (jnp.arange(LANES)[:, None] == head_of[None, :]).astype(BF16)
    dskip = jnp.repeat(d_skip, SSM_HEAD_DIM)[None]
    y = _ssd_scan(xs, bm, cm, dt, _pad_lanes(a_log[None]), dskip, eh, n_heads=n_heads)
    m = b * s
    return _ssd_out(y.reshape(m, di), z.reshape(m, di), norm_w[None], w_out.astype(BF16),
                    x.reshape(m, d)).reshape(b, s, d)


def kernel(x, mix_norm, ffn_norm, ffn_w_gu, ffn_w_down, conv_w_in, conv_w_dw, conv_w_out, fox_w_in, fox_b_f, fox_q_gain, fox_k_gain, fox_w_out, ssd_w_in, ssd_conv_w, ssd_conv_b, ssd_dt_bias, ssd_a_log, ssd_d, ssd_norm_w, ssd_w_out):
    b, s, d = x.shape
    depth = mix_norm.shape[0]
    ffn_wgu16, ffn_wd16 = ffn_w_gu.astype(BF16), ffn_w_down.astype(BF16)
    conv_win16, conv_wout16 = conv_w_in.astype(BF16), conv_w_out.astype(BF16)
    for i in range(depth):
        kind, j = i % 3, i // 3
        if kind == 0:
            x = _conv_layer(x, mix_norm[i], conv_win16, conv_w_dw[j], conv_wout16, layer=j)
        elif kind == 1:
            x = _fox_layer(x, mix_norm[i], fox_w_in[j], fox_b_f[j], fox_q_gain[j], fox_k_gain[j],
                           fox_w_out[j])
        else:
            x = _ssd_layer(x, mix_norm[i], ssd_w_in[j], ssd_conv_w[j], ssd_conv_b[j],
                           ssd_dt_bias[j], ssd_a_log[j], ssd_d[j], ssd_norm_w[j], ssd_w_out[j])
        x = _ffn(x.reshape(b * s, d), ffn_norm[i][None], ffn_wgu16, ffn_wd16,
                 layer=i).reshape(b, s, d)
    return x
```

```python
import functools

import jax
import jax.numpy as jnp
from jax import lax
from jax.experimental import pallas as pl
from jax.experimental.pallas import tpu as pltpu

F32 = jnp.float32
BF16 = jnp.bfloat16

RMS_EPS = 1e-6
ATTN_HEAD_DIM = 64
SSM_HEAD_DIM = 64
SSM_GROUPS = 8
SSM_STATE = 128
SSM_CHUNK = 128
LANES = 128
NEG_BIG = -1e30
LOG2E = 1.4426950408889634
SKIP_LOG2 = 64.0
FAST_PATH_MAX_LOG2 = 100.0

VMEM_LIMIT = 56 * 1024 * 1024


def _params(sem):
    return pltpu.CompilerParams(dimension_semantics=sem, vmem_limit_bytes=VMEM_LIMIT)


def _const_spec(shape):
    nd = len(shape)
    return pl.BlockSpec(shape, lambda *_: (0,) * nd, pipeline_mode=pl.Buffered(1))


def _layer_spec(shape, layer):
    return pl.BlockSpec((None,) + tuple(shape[1:]), lambda *_: (layer, 0, 0),
                        pipeline_mode=pl.Buffered(1))


def _rmsnorm(x, w):
    return x * lax.rsqrt(jnp.mean(x * x, axis=-1, keepdims=True) + RMS_EPS) * w


def _dot(a, b):
    return jnp.dot(a, b, preferred_element_type=F32)


def _softplus(x):
    return jnp.maximum(x, 0.0) + jnp.log1p(jnp.exp(-jnp.abs(x)))


def _silu(x):
    return x * jax.nn.sigmoid(x)


def _lane_cumsum(x):
    n = x.shape[-1]
    lane = lax.broadcasted_iota(jnp.int32, x.shape, x.ndim - 1)
    s = 1
    while s < n:
        x = x + jnp.where(lane >= s, pltpu.roll(x, s, x.ndim - 1), 0.0)
        s *= 2
    return x


def _ffn_body(*refs, chunk, with_proj):
    if with_proj:
        y_ref, wo_ref, x_ref, nw_ref, wgu_ref, wd_ref, o_ref, a_ref = refs
        x = x_ref[...] + _dot(y_ref[...], wo_ref[...])
    else:
        x_ref, nw_ref, wgu_ref, wd_ref, o_ref, a_ref = refs
        x = x_ref[...]
    h = _rmsnorm(x, nw_ref[...]).astype(BF16)
    dff = wd_ref.shape[0]
    for c0 in range(0, dff, chunk):
        g = _dot(h, wgu_ref[:, c0:c0 + chunk])
        u = _dot(h, wgu_ref[:, dff + c0:dff + c0 + chunk])
        a_ref[:, c0:c0 + chunk] = (_silu(g) * u).astype(BF16)
    o_ref[...] = x + _dot(a_ref[...], wd_ref[...])


def _ffn(x, nw, wgu, wd, *, layer, proj=None, tm=1024, chunk=256):
    m, d = x.shape
    dff = wd.shape[1]
    rows = lambda i: (i, 0)
    args, specs = [], []
    if proj is not None:
        y, w_o = proj
        args += [y, w_o]
        specs += [pl.BlockSpec((tm, y.shape[1]), rows), _const_spec(w_o.shape)]
    args += [x, nw, wgu, wd]
    specs += [pl.BlockSpec((tm, d), rows), _const_spec((1, d)), _layer_spec(wgu.shape, layer),
              _layer_spec(wd.shape, layer)]
    return pl.pallas_call(
        functools.partial(_ffn_body, chunk=chunk, with_proj=proj is not None),
        grid=(m // tm,),
        in_specs=specs,
        out_specs=pl.BlockSpec((tm, d), rows),
        out_shape=jax.ShapeDtypeStruct((m, d), F32),
        scratch_shapes=[pltpu.VMEM((tm, dff), BF16)],
        compiler_params=_params(("arbitrary",)),
        name="ffn",
    )(*args)


def _conv_mixer_body(x_ref, nw_ref, win_ref, wdw_ref, wout_ref, o_ref, ext_ref, y_ref,
                     *, tm, halo, chunk):
    i = pl.program_id(1)
    d = x_ref.shape[-1]
    x = x_ref[...]
    h = _rmsnorm(x, nw_ref[...]).astype(BF16)

    @pl.when(i == 0)
    def _():
        ext_ref[:, 0:halo, :] = jnp.zeros((d // LANES, halo, LANES), F32)

    @pl.when(i > 0)
    def _():
        ext_ref[:, 0:halo, :] = ext_ref[:, tm:tm + halo, :]

    kw = wdw_ref.shape[0]
    for q0 in range(0, d, chunk):
        bq = _dot(h, win_ref[:, q0:q0 + chunk])
        cq = _dot(h, win_ref[:, d + q0:d + q0 + chunk])
        vq = _dot(h, win_ref[:, 2 * d + q0:2 * d + q0 + chunk])
        for r0 in range(0, chunk, LANES):
            c0 = q0 + r0
            sl = c0 // LANES
            cs = slice(c0, c0 + LANES)
            rs = slice(r0, r0 + LANES)
            cv = cq[:, rs] * vq[:, rs]
            ext_ref[sl, halo:halo + tm, :] = cv
            u = cv * wdw_ref[kw - 1:kw, cs]
            for k in range(kw - 1):
                off = halo - (kw - 1) + k
                u = u + ext_ref[sl, off:off + tm, :] * wdw_ref[k:k + 1, cs]
            y_ref[:, cs] = (bq[:, rs] * u).astype(BF16)
    o_ref[...] = x + _dot(y_ref[...], wout_ref[...])


def _conv_mixer(x, nw, win, wdw, wout, *, layer, tm=1024):
    b, s, d = x.shape
    halo = 8
    return pl.pallas_call(
        functools.partial(_conv_mixer_body, tm=tm, halo=halo, chunk=256),
        grid=(b, s // tm),
        in_specs=[
            pl.BlockSpec((None, tm, d), lambda bi, i: (bi, i, 0)),
            _const_spec((1, d)),
            _layer_spec(win.shape, layer),
            _const_spec(wdw.shape),
            _layer_spec(wout.shape, layer),
        ],
        out_specs=pl.BlockSpec((None, tm, d), lambda bi, i: (bi, i, 0)),
        out_shape=jax.ShapeDtypeStruct((b, s, d), F32),
        scratch_shapes=[pltpu.VMEM((d // LANES, halo + tm, LANES), F32),
                        pltpu.VMEM((tm, d), BF16)],
        compiler_params=_params(("arbitrary", "arbitrary")),
        name="conv_mixer",
    )(x, nw, win, wdw, wout)


def _fox_in_body(x_ref, nw_ref, wqkv_ref, wf_ref, bf_ref, qg_ref, kg_ref, e_ref, et_ref,
                 q_ref, k_ref, v_ref, ct_ref, carry_ref, *, n_heads):
    i = pl.program_id(1)
    aw = q_ref.shape[-1]
    x = x_ref[...]
    h = _rmsnorm(x, nw_ref[...]).astype(BF16)
    qkv = _dot(h, wqkv_ref[...])

    def head_norm(t, gain):
        ms = _dot((t * t).astype(BF16), e_ref[...])
        r = lax.rsqrt(ms + RMS_EPS)
        r_hi = r.astype(BF16)
        r_lo = (r - r_hi.astype(F32)).astype(BF16)
        rb = _dot(jnp.concatenate([r_hi, r_lo], axis=1), et_ref[...])
        return t * rb * gain

    q_ref[...] = head_norm(qkv[:, :aw], qg_ref[...]).astype(BF16)
    k_ref[...] = head_norm(qkv[:, aw:2 * aw], kg_ref[...]).astype(BF16)
    v_ref[...] = qkv[:, 2 * aw:].astype(BF16)

    fl = _dot(h, wf_ref[...]) + bf_ref[...]
    logf = -_softplus(-fl) * LOG2E
    local = _lane_cumsum(logf.T[0:n_heads, :])

    @pl.when(i == 0)
    def _():
        carry_ref[...] = jnp.zeros_like(carry_ref)

    cum = local + carry_ref[:, 0:1]
    ct_ref[...] = cum
    tm = cum.shape[-1]
    carry_ref[...] = jnp.broadcast_to(cum[:, tm - 1:tm], carry_ref.shape)


def _fox_in(x, nw, wqkv, wf, bf, qg, kg, e, et, *, n_heads, tm=1024):
    b, s, d = x.shape
    aw = wqkv.shape[1] // 3
    row = lambda bi, i: (bi, i, 0)
    return pl.pallas_call(
        functools.partial(_fox_in_body, n_heads=n_heads),
        grid=(b, s // tm),
        in_specs=[
            pl.BlockSpec((None, tm, d), row),
            _const_spec((1, d)),
            _const_spec(wqkv.shape),
            _const_spec(wf.shape),
            _const_spec(bf.shape),
            _const_spec(qg.shape),
            _const_spec(kg.shape),
            _const_spec(e.shape),
            _const_spec(et.shape),
        ],
        out_specs=[
            pl.BlockSpec((None, tm, aw), row),
            pl.BlockSpec((None, tm, aw), row),
            pl.BlockSpec((None, tm, aw), row),
            pl.BlockSpec((None, n_heads, tm), lambda bi, i: (bi, 0, i)),
        ],
        out_shape=[
            jax.ShapeDtypeStruct((b, s, aw), BF16),
            jax.ShapeDtypeStruct((b, s, aw), BF16),
            jax.ShapeDtypeStruct((b, s, aw), BF16),
            jax.ShapeDtypeStruct((b, n_heads, s), F32),
        ],
        scratch_shapes=[pltpu.VMEM((n_heads, LANES), F32)],
        compiler_params=_params(("arbitrary", "arbitrary")),
        name="fox_in",
    )(x, nw, wqkv, wf, bf, qg, kg, e, et)


def _fox_attn_body(thr_ref, cs_ref, ce_ref, q_ref, k_ref, v_ref, ck_ref, o_ref, p_ref, acc_ref,
                   *, t, n_heads, nsub):
    bi, pr, step = pl.program_id(0), pl.program_id(1), pl.program_id(2)
    hd = ATTN_HEAD_DIM
    lane = lax.broadcasted_iota(jnp.int32, (t, 2 * hd), 1)
    lo = lane < hd
    r0 = bi * n_heads + 2 * pr
    thr = thr_ref[0]
    tiles = [nsub * step + sub for sub in range(nsub)]

    def first_block(i):
        cs0, cs1 = cs_ref[r0, i], cs_ref[r0 + 1, i]

        def needed(j):
            jj = jnp.maximum(j, 0)
            near = (cs0 - ce_ref[r0, jj] >= thr) | (cs1 - ce_ref[r0 + 1, jj] >= thr)
            return (j >= 0) & near

        return lax.while_loop(needed, lambda j: j - 1, i - 1) + 1

    j0s = [first_block(i) for i in tiles]

    def split_heads(sub):
        q2 = q_ref[sub * t:(sub + 1) * t, :]
        zero = jnp.zeros_like(q2)
        return jnp.where(lo, q2, zero), jnp.where(lo, zero, q2)

    def scores(qs, j):
        k2 = k_ref[pl.ds(pl.multiple_of(j * t, t), t), :]
        ck = ck_ref[j]
        return tuple(lax.dot_general(qs[hh], k2, (((1,), (1,)), ((), ())),
                                     preferred_element_type=F32) - ck[hh:hh + 1, :]
                     for hh in range(2))

    def causal(s):
        row = lax.broadcasted_iota(jnp.int32, (t, t), 0)
        col = lax.broadcasted_iota(jnp.int32, (t, t), 1)
        return tuple(jnp.where(row >= col, sh, NEG_BIG) for sh in s)

    @pl.when(thr_ref[1] <= 0.0)
    def _():
        for sub in range(nsub):
            i, j0, qs = tiles[sub], j0s[sub], split_heads(sub)

            def update(j, s, ms, ls, acc):
                v2 = v_ref[pl.ds(pl.multiple_of(j * t, t), t), :]
                new_m, new_l, alphas, pvs = [], [], [], []
                for hh in range(2):
                    m_new = jnp.maximum(ms[hh], jnp.max(s[hh], axis=1, keepdims=True))
                    alpha = jnp.exp2(ms[hh] - m_new)
                    p = jnp.exp2(s[hh] - m_new)
                    new_l.append(alpha * ls[hh] + jnp.sum(p, axis=1, keepdims=True))
                    new_m.append(m_new)
                    alphas.append(alpha)
                    pvs.append(_dot(p.astype(BF16), v2))
                acc = acc * jnp.where(lo, alphas[0], alphas[1]) + jnp.where(lo, pvs[0], pvs[1])
                return tuple(new_m), tuple(new_l), acc

            def body(j, carry, qs=qs, update=update):
                s, ms, ls, acc = carry
                s_next = scores(qs, j + 1)
                ms, ls, acc = update(j, s, ms, ls, acc)
                return s_next, ms, ls, acc

            m0 = jnp.full((t, 1), NEG_BIG, F32)
            l0 = jnp.zeros((t, 1), F32)
            carry = (scores(qs, j0), (m0, m0), (l0, l0), jnp.zeros((t, 2 * hd), F32))
            s, ms, ls, acc = lax.fori_loop(j0, i, body, carry)
            _, ls, acc = update(i, causal(s), ms, ls, acc)
            o_ref[sub * t:(sub + 1) * t, :] = (acc / jnp.where(lo, ls[0], ls[1])).astype(o_ref.dtype)

    @pl.when(thr_ref[1] > 0.0)
    def _():
        e_lo = jnp.where(lo, 1.0, 0.0).astype(BF16)
        e_hi = jnp.where(lo, 0.0, 1.0).astype(BF16)

        def pv(j, p16):
            v2 = v_ref[pl.ds(pl.multiple_of(j * t, t), t), :]
            vb = jnp.concatenate(
                [jnp.concatenate([v2 * e_lo, e_lo], axis=1),
                 jnp.concatenate([v2 * e_hi, e_hi], axis=1)], axis=0)
            return _dot(p16, vb)

        def weights(s, offs):
            return jnp.concatenate([jnp.exp2(s[hh] - offs[hh]).astype(BF16) for hh in range(2)], axis=1)

        qss, offss = [], []
        for sub in range(nsub):
            i, qs = tiles[sub], split_heads(sub)
            ck_i = ck_ref[i]
            offs = tuple(thr_ref[2] - jnp.broadcast_to(ck_i[hh:hh + 1, :], (LANES, t)).T[:, 0:1]
                         for hh in range(2))
            p_ref[sub, 0] = weights(causal(scores(qs, i)), offs)
            acc_ref[sub] = jnp.zeros((t, 4 * hd), F32)
            qss.append(qs)
            offss.append(offs)

        jprevs = []
        for sub in range(nsub):
            i, j0 = tiles[sub], j0s[sub]

            def body(j, jprev, sub=sub, j0=j0):
                slot = (j - j0) & 1
                p_new = weights(scores(qss[sub], j), offss[sub])
                acc_ref[sub] += pv(jprev, p_ref[sub, slot])
                p_ref[sub, 1 - slot] = p_new
                return j

            jprevs.append(lax.fori_loop(j0, i, body, i))

        for sub in range(nsub):
            i, j0 = tiles[sub], j0s[sub]
            acc = acc_ref[sub] + pv(jprevs[sub], p_ref[sub, (i - j0) & 1])
            o_ref[sub * t:(sub + 1) * t, :] = (acc[:, :2 * hd] / acc[:, 2 * hd:]).astype(o_ref.dtype)


def _fox_attn(thr, cs, ce, q, k, v, ck, *, t, n_heads, nsub=8):
    b, s, aw = q.shape
    pairs = aw // LANES
    smem = pl.BlockSpec(memory_space=pltpu.SMEM)
    return pl.pallas_call(
        functools.partial(_fox_attn_body, t=t, n_heads=n_heads, nsub=nsub),
        grid=(b, pairs, s // (nsub * t)),
        in_specs=[
            smem, smem, smem,
            pl.BlockSpec((None, nsub * t, LANES), lambda bi, p, i: (bi, i, p)),
            pl.BlockSpec((None, s, LANES), lambda bi, p, i: (bi, 0, p)),
            pl.BlockSpec((None, s, LANES), lambda bi, p, i: (bi, 0, p)),
            pl.BlockSpec((None, None, s // t, 2, t), lambda bi, p, i: (bi, p, 0, 0, 0)),
        ],
        out_specs=pl.BlockSpec((None, nsub * t, LANES), lambda bi, p, i: (bi, i, p)),
        out_shape=jax.ShapeDtypeStruct((b, s, aw), BF16),
        scratch_shapes=[pltpu.VMEM((nsub, 2, t, 2 * t), BF16), pltpu.VMEM((nsub, t, 2 * LANES), F32)],
        compiler_params=_params(("arbitrary", "arbitrary", "arbitrary")),
        name="fox_attn",
    )(thr, cs, ce, q, k, v, ck)


def _ssd_in_body(x_ref, nw_ref, w_ref, wdt_ref, cw_ref, cb_ref, dtb_ref,
                 z_ref, xs_ref, b_ref, c_ref, dt_ref, ext_ref, *, tm, halo, chunk):
    i = pl.program_id(1)
    di = xs_ref.shape[-1]
    gn = b_ref.shape[-1]
    cdim = cw_ref.shape[-1]
    x = x_ref[...]
    h = _rmsnorm(x, nw_ref[...]).astype(BF16)

    @pl.when(i == 0)
    def _():
        ext_ref[:, 0:halo, :] = jnp.zeros((cdim // LANES, halo, LANES), F32)

    @pl.when(i > 0)
    def _():
        ext_ref[:, 0:halo, :] = ext_ref[:, tm:tm + halo, :]

    dt_ref[...] = _softplus(_dot(h, wdt_ref[...]) + dtb_ref[...])
    kw = cw_ref.shape[0]
    outs = ((xs_ref, 0, di), (b_ref, di, di + gn), (c_ref, di + gn, cdim))
    for o_ref, lo_c, hi_c in outs:
        for q0 in range(lo_c, hi_c, chunk):
            raw = _dot(h, w_ref[:, di + q0:di + q0 + chunk])
            if q0 < di:
                z_ref[:, q0:q0 + chunk] = _dot(h, w_ref[:, q0:q0 + chunk]).astype(z_ref.dtype)
            for r0 in range(0, chunk, LANES):
                c0 = q0 + r0
                sl = c0 // LANES
                cs = slice(c0, c0 + LANES)
                cur = raw[:, r0:r0 + LANES]
                ext_ref[sl, halo:halo + tm, :] = cur
                u = cur * cw_ref[kw - 1:kw, cs] + cb_ref[:, cs]
                for k in range(kw - 1):
                    off = halo - (kw - 1) + k
                    u = u + ext_ref[sl, off:off + tm, :] * cw_ref[k:k + 1, cs]
                o_ref[:, c0 - lo_c:c0 - lo_c + LANES] = _silu(u).astype(o_ref.dtype)


def _ssd_in(x, nw, w, wdt, cw, cb, dtb, *, di, tm=512):
    b, s, d = x.shape
    cdim = cw.shape[1]
    gn = (cdim - di) // 2
    halo = 8
    chunk = 512
    row = lambda bi, i: (bi, i, 0)
    return pl.pallas_call(
        functools.partial(_ssd_in_body, tm=tm, halo=halo, chunk=chunk),
        grid=(b, s // tm),
        in_specs=[
            pl.BlockSpec((None, tm, d), row),
            _const_spec((1, d)),
            _const_spec(w.shape),
            _const_spec(wdt.shape),
            _const_spec(cw.shape),
            _const_spec(cb.shape),
            _const_spec(dtb.shape),
        ],
        out_specs=[
            pl.BlockSpec((None, tm, di), row),
            pl.BlockSpec((None, tm, di), row),
            pl.BlockSpec((None, tm, gn), row),
            pl.BlockSpec((None, tm, gn), row),
            pl.BlockSpec((None, tm, LANES), row),
        ],
        out_shape=[
            jax.ShapeDtypeStruct((b, s, di), BF16),
            jax.ShapeDtypeStruct((b, s, di), BF16),
            jax.ShapeDtypeStruct((b, s, gn), BF16),
            jax.ShapeDtypeStruct((b, s, gn), BF16),
            jax.ShapeDtypeStruct((b, s, LANES), F32),
        ],
        scratch_shapes=[pltpu.VMEM((cdim // LANES, halo + tm, LANES), F32)],
        compiler_params=_params(("arbitrary", "arbitrary")),
        name="ssd_in",
    )(x, nw, w, wdt, cw, cb, dtb)


def _ssd_scan_body(xs_ref, b_ref, c_ref, dt_ref, alog_ref, dskip_ref, eh_ref, y_ref, st_ref,
                   *, n_heads, nc):
    ci = pl.program_id(1)
    L = SSM_CHUNK
    P, N, G = SSM_HEAD_DIM, SSM_STATE, SSM_GROUPS
    hpg = n_heads // G
    gw = hpg * P

    @pl.when(ci == 0)
    def _():
        st_ref[...] = jnp.zeros_like(st_ref)

    lane_h = lax.broadcasted_iota(jnp.int32, (1, LANES), 1)
    a = jnp.where(lane_h < n_heads, -jnp.exp(alog_ref[...]), 0.0)
    row = lax.broadcasted_iota(jnp.int32, (L, L), 0)
    col = lax.broadcasted_iota(jnp.int32, (L, L), 1)
    tril = row >= col
    tril16 = jnp.where(tril, 1.0, 0.0).astype(BF16)
    eh = eh_ref[...]
    head_of_lane = lax.broadcasted_iota(jnp.int32, (L, gw), 1) // P
    head_mask = [jnp.where(head_of_lane == k, 1.0, 0.0).astype(BF16) for k in range(hpg)]

    def prologue(c):
        rs = slice(c * L, (c + 1) * L)
        dt = dt_ref[rs, :]
        dtT = dt.T
        da = dt * a
        da_hi = da.astype(BF16)
        rest = da - da_hi.astype(F32)
        da_mid = rest.astype(BF16)
        da_lo = (rest - da_mid.astype(F32)).astype(BF16)
        parts = _dot(tril16, jnp.concatenate([da_hi, da_mid, da_lo], axis=1))
        acum = parts[:, :LANES] + parts[:, LANES:2 * LANES] + parts[:, 2 * LANES:]
        acumT = acum.T
        a_last = acum[L - 1:L, :]
        sdt_b = _dot((jnp.exp(a_last - acum) * dt).astype(BF16), eh)
        ea_b = _dot(jnp.exp(acum).astype(BF16), eh)
        dl = jnp.broadcast_to(jnp.exp(a_last), (16, LANES))
        dl_hi = dl.astype(BF16)
        dl_lo = (dl - dl_hi.astype(F32)).astype(BF16)
        decay_last = (_dot(dl_hi, eh) + _dot(dl_lo, eh))[0:1, :]
        return dtT, acum, acumT, sdt_b, ea_b, decay_last

    def main(c, pro):
        dtT, acum, acumT, sdt_b, ea_b, decay_last = pro
        rs = slice(c * L, (c + 1) * L)
        xs16 = xs_ref[rs, :]
        xs = xs16.astype(F32)
        xw = (xs * sdt_b).astype(BF16)
        cbs, y_offs = [], []
        for g in range(G):
            bg = b_ref[rs, g * N:(g + 1) * N]
            cg = c_ref[rs, g * N:(g + 1) * N]
            cbs.append(lax.dot_general(cg, bg, (((1,), (1,)), ((), ())), preferred_element_type=F32))
            gs = slice(g * gw, (g + 1) * gw)
            st = st_ref[g]
            y_offs.append(_dot(cg, st.astype(BF16)) * ea_b[:, gs])
            contrib = lax.dot_general(bg, xw[:, gs], (((0,), (0,)), ((), ())),
                                      preferred_element_type=F32)
            st_ref[g] = st * decay_last[:, gs] + contrib

        for g in range(G):
            gs = slice(g * gw, (g + 1) * gw)
            cb, y_off = cbs[g], y_offs[g]
            ws = []
            for k in range(hpg):
                hidx = g * hpg + k
                seg = acum[:, hidx:hidx + 1] - acumT[hidx:hidx + 1, :]
                decay = jnp.exp(jnp.where(tril, seg, NEG_BIG))
                ws.append((cb * decay * dtT[hidx:hidx + 1, :]).astype(BF16))
            xg = xs16[:, gs]
            xblk = jnp.concatenate([xg * head_mask[k] for k in range(hpg)], axis=0)
            y_diag = _dot(jnp.concatenate(ws, axis=1), xblk)
            y_ref[rs, gs] = (y_diag + y_off + dskip_ref[:, gs] * xs[:, gs]).astype(y_ref.dtype)

    pros = [prologue(c) for c in range(nc)]
    for c in range(nc):
        main(c, pros[c])


def _ssd_scan(xs, bm, cm, dt, alog, dskip, eh, *, n_heads, nc=4):
    b, s, di = xs.shape
    gn = bm.shape[-1]
    L = nc * SSM_CHUNK
    row = lambda bi, i: (bi, i, 0)
    return pl.pallas_call(
        functools.partial(_ssd_scan_body, n_heads=n_heads, nc=nc),
        grid=(b, s // L),
        in_specs=[
            pl.BlockSpec((None, L, di), row),
            pl.BlockSpec((None, L, gn), row),
            pl.BlockSpec((None, L, gn), row),
            pl.BlockSpec((None, L, LANES), row),
            _const_spec(alog.shape),
            _const_spec(dskip.shape),
            _const_spec(eh.shape),
        ],
        out_specs=pl.BlockSpec((None, L, di), row),
        out_shape=jax.ShapeDtypeStruct((b, s, di), BF16),
        scratch_shapes=[pltpu.VMEM((SSM_GROUPS, SSM_STATE, di // SSM_GROUPS), F32)],
        compiler_params=_params(("arbitrary", "arbitrary")),
        name="ssd_scan",
    )(xs, bm, cm, dt, alog, dskip, eh)


def _ssd_out_body(y_ref, z_ref, nw_ref, w_ref, x_ref, o_ref, yn_ref):
    di = y_ref.shape[-1]
    gw = di // SSM_GROUPS
    for g in range(SSM_GROUPS):
        gs = slice(g * gw, (g + 1) * gw)
        yg = y_ref[:, gs].astype(F32) * _silu(z_ref[:, gs].astype(F32))
        yn_ref[:, gs] = _rmsnorm(yg, nw_ref[:, gs]).astype(BF16)
    o_ref[...] = x_ref[...] + _dot(yn_ref[...], w_ref[...])


def _ssd_out(y, z, nw, w, x, *, tm=1024):
    m, di = y.shape
    d = w.shape[1]
    return pl.pallas_call(
        _ssd_out_body,
        grid=(m // tm,),
        in_specs=[
            pl.BlockSpec((tm, di), lambda i: (i, 0)),
            pl.BlockSpec((tm, di), lambda i: (i, 0)),
            _const_spec(nw.shape),
            _const_spec(w.shape),
            pl.BlockSpec((tm, d), lambda i: (i, 0)),
        ],
        out_specs=pl.BlockSpec((tm, d), lambda i: (i, 0)),
        out_shape=jax.ShapeDtypeStruct((m, d), F32),
        scratch_shapes=[pltpu.VMEM((tm, di), BF16)],
        compiler_params=_params(("arbitrary",)),
        name="ssd_out",
    )(y, z, nw, w, x)


def _pad_lanes(a, width=LANES):
    return jnp.pad(a, ((0, 0), (0, width - a.shape[-1])))


def _conv_layer(x, nw, w_in, w_dw, w_out, *, layer):
    return _conv_mixer(x, nw[None], w_in, w_dw, w_out, layer=layer)


def _fox_layer(x, nw, w_in, b_f, q_gain, k_gain, w_out, *, t=256):
    b, s, d = x.shape
    n_heads = b_f.shape[0]
    hd = ATTN_HEAD_DIM
    aw = n_heads * hd
    perm = jnp.argsort(b_f)

    def by_head(w):
        return w.reshape(d, n_heads, hd)[:, perm].reshape(d, aw)

    wqkv = jnp.concatenate([by_head(w_in[:, j * aw:(j + 1) * aw]) for j in range(3)],
                           axis=1).astype(BF16)
    wf = _pad_lanes(w_in[:, 3 * aw:][:, perm]).astype(BF16)
    bf = _pad_lanes(b_f[perm][None])
    w_out = w_out.reshape(n_heads, hd, d)[perm].reshape(aw, d)
    qg = jnp.tile(q_gain, n_heads)[None] * (hd ** -0.5 * LOG2E)
    kg = jnp.tile(k_gain, n_heads)[None]
    smax = 1.02 * LOG2E * hd ** 0.5 * jnp.max(jnp.abs(q_gain)) * jnp.max(jnp.abs(k_gain))
    fast = (2.0 * smax <= FAST_PATH_MAX_LOG2).astype(F32)
    thr = jnp.stack([-(SKIP_LOG2 + 2.0 * smax), fast, smax]).astype(F32)
    head_of = jnp.arange(aw) // hd
    e = (head_of[:, None] == jnp.arange(LANES)[None, :]).astype(F32)
    et = jnp.concatenate([e.T, e.T], axis=0).astype(BF16)
    e = (e / hd).astype(BF16)
    q, k, v, ct = _fox_in(x, nw[None], wqkv, wf, bf, qg, kg, e, et, n_heads=n_heads)
    ck = ct.reshape(b, n_heads // 2, 2, s // t, t).transpose(0, 1, 3, 2, 4)
    cs = ct[:, :, 0::t].reshape(b * n_heads, s // t)
    ce = ct[:, :, t - 1::t].reshape(b * n_heads, s // t)
    attn = _fox_attn(thr, cs, ce, q, k, v, ck, t=t, n_heads=n_heads)
    return attn.reshape(b * s, aw), w_out.astype(BF16)


def _ssd_layer(x, nw, w_in, conv_w, conv_b, dt_bias, a_log, d_skip, norm_w, w_out):
    b, s, d = x.shape
    n_heads = a_log.shape[0]
    di = n_heads * SSM_HEAD_DIM
    cdim = conv_w.shape[1]
    wdt = _pad_lanes(w_in[:, di + cdim:]).astype(BF16)
    z, xs, bm, cm, dt = _ssd_in(x, nw[None], w_in.astype(BF16), wdt, conv_w, conv_b[None],
                                _pad_lanes(dt_bias[None]), di=di)
    head_of = jnp.arange(di) // SSM_HEAD_DIM
    eh = (jnp.arange(LANES)[:, None] == head_of[None, :]).astype(BF16)
    dskip = jnp.repeat(d_skip, SSM_HEAD_DIM)[None]
    y = _ssd_scan(xs, bm, cm, dt, _pad_lanes(a_log[None]), dskip, eh, n_heads=n_heads)
    m = b * s
    return _ssd_out(y.reshape(m, di), z.reshape(m, di), norm_w[None], w_out.astype(BF16),
                    x.reshape(m, d)).reshape(b, s, d)


def kernel(x, mix_norm, ffn_norm, ffn_w_gu, ffn_w_down, conv_w_in, conv_w_dw, conv_w_out, fox_w_in, fox_b_f, fox_q_gain, fox_k_gain, fox_w_out, ssd_w_in, ssd_conv_w, ssd_conv_b, ssd_dt_bias, ssd_a_log, ssd_d, ssd_norm_w, ssd_w_out):
    b, s, d = x.shape
    depth = mix_norm.shape[0]
    ffn_wgu16, ffn_wd16 = ffn_w_gu.astype(BF16), ffn_w_down.astype(BF16)
    conv_win16, conv_wout16 = conv_w_in.astype(BF16), conv_w_out.astype(BF16)
    for i in range(depth):
        kind, j = i % 3, i // 3
        proj = None
        if kind == 0:
            x = _conv_layer(x, mix_norm[i], conv_win16, conv_w_dw[j], conv_wout16, layer=j)
        elif kind == 1:
            proj = _fox_layer(x, mix_norm[i], fox_w_in[j], fox_b_f[j], fox_q_gain[j], fox_k_gain[j],
                              fox_w_out[j])
        else:
            x = _ssd_layer(x, mix_norm[i], ssd_w_in[j], ssd_conv_w[j], ssd_conv_b[j],
                           ssd_dt_bias[j], ssd_a_log[j], ssd_d[j], ssd_norm_w[j], ssd_w_out[j])
        x = _ffn(x.reshape(b * s, d), ffn_norm[i][None], ffn_wgu16, ffn_wd16,
                 layer=i, proj=proj).reshape(b, s, d)
    return x
```

```python
import functools

import jax
import jax.numpy as jnp
from jax import lax
from jax.experimental import pallas as pl
from jax.experimental.pallas import tpu as pltpu

F32 = jnp.float32
BF16 = jnp.bfloat16

RMS_EPS = 1e-6
ATTN_HEAD_DIM = 64
SSM_HEAD_DIM = 64
SSM_GROUPS = 8
SSM_STATE = 128
SSM_CHUNK = 128
LANES = 128
NEG_BIG = -1e30
LOG2E = 1.4426950408889634
SKIP_LOG2 = 64.0
FAST_PATH_MAX_LOG2 = 100.0

VMEM_LIMIT = 56 * 1024 * 1024


def _params(sem):
    return pltpu.CompilerParams(dimension_semantics=sem, vmem_limit_bytes=VMEM_LIMIT)


def _const_spec(shape):
    nd = len(shape)
    return pl.BlockSpec(shape, lambda *_: (0,) * nd, pipeline_mode=pl.Buffered(1))


def _layer_spec(shape, layer):
    return pl.BlockSpec((None,) + tuple(shape[1:]), lambda *_: (layer, 0, 0),
                        pipeline_mode=pl.Buffered(1))


def _rmsnorm(x, w):
    return x * lax.rsqrt(jnp.mean(x * x, axis=-1, keepdims=True) + RMS_EPS) * w


def _dot(a, b):
    return jnp.dot(a, b, preferred_element_type=F32)


def _softplus(x):
    return jnp.maximum(x, 0.0) + jnp.log1p(jnp.exp(-jnp.abs(x)))


def _silu(x):
    return x * jax.nn.sigmoid(x)


def _lane_cumsum(x):
    n = x.shape[-1]
    lane = lax.broadcasted_iota(jnp.int32, x.shape, x.ndim - 1)
    s = 1
    while s < n:
        x = x + jnp.where(lane >= s, pltpu.roll(x, s, x.ndim - 1), 0.0)
        s *= 2
    return x


def _ffn_body(*refs, chunk, with_proj):
    if with_proj:
        y_ref, wo_ref, x_ref, nw_ref, wgu_ref, wd_ref, o_ref, a_ref = refs
        x = x_ref[...] + _dot(y_ref[...], wo_ref[...])
    else:
        x_ref, nw_ref, wgu_ref, wd_ref, o_ref, a_ref = refs
        x = x_ref[...]
    h = _rmsnorm(x, nw_ref[...]).astype(BF16)
    dff = wd_ref.shape[0]
    for c0 in range(0, dff, chunk):
        g = _dot(h, wgu_ref[:, c0:c0 + chunk].astype(BF16))
        u = _dot(h, wgu_ref[:, dff + c0:dff + c0 + chunk].astype(BF16))
        a_ref[:, c0:c0 + chunk] = (_silu(g) * u).astype(BF16)
    o_ref[...] = x + _dot(a_ref[...], wd_ref[...].astype(BF16))


def _ffn(x, nw, wgu, wd, *, layer, proj=None, tm=512, chunk=256):
    m, d = x.shape
    dff = wd.shape[1]
    rows = lambda i: (i, 0)
    args, specs = [], []
    if proj is not None:
        y, w_o = proj
        args += [y, w_o]
        specs += [pl.BlockSpec((tm, y.shape[1]), rows), _const_spec(w_o.shape)]
    args += [x, nw, wgu, wd]
    specs += [pl.BlockSpec((tm, d), rows), _const_spec((1, d)), _layer_spec(wgu.shape, layer),
              _layer_spec(wd.shape, layer)]
    return pl.pallas_call(
        functools.partial(_ffn_body, chunk=chunk, with_proj=proj is not None),
        grid=(m // tm,),
        in_specs=specs,
        out_specs=pl.BlockSpec((tm, d), rows),
        out_shape=jax.ShapeDtypeStruct((m, d), F32),
        scratch_shapes=[pltpu.VMEM((tm, dff), BF16)],
        compiler_params=_params(("arbitrary",)),
        name="ffn",
    )(*args)


def _conv_mixer_body(x_ref, nw_ref, win_ref, wdw_ref, wout_ref, o_ref, ext_ref, y_ref,
                     *, tm, halo, chunk):
    i = pl.program_id(1)
    d = x_ref.shape[-1]
    x = x_ref[...]
    h = _rmsnorm(x, nw_ref[...]).astype(BF16)

    @pl.when(i == 0)
    def _():
        ext_ref[:, 0:halo, :] = jnp.zeros((d // LANES, halo, LANES), F32)

    @pl.when(i > 0)
    def _():
        ext_ref[:, 0:halo, :] = ext_ref[:, tm:tm + halo, :]

    kw = wdw_ref.shape[0]
    for q0 in range(0, d, chunk):
        bq = _dot(h, win_ref[:, q0:q0 + chunk])
        cq = _dot(h, win_ref[:, d + q0:d + q0 + chunk])
        vq = _dot(h, win_ref[:, 2 * d + q0:2 * d + q0 + chunk])
        for r0 in range(0, chunk, LANES):
            c0 = q0 + r0
            sl = c0 // LANES
            cs = slice(c0, c0 + LANES)
            rs = slice(r0, r0 + LANES)
            cv = cq[:, rs] * vq[:, rs]
            ext_ref[sl, halo:halo + tm, :] = cv
            u = cv * wdw_ref[kw - 1:kw, cs]
            for k in range(kw - 1):
                off = halo - (kw - 1) + k
                u = u + ext_ref[sl, off:off + tm, :] * wdw_ref[k:k + 1, cs]
            y_ref[:, cs] = (bq[:, rs] * u).astype(BF16)
    o_ref[...] = x + _dot(y_ref[...], wout_ref[...])


def _conv_mixer(x, nw, win, wdw, wout, *, layer, tm=1024):
    b, s, d = x.shape
    halo = 8
    return pl.pallas_call(
        functools.partial(_conv_mixer_body, tm=tm, halo=halo, chunk=256),
        grid=(b, s // tm),
        in_specs=[
            pl.BlockSpec((None, tm, d), lambda bi, i: (bi, i, 0)),
            _const_spec((1, d)),
            _layer_spec(win.shape, layer),
            _const_spec(wdw.shape),
            _layer_spec(wout.shape, layer),
        ],
        out_specs=pl.BlockSpec((None, tm, d), lambda bi, i: (bi, i, 0)),
        out_shape=jax.ShapeDtypeStruct((b, s, d), F32),
        scratch_shapes=[pltpu.VMEM((d // LANES, halo + tm, LANES), F32),
                        pltpu.VMEM((tm, d), BF16)],
        compiler_params=_params(("arbitrary", "arbitrary")),
        name="conv_mixer",
    )(x, nw, win, wdw, wout)


def _fox_in_body(x_ref, nw_ref, wqkv_ref, wf_ref, bf_ref, qg_ref, kg_ref, e_ref, et_ref,
                 q_ref, k_ref, v_ref, ct_ref, carry_ref, *, n_heads):
    i = pl.program_id(1)
    aw = q_ref.shape[-1]
    x = x_ref[...]
    h = _rmsnorm(x, nw_ref[...]).astype(BF16)
    qkv = _dot(h, wqkv_ref[...])

    def head_norm(t, gain):
        ms = _dot((t * t).astype(BF16), e_ref[...])
        r = lax.rsqrt(ms + RMS_EPS)
        r_hi = r.astype(BF16)
        r_lo = (r - r_hi.astype(F32)).astype(BF16)
        rb = _dot(jnp.concatenate([r_hi, r_lo], axis=1), et_ref[...])
        return t * rb * gain

    q_ref[...] = head_norm(qkv[:, :aw], qg_ref[...]).astype(BF16)
    k_ref[...] = head_norm(qkv[:, aw:2 * aw], kg_ref[...]).astype(BF16)
    v_ref[...] = qkv[:, 2 * aw:].astype(BF16)

    fl = _dot(h, wf_ref[...]) + bf_ref[...]
    logf = -_softplus(-fl) * LOG2E
    local = _lane_cumsum(logf.T[0:n_heads, :])

    @pl.when(i == 0)
    def _():
        carry_ref[...] = jnp.zeros_like(carry_ref)

    cum = local + carry_ref[:, 0:1]
    ct_ref[...] = cum
    tm = cum.shape[-1]
    carry_ref[...] = jnp.broadcast_to(cum[:, tm - 1:tm], carry_ref.shape)


def _fox_in(x, nw, wqkv, wf, bf, qg, kg, e, et, *, n_heads, tm=1024):
    b, s, d = x.shape
    aw = wqkv.shape[1] // 3
    row = lambda bi, i: (bi, i, 0)
    return pl.pallas_call(
        functools.partial(_fox_in_body, n_heads=n_heads),
        grid=(b, s // tm),
        in_specs=[
            pl.BlockSpec((None, tm, d), row),
            _const_spec((1, d)),
            _const_spec(wqkv.shape),
            _const_spec(wf.shape),
            _const_spec(bf.shape),
            _const_spec(qg.shape),
            _const_spec(kg.shape),
            _const_spec(e.shape),
            _const_spec(et.shape),
        ],
        out_specs=[
            pl.BlockSpec((None, tm, aw), row),
            pl.BlockSpec((None, tm, aw), row),
            pl.BlockSpec((None, tm, aw), row),
            pl.BlockSpec((None, n_heads, tm), lambda bi, i: (bi, 0, i)),
        ],
        out_shape=[
            jax.ShapeDtypeStruct((b, s, aw), BF16),
            jax.ShapeDtypeStruct((b, s, aw), BF16),
            jax.ShapeDtypeStruct((b, s, aw), BF16),
            jax.ShapeDtypeStruct((b, n_heads, s), F32),
        ],
        scratch_shapes=[pltpu.VMEM((n_heads, LANES), F32)],
        compiler_params=_params(("arbitrary", "arbitrary")),
        name="fox_in",
    )(x, nw, wqkv, wf, bf, qg, kg, e, et)


def _fox_attn_body(thr_ref, cs_ref, ce_ref, q_ref, k_ref, v_ref, ck_ref, o_ref, p_ref, acc_ref,
                   *, t, n_heads, nsub):
    bi, pr, step = pl.program_id(0), pl.program_id(1), pl.program_id(2)
    hd = ATTN_HEAD_DIM
    lane = lax.broadcasted_iota(jnp.int32, (t, 2 * hd), 1)
    lo = lane < hd
    r0 = bi * n_heads + 2 * pr
    thr = thr_ref[0]
    tiles = [nsub * step + sub for sub in range(nsub)]

    def first_block(i):
        cs0, cs1 = cs_ref[r0, i], cs_ref[r0 + 1, i]

        def needed(j):
            jj = jnp.maximum(j, 0)
            near = (cs0 - ce_ref[r0, jj] >= thr) | (cs1 - ce_ref[r0 + 1, jj] >= thr)
            return (j >= 0) & near

        return lax.while_loop(needed, lambda j: j - 1, i - 1) + 1

    j0s = [first_block(i) for i in tiles]

    def split_heads(sub):
        q2 = q_ref[sub * t:(sub + 1) * t, :]
        zero = jnp.zeros_like(q2)
        return jnp.where(lo, q2, zero), jnp.where(lo, zero, q2)

    def scores(qs, j):
        k2 = k_ref[pl.ds(pl.multiple_of(j * t, t), t), :]
        ck = ck_ref[j]
        return tuple(lax.dot_general(qs[hh], k2, (((1,), (1,)), ((), ())),
                                     preferred_element_type=F32) - ck[hh:hh + 1, :]
                     for hh in range(2))

    def causal(s):
        row = lax.broadcasted_iota(jnp.int32, (t, t), 0)
        col = lax.broadcasted_iota(jnp.int32, (t, t), 1)
        return tuple(jnp.where(row >= col, sh, NEG_BIG) for sh in s)

    @pl.when(thr_ref[1] <= 0.0)
    def _():
        for sub in range(nsub):
            i, j0, qs = tiles[sub], j0s[sub], split_heads(sub)

            def update(j, s, ms, ls, acc):
                v2 = v_ref[pl.ds(pl.multiple_of(j * t, t), t), :]
                new_m, new_l, alphas, pvs = [], [], [], []
                for hh in range(2):
                    m_new = jnp.maximum(ms[hh], jnp.max(s[hh], axis=1, keepdims=True))
                    alpha = jnp.exp2(ms[hh] - m_new)
                    p = jnp.exp2(s[hh] - m_new)
                    new_l.append(alpha * ls[hh] + jnp.sum(p, axis=1, keepdims=True))
                    new_m.append(m_new)
                    alphas.append(alpha)
                    pvs.append(_dot(p.astype(BF16), v2))
                acc = acc * jnp.where(lo, alphas[0], alphas[1]) + jnp.where(lo, pvs[0], pvs[1])
                return tuple(new_m), tuple(new_l), acc

            def body(j, carry, qs=qs, update=update):
                s, ms, ls, acc = carry
                s_next = scores(qs, j + 1)
                ms, ls, acc = update(j, s, ms, ls, acc)
                return s_next, ms, ls, acc

            m0 = jnp.full((t, 1), NEG_BIG, F32)
            l0 = jnp.zeros((t, 1), F32)
            carry = (scores(qs, j0), (m0, m0), (l0, l0), jnp.zeros((t, 2 * hd), F32))
            s, ms, ls, acc = lax.fori_loop(j0, i, body, carry)
            _, ls, acc = update(i, causal(s), ms, ls, acc)
            o_ref[sub * t:(sub + 1) * t, :] = (acc / jnp.where(lo, ls[0], ls[1])).astype(o_ref.dtype)

    @pl.when(thr_ref[1] > 0.0)
    def _():
        e_lo = jnp.where(lo, 1.0, 0.0).astype(BF16)
        e_hi = jnp.where(lo, 0.0, 1.0).astype(BF16)

        def pv(j, p16):
            v2 = v_ref[pl.ds(pl.multiple_of(j * t, t), t), :]
            vb = jnp.concatenate(
                [jnp.concatenate([v2 * e_lo, e_lo], axis=1),
                 jnp.concatenate([v2 * e_hi, e_hi], axis=1)], axis=0)
            return _dot(p16, vb)

        def weights(s, offs):
            return jnp.concatenate([jnp.exp2(s[hh] - offs[hh]).astype(BF16) for hh in range(2)], axis=1)

        qss, offss = [], []
        for sub in range(nsub):
            i, qs = tiles[sub], split_heads(sub)
            ck_i = ck_ref[i]
            offs = tuple(thr_ref[2] - jnp.broadcast_to(ck_i[hh:hh + 1, :], (LANES, t)).T[:, 0:1]
                         for hh in range(2))
            p_ref[sub, 0] = weights(causal(scores(qs, i)), offs)
            acc_ref[sub] = jnp.zeros((t, 4 * hd), F32)
            qss.append(qs)
            offss.append(offs)

        jprevs = []
        for sub in range(nsub):
            i, j0 = tiles[sub], j0s[sub]

            def body(j, jprev, sub=sub, j0=j0):
                slot = (j - j0) & 1
                p_new = weights(scores(qss[sub], j), offss[sub])
                acc_ref[sub] += pv(jprev, p_ref[sub, slot])
                p_ref[sub, 1 - slot] = p_new
                return j

            jprevs.append(lax.fori_loop(j0, i, body, i))

        for sub in range(nsub):
            i, j0 = tiles[sub], j0s[sub]
            acc = acc_ref[sub] + pv(jprevs[sub], p_ref[sub, (i - j0) & 1])
            o_ref[sub * t:(sub + 1) * t, :] = (acc[:, :2 * hd] / acc[:, 2 * hd:]).astype(o_ref.dtype)


def _fox_attn(thr, cs, ce, q, k, v, ck, *, t, n_heads, nsub=8):
    b, s, aw = q.shape
    pairs = aw // LANES
    smem = pl.BlockSpec(memory_space=pltpu.SMEM)
    return pl.pallas_call(
        functools.partial(_fox_attn_body, t=t, n_heads=n_heads, nsub=nsub),
        grid=(b, pairs, s // (nsub * t)),
        in_specs=[
            smem, smem, smem,
            pl.BlockSpec((None, nsub * t, LANES), lambda bi, p, i: (bi, i, p)),
            pl.BlockSpec((None, s, LANES), lambda bi, p, i: (bi, 0, p)),
            pl.BlockSpec((None, s, LANES), lambda bi, p, i: (bi, 0, p)),
            pl.BlockSpec((None, None, s // t, 2, t), lambda bi, p, i: (bi, p, 0, 0, 0)),
        ],
        out_specs=pl.BlockSpec((None, nsub * t, LANES), lambda bi, p, i: (bi, i, p)),
        out_shape=jax.ShapeDtypeStruct((b, s, aw), BF16),
        scratch_shapes=[pltpu.VMEM((nsub, 2, t, 2 * t), BF16), pltpu.VMEM((nsub, t, 2 * LANES), F32)],
        compiler_params=_params(("arbitrary", "arbitrary", "arbitrary")),
        name="fox_attn",
    )(thr, cs, ce, q, k, v, ck)


def _ssd_in_body(x_ref, nw_ref, w_ref, wdt_ref, cw_ref, cb_ref, dtb_ref,
                 z_ref, xs_ref, b_ref, c_ref, dt_ref, ext_ref, *, tm, halo, chunk):
    i = pl.program_id(1)
    di = xs_ref.shape[-1]
    gn = b_ref.shape[-1]
    cdim = cw_ref.shape[-1]
    x = x_ref[...]
    h = _rmsnorm(x, nw_ref[...]).astype(BF16)

    @pl.when(i == 0)
    def _():
        ext_ref[:, 0:halo, :] = jnp.zeros((cdim // LANES, halo, LANES), F32)

    @pl.when(i > 0)
    def _():
        ext_ref[:, 0:halo, :] = ext_ref[:, tm:tm + halo, :]

    dt_ref[...] = _softplus(_dot(h, wdt_ref[...]) + dtb_ref[...])
    kw = cw_ref.shape[0]
    outs = ((xs_ref, 0, di), (b_ref, di, di + gn), (c_ref, di + gn, cdim))
    for o_ref, lo_c, hi_c in outs:
        for q0 in range(lo_c, hi_c, chunk):
            raw = _dot(h, w_ref[:, di + q0:di + q0 + chunk])
            if q0 < di:
                z_ref[:, q0:q0 + chunk] = _dot(h, w_ref[:, q0:q0 + chunk]).astype(z_ref.dtype)
            for r0 in range(0, chunk, LANES):
                c0 = q0 + r0
                sl = c0 // LANES
                cs = slice(c0, c0 + LANES)
                cur = raw[:, r0:r0 + LANES]
                ext_ref[sl, halo:halo + tm, :] = cur
                u = cur * cw_ref[kw - 1:kw, cs] + cb_ref[:, cs]
                for k in range(kw - 1):
                    off = halo - (kw - 1) + k
                    u = u + ext_ref[sl, off:off + tm, :] * cw_ref[k:k + 1, cs]
                o_ref[:, c0 - lo_c:c0 - lo_c + LANES] = _silu(u).astype(o_ref.dtype)


def _ssd_in(x, nw, w, wdt, cw, cb, dtb, *, di, tm=512):
    b, s, d = x.shape
    cdim = cw.shape[1]
    gn = (cdim - di) // 2
    halo = 8
    chunk = 256
    row = lambda bi, i: (bi, i, 0)
    return pl.pallas_call(
        functools.partial(_ssd_in_body, tm=tm, halo=halo, chunk=chunk),
        grid=(b, s // tm),
        in_specs=[
            pl.BlockSpec((None, tm, d), row),
            _const_spec((1, d)),
            _const_spec(w.shape),
            _const_spec(wdt.shape),
            _const_spec(cw.shape),
            _const_spec(cb.shape),
            _const_spec(dtb.shape),
        ],
        out_specs=[
            pl.BlockSpec((None, tm, di), row),
            pl.BlockSpec((None, tm, di), row),
            pl.BlockSpec((None, tm, gn), row),
            pl.BlockSpec((None, tm, gn), row),
            pl.BlockSpec((None, tm, LANES), row),
        ],
        out_shape=[
            jax.ShapeDtypeStruct((b, s, di), BF16),
            jax.ShapeDtypeStruct((b, s, di), BF16),
            jax.ShapeDtypeStruct((b, s, gn), BF16),
            jax.ShapeDtypeStruct((b, s, gn), BF16),
            jax.ShapeDtypeStruct((b, s, LANES), F32),
        ],
        scratch_shapes=[pltpu.VMEM((cdim // LANES, halo + tm, LANES), F32)],
        compiler_params=_params(("arbitrary", "arbitrary")),
        name="ssd_in",
    )(x, nw, w, wdt, cw, cb, dtb)


def _ssd_scan_body(xs_ref, b_ref, c_ref, dt_ref, alog_ref, dskip_ref, eh_ref, y_ref, st_ref,
                   *, n_heads, nc):
    ci = pl.program_id(1)
    L = SSM_CHUNK
    P, N, G = SSM_HEAD_DIM, SSM_STATE, SSM_GROUPS
    hpg = n_heads // G
    gw = hpg * P

    @pl.when(ci == 0)
    def _():
        st_ref[...] = jnp.zeros_like(st_ref)

    lane_h = lax.broadcasted_iota(jnp.int32, (1, LANES), 1)
    a = jnp.where(lane_h < n_heads, -jnp.exp(alog_ref[...]), 0.0)
    row = lax.broadcasted_iota(jnp.int32, (L, L), 0)
    col = lax.broadcasted_iota(jnp.int32, (L, L), 1)
    tril = row >= col
    tril16 = jnp.where(tril, 1.0, 0.0).astype(BF16)
    eh = eh_ref[...]
    head_of_lane = lax.broadcasted_iota(jnp.int32, (L, gw), 1) // P
    head_mask = [jnp.where(head_of_lane == k, 1.0, 0.0).astype(BF16) for k in range(hpg)]

    def prologue(c):
        rs = slice(c * L, (c + 1) * L)
        dt = dt_ref[rs, :]
        dtT = dt.T
        da = dt * a
        da_hi = da.astype(BF16)
        rest = da - da_hi.astype(F32)
        da_mid = rest.astype(BF16)
        da_lo = (rest - da_mid.astype(F32)).astype(BF16)
        parts = _dot(tril16, jnp.concatenate([da_hi, da_mid, da_lo], axis=1))
        acum = parts[:, :LANES] + parts[:, LANES:2 * LANES] + parts[:, 2 * LANES:]
        acumT = acum.T
        a_last = acum[L - 1:L, :]
        sdt_b = _dot((jnp.exp(a_last - acum) * dt).astype(BF16), eh)
        ea_b = _dot(jnp.exp(acum).astype(BF16), eh)
        dl = jnp.broadcast_to(jnp.exp(a_last), (16, LANES))
        dl_hi = dl.astype(BF16)
        dl_lo = (dl - dl_hi.astype(F32)).astype(BF16)
        decay_last = (_dot(dl_hi, eh) + _dot(dl_lo, eh))[0:1, :]
        return dtT, acum, acumT, sdt_b, ea_b, decay_last

    def main(c, pro):
        dtT, acum, acumT, sdt_b, ea_b, decay_last = pro
        rs = slice(c * L, (c + 1) * L)
        xs16 = xs_ref[rs, :]
        xs = xs16.astype(F32)
        xw = (xs * sdt_b).astype(BF16)
        cbs, y_offs = [], []
        for g in range(G):
            bg = b_ref[rs, g * N:(g + 1) * N]
            cg = c_ref[rs, g * N:(g + 1) * N]
            cbs.append(lax.dot_general(cg, bg, (((1,), (1,)), ((), ())), preferred_element_type=F32))
            gs = slice(g * gw, (g + 1) * gw)
            st = st_ref[g]
            y_offs.append(_dot(cg, st.astype(BF16)) * ea_b[:, gs])
            contrib = lax.dot_general(bg, xw[:, gs], (((0,), (0,)), ((), ())),
                                      preferred_element_type=F32)
            st_ref[g] = st * decay_last[:, gs] + contrib

        for g in range(G):
            gs = slice(g * gw, (g + 1) * gw)
            cb, y_off = cbs[g], y_offs[g]
            ws = []
            for k in range(hpg):
                hidx = g * hpg + k
                seg = acum[:, hidx:hidx + 1] - acumT[hidx:hidx + 1, :]
                decay = jnp.exp(jnp.where(tril, seg, NEG_BIG))
                ws.append((cb * decay * dtT[hidx:hidx + 1, :]).astype(BF16))
            xg = xs16[:, gs]
            xblk = jnp.concatenate([xg * head_mask[k] for k in range(hpg)], axis=0)
            y_diag = _dot(jnp.concatenate(ws, axis=1), xblk)
            y_ref[rs, gs] = (y_diag + y_off + dskip_ref[:, gs] * xs[:, gs]).astype(y_ref.dtype)

    pros = [prologue(c) for c in range(nc)]
    for c in range(nc):
        main(c, pros[c])


def _ssd_scan(xs, bm, cm, dt, alog, dskip, eh, *, n_heads, nc=4):
    b, s, di = xs.shape
    gn = bm.shape[-1]
    L = nc * SSM_CHUNK
    row = lambda bi, i: (bi, i, 0)
    return pl.pallas_call(
        functools.partial(_ssd_scan_body, n_heads=n_heads, nc=nc),
        grid=(b, s // L),
        in_specs=[
            pl.BlockSpec((None, L, di), row),
            pl.BlockSpec((None, L, gn), row),
            pl.BlockSpec((None, L, gn), row),
            pl.BlockSpec((None, L, LANES), row),
            _const_spec(alog.shape),
            _const_spec(dskip.shape),
            _const_spec(eh.shape),
        ],
        out_specs=pl.BlockSpec((None, L, di), row),
        out_shape=jax.ShapeDtypeStruct((b, s, di), BF16),
        scratch_shapes=[pltpu.VMEM((SSM_GROUPS, SSM_STATE, di // SSM_GROUPS), F32)],
        compiler_params=_params(("arbitrary", "arbitrary")),
        name="ssd_scan",
    )(xs, bm, cm, dt, alog, dskip, eh)


def _ssd_out_body(y_ref, z_ref, nw_ref, w_ref, x_ref, o_ref, yn_ref):
    di = y_ref.shape[-1]
    gw = di // SSM_GROUPS
    for g in range(SSM_GROUPS):
        gs = slice(g * gw, (g + 1) * gw)
        yg = y_ref[:, gs].astype(F32) * _silu(z_ref[:, gs].astype(F32))
        yn_ref[:, gs] = _rmsnorm(yg, nw_ref[:, gs]).astype(BF16)
    o_ref[...] = x_ref[...] + _dot(yn_ref[...], w_ref[...])


def _ssd_out(y, z, nw, w, x, *, tm=1024):
    m, di = y.shape
    d = w.shape[1]
    return pl.pallas_call(
        _ssd_out_body,
        grid=(m // tm,),
        in_specs=[
            pl.BlockSpec((tm, di), lambda i: (i, 0)),
            pl.BlockSpec((tm, di), lambda i: (i, 0)),
            _const_spec(nw.shape),
            _const_spec(w.shape),
            pl.BlockSpec((tm, d), lambda i: (i, 0)),
        ],
        out_specs=pl.BlockSpec((tm, d), lambda i: (i, 0)),
        out_shape=jax.ShapeDtypeStruct((m, d), F32),
        scratch_shapes=[pltpu.VMEM((tm, di), BF16)],
        compiler_params=_params(("arbitrary",)),
        name="ssd_out",
    )(y, z, nw, w, x)


def _pad_lanes(a, width=LANES):
    return jnp.pad(a, ((0, 0), (0, width - a.shape[-1])))


def _conv_layer(x, nw, w_in, w_dw, w_out, *, layer):
    return _conv_mixer(x, nw[None], w_in, w_dw, w_out, layer=layer)


def _fox_layer(x, nw, w_in, b_f, q_gain, k_gain, w_out, *, t=256):
    b, s, d = x.shape
    n_heads = b_f.shape[0]
    hd = ATTN_HEAD_DIM
    aw = n_heads * hd
    perm = jnp.argsort(b_f)

    def by_head(w):
        return w.reshape(d, n_heads, hd)[:, perm].reshape(d, aw)

    wqkv = jnp.concatenate([by_head(w_in[:, j * aw:(j + 1) * aw]) for j in range(3)],
                           axis=1).astype(BF16)
    wf = _pad_lanes(w_in[:, 3 * aw:][:, perm]).astype(BF16)
    bf = _pad_lanes(b_f[perm][None])
    w_out = w_out.reshape(n_heads, hd, d)[perm].reshape(aw, d)
    qg = jnp.tile(q_gain, n_heads)[None] * (hd ** -0.5 * LOG2E)
    kg = jnp.tile(k_gain, n_heads)[None]
    smax = 1.02 * LOG2E * hd ** 0.5 * jnp.max(jnp.abs(q_gain)) * jnp.max(jnp.abs(k_gain))
    fast = (2.0 * smax <= FAST_PATH_MAX_LOG2).astype(F32)
    thr = jnp.stack([-(SKIP_LOG2 + 2.0 * smax), fast, smax]).astype(F32)
    head_of = jnp.arange(aw) // hd
    e = (head_of[:, None] == jnp.arange(LANES)[None, :]).astype(F32)
    et = jnp.concatenate([e.T, e.T], axis=0).astype(BF16)
    e = (e / hd).astype(BF16)
    q, k, v, ct = _fox_in(x, nw[None], wqkv, wf, bf, qg, kg, e, et, n_heads=n_heads)
    ck = ct.reshape(b, n_heads // 2, 2, s // t, t).transpose(0, 1, 3, 2, 4)
    cs = ct[:, :, 0::t].reshape(b * n_heads, s // t)
    ce = ct[:, :, t - 1::t].reshape(b * n_heads, s // t)
    attn = _fox_attn(thr, cs, ce, q, k, v, ck, t=t, n_heads=n_heads)
    return attn.reshape(b * s, aw), w_out.astype(BF16)


def _ssd_layer(x, nw, w_in, conv_w, conv_b, dt_bias, a_log, d_skip, norm_w, w_out):
    b, s, d = x.shape
    n_heads = a_log.shape[0]
    di = n_heads * SSM_HEAD_DIM
    cdim = conv_w.shape[1]
    wdt = _pad_lanes(w_in[:, di + cdim:]).astype(BF16)
    z, xs, bm, cm, dt = _ssd_in(x, nw[None], w_in[:, :di + cdim].astype(BF16), wdt, conv_w, conv_b[None],
                                _pad_lanes(dt_bias[None]), di=di)
    head_of = jnp.arange(di) // SSM_HEAD_DIM
    eh = (jnp.arange(LANES)[:, None] == head_of[None, :]).astype(BF16)
    dskip = jnp.repeat(d_skip, SSM_HEAD_DIM)[None]
    y = _ssd_scan(xs, bm, cm, dt, _pad_lanes(a_log[None]), dskip, eh, n_heads=n_heads)
    m = b * s
    return _ssd_out(y.reshape(m, di), z.reshape(m, di), norm_w[None], w_out.astype(BF16),
                    x.reshape(m, d)).reshape(b, s, d)


def kernel(x, mix_norm, ffn_norm, ffn_w_gu, ffn_w_down, conv_w_in, conv_w_dw, conv_w_out, fox_w_in, fox_b_f, fox_q_gain, fox_k_gain, fox_w_out, ssd_w_in, ssd_conv_w, ssd_conv_b, ssd_dt_bias, ssd_a_log, ssd_d, ssd_norm_w, ssd_w_out):
    b, s, d = x.shape
    depth = mix_norm.shape[0]
    conv_win16, conv_wout16 = conv_w_in.astype(BF16), conv_w_out.astype(BF16)
    for i in range(depth):
        kind, j = i % 3, i // 3
        proj = None
        if kind == 0:
            x = _conv_layer(x, mix_norm[i], conv_win16, conv_w_dw[j], conv_wout16, layer=j)
        elif kind == 1:
            proj = _fox_layer(x, mix_norm[i], fox_w_in[j], fox_b_f[j], fox_q_gain[j], fox_k_gain[j],
                              fox_w_out[j])
        else:
            x = _ssd_layer(x, mix_norm[i], ssd_w_in[j], ssd_conv_w[j], ssd_conv_b[j],
                           ssd_dt_bias[j], ssd_a_log[j], ssd_d[j], ssd_norm_w[j], ssd_w_out[j])
        x = _ffn(x.reshape(b * s, d), ffn_norm[i][None], ffn_w_gu, ffn_w_down,
                 layer=i, proj=proj).reshape(b, s, d)
    return x
```

```python
import functools

import jax
import jax.numpy as jnp
from jax import lax
from jax.experimental import pallas as pl
from jax.experimental.pallas import tpu as pltpu

F32 = jnp.float32
BF16 = jnp.bfloat16

RMS_EPS = 1e-6
ATTN_HEAD_DIM = 64
SSM_HEAD_DIM = 64
SSM_GROUPS = 8
SSM_STATE = 128
SSM_CHUNK = 128
LANES = 128
NEG_BIG = -1e30
LOG2E = 1.4426950408889634
SKIP_LOG2 = 64.0
FAST_PATH_MAX_LOG2 = 100.0

VMEM_LIMIT = 56 * 1024 * 1024


def _params(sem):
    return pltpu.CompilerParams(dimension_semantics=sem, vmem_limit_bytes=VMEM_LIMIT)


def _const_spec(shape):
    nd = len(shape)
    return pl.BlockSpec(shape, lambda *_: (0,) * nd, pipeline_mode=pl.Buffered(1))


def _layer_spec(shape, layer):
    return pl.BlockSpec((None,) + tuple(shape[1:]), lambda *_: (layer, 0, 0),
                        pipeline_mode=pl.Buffered(1))


def _rmsnorm(x, w):
    return x * lax.rsqrt(jnp.mean(x * x, axis=-1, keepdims=True) + RMS_EPS) * w


def _dot(a, b):
    return jnp.dot(a, b, preferred_element_type=F32)


def _softplus(x):
    return jnp.maximum(x, 0.0) + jnp.log1p(jnp.exp(-jnp.abs(x)))


def _silu(x):
    return x * jax.nn.sigmoid(x)


def _lane_cumsum(x):
    n = x.shape[-1]
    lane = lax.broadcasted_iota(jnp.int32, x.shape, x.ndim - 1)
    s = 1
    while s < n:
        x = x + jnp.where(lane >= s, pltpu.roll(x, s, x.ndim - 1), 0.0)
        s *= 2
    return x


def _ffn_body(*refs, chunk, with_proj):
    if with_proj:
        y_ref, wo_ref, x_ref, nw_ref, wgu_ref, wd_ref, o_ref, a_ref = refs
        x = x_ref[...] + _dot(y_ref[...], wo_ref[...])
    else:
        x_ref, nw_ref, wgu_ref, wd_ref, o_ref, a_ref = refs
        x = x_ref[...]
    h = _rmsnorm(x, nw_ref[...]).astype(BF16)
    dff = wd_ref.shape[0]
    for c0 in range(0, dff, chunk):
        g = _dot(h, wgu_ref[:, c0:c0 + chunk].astype(BF16))
        u = _dot(h, wgu_ref[:, dff + c0:dff + c0 + chunk].astype(BF16))
        a_ref[:, c0:c0 + chunk] = (_silu(g) * u).astype(BF16)
    o_ref[...] = x + _dot(a_ref[...], wd_ref[...].astype(BF16))


def _ffn(x, nw, wgu, wd, *, layer, proj=None, tm=512, chunk=256):
    m, d = x.shape
    dff = wd.shape[1]
    rows = lambda i: (i, 0)
    args, specs = [], []
    if proj is not None:
        y, w_o = proj
        args += [y, w_o]
        specs += [pl.BlockSpec((tm, y.shape[1]), rows), _const_spec(w_o.shape)]
    args += [x, nw, wgu, wd]
    specs += [pl.BlockSpec((tm, d), rows), _const_spec((1, d)), _layer_spec(wgu.shape, layer),
              _layer_spec(wd.shape, layer)]
    return pl.pallas_call(
        functools.partial(_ffn_body, chunk=chunk, with_proj=proj is not None),
        grid=(m // tm,),
        in_specs=specs,
        out_specs=pl.BlockSpec((tm, d), rows),
        out_shape=jax.ShapeDtypeStruct((m, d), F32),
        scratch_shapes=[pltpu.VMEM((tm, dff), BF16)],
        compiler_params=_params(("arbitrary",)),
        name="ffn",
    )(*args)


def _conv_mixer_body(x_ref, nw_ref, win_ref, wdw_ref, wout_ref, o_ref, ext_ref, y_ref,
                     *, tm, halo, chunk):
    i = pl.program_id(1)
    d = x_ref.shape[-1]
    x = x_ref[...]
    h = _rmsnorm(x, nw_ref[...]).astype(BF16)

    @pl.when(i == 0)
    def _():
        ext_ref[:, 0:halo, :] = jnp.zeros((d // LANES, halo, LANES), F32)

    @pl.when(i > 0)
    def _():
        ext_ref[:, 0:halo, :] = ext_ref[:, tm:tm + halo, :]

    kw = wdw_ref.shape[0]
    for q0 in range(0, d, chunk):
        bq = _dot(h, win_ref[:, q0:q0 + chunk].astype(BF16))
        cq = _dot(h, win_ref[:, d + q0:d + q0 + chunk].astype(BF16))
        vq = _dot(h, win_ref[:, 2 * d + q0:2 * d + q0 + chunk].astype(BF16))
        for r0 in range(0, chunk, LANES):
            c0 = q0 + r0
            sl = c0 // LANES
            cs = slice(c0, c0 + LANES)
            rs = slice(r0, r0 + LANES)
            cv = cq[:, rs] * vq[:, rs]
            ext_ref[sl, halo:halo + tm, :] = cv
            u = cv * wdw_ref[kw - 1:kw, cs]
            for k in range(kw - 1):
                off = halo - (kw - 1) + k
                u = u + ext_ref[sl, off:off + tm, :] * wdw_ref[k:k + 1, cs]
            y_ref[:, cs] = (bq[:, rs] * u).astype(BF16)
    o_ref[...] = x + _dot(y_ref[...], wout_ref[...].astype(BF16))


def _conv_mixer(x, nw, win, wdw, wout, *, layer, tm=1024):
    b, s, d = x.shape
    halo = 8
    return pl.pallas_call(
        functools.partial(_conv_mixer_body, tm=tm, halo=halo, chunk=256),
        grid=(b, s // tm),
        in_specs=[
            pl.BlockSpec((None, tm, d), lambda bi, i: (bi, i, 0)),
            _const_spec((1, d)),
            _layer_spec(win.shape, layer),
            _const_spec(wdw.shape),
            _layer_spec(wout.shape, layer),
        ],
        out_specs=pl.BlockSpec((None, tm, d), lambda bi, i: (bi, i, 0)),
        out_shape=jax.ShapeDtypeStruct((b, s, d), F32),
        scratch_shapes=[pltpu.VMEM((d // LANES, halo + tm, LANES), F32),
                        pltpu.VMEM((tm, d), BF16)],
        compiler_params=_params(("arbitrary", "arbitrary")),
        name="conv_mixer",
    )(x, nw, win, wdw, wout)


def _fox_in_body(x_ref, nw_ref, wqkv_ref, wf_ref, bf_ref, qg_ref, kg_ref, e_ref, et_ref,
                 q_ref, k_ref, v_ref, ct_ref, carry_ref, *, n_heads):
    i = pl.program_id(1)
    aw = q_ref.shape[-1]
    x = x_ref[...]
    h = _rmsnorm(x, nw_ref[...]).astype(BF16)
    qkv = _dot(h, wqkv_ref[...])

    def head_norm(t, gain):
        ms = _dot((t * t).astype(BF16), e_ref[...])
        r = lax.rsqrt(ms + RMS_EPS)
        r_hi = r.astype(BF16)
        r_lo = (r - r_hi.astype(F32)).astype(BF16)
        rb = _dot(jnp.concatenate([r_hi, r_lo], axis=1), et_ref[...])
        return t * rb * gain

    q_ref[...] = head_norm(qkv[:, :aw], qg_ref[...]).astype(BF16)
    k_ref[...] = head_norm(qkv[:, aw:2 * aw], kg_ref[...]).astype(BF16)
    v_ref[...] = qkv[:, 2 * aw:].astype(BF16)

    fl = _dot(h, wf_ref[...]) + bf_ref[...]
    logf = -_softplus(-fl) * LOG2E
    local = _lane_cumsum(logf.T[0:n_heads, :])

    @pl.when(i == 0)
    def _():
        carry_ref[...] = jnp.zeros_like(carry_ref)

    cum = local + carry_ref[:, 0:1]
    ct_ref[...] = cum
    tm = cum.shape[-1]
    carry_ref[...] = jnp.broadcast_to(cum[:, tm - 1:tm], carry_ref.shape)


def _fox_in(x, nw, wqkv, wf, bf, qg, kg, e, et, *, n_heads, tm=1024):
    b, s, d = x.shape
    aw = wqkv.shape[1] // 3
    row = lambda bi, i: (bi, i, 0)
    return pl.pallas_call(
        functools.partial(_fox_in_body, n_heads=n_heads),
        grid=(b, s // tm),
        in_specs=[
            pl.BlockSpec((None, tm, d), row),
            _const_spec((1, d)),
            _const_spec(wqkv.shape),
            _const_spec(wf.shape),
            _const_spec(bf.shape),
            _const_spec(qg.shape),
            _const_spec(kg.shape),
            _const_spec(e.shape),
            _const_spec(et.shape),
        ],
        out_specs=[
            pl.BlockSpec((None, tm, aw), row),
            pl.BlockSpec((None, tm, aw), row),
            pl.BlockSpec((None, tm, aw), row),
            pl.BlockSpec((None, n_heads, tm), lambda bi, i: (bi, 0, i)),
        ],
        out_shape=[
            jax.ShapeDtypeStruct((b, s, aw), BF16),
            jax.ShapeDtypeStruct((b, s, aw), BF16),
            jax.ShapeDtypeStruct((b, s, aw), BF16),
            jax.ShapeDtypeStruct((b, n_heads, s), F32),
        ],
        scratch_shapes=[pltpu.VMEM((n_heads, LANES), F32)],
        compiler_params=_params(("arbitrary", "arbitrary")),
        name="fox_in",
    )(x, nw, wqkv, wf, bf, qg, kg, e, et)


def _fox_attn_body(thr_ref, cs_ref, ce_ref, q_ref, k_ref, v_ref, ck_ref, o_ref, p_ref, acc_ref,
                   *, t, n_heads, nsub):
    bi, pr, step = pl.program_id(0), pl.program_id(1), pl.program_id(2)
    hd = ATTN_HEAD_DIM
    lane = lax.broadcasted_iota(jnp.int32, (t, 2 * hd), 1)
    lo = lane < hd
    r0 = bi * n_heads + 2 * pr
    thr = thr_ref[0]
    tiles = [nsub * step + sub for sub in range(nsub)]

    def first_block(i):
        cs0, cs1 = cs_ref[r0, i], cs_ref[r0 + 1, i]

        def needed(j):
            jj = jnp.maximum(j, 0)
            near = (cs0 - ce_ref[r0, jj] >= thr) | (cs1 - ce_ref[r0 + 1, jj] >= thr)
            return (j >= 0) & near

        return lax.while_loop(needed, lambda j: j - 1, i - 1) + 1

    j0s = [first_block(i) for i in tiles]

    def split_heads(sub):
        q2 = q_ref[sub * t:(sub + 1) * t, :]
        zero = jnp.zeros_like(q2)
        return jnp.where(lo, q2, zero), jnp.where(lo, zero, q2)

    def scores(qs, j):
        k2 = k_ref[pl.ds(pl.multiple_of(j * t, t), t), :]
        ck = ck_ref[j]
        return tuple(lax.dot_general(qs[hh], k2, (((1,), (1,)), ((), ())),
                                     preferred_element_type=F32) - ck[hh:hh + 1, :]
                     for hh in range(2))

    def causal(s):
        row = lax.broadcasted_iota(jnp.int32, (t, t), 0)
        col = lax.broadcasted_iota(jnp.int32, (t, t), 1)
        return tuple(jnp.where(row >= col, sh, NEG_BIG) for sh in s)

    @pl.when(thr_ref[1] <= 0.0)
    def _():
        for sub in range(nsub):
            i, j0, qs = tiles[sub], j0s[sub], split_heads(sub)

            def update(j, s, ms, ls, acc):
                v2 = v_ref[pl.ds(pl.multiple_of(j * t, t), t), :]
                new_m, new_l, alphas, pvs = [], [], [], []
                for hh in range(2):
                    m_new = jnp.maximum(ms[hh], jnp.max(s[hh], axis=1, keepdims=True))
                    alpha = jnp.exp2(ms[hh] - m_new)
                    p = jnp.exp2(s[hh] - m_new)
                    new_l.append(alpha * ls[hh] + jnp.sum(p, axis=1, keepdims=True))
                    new_m.append(m_new)
                    alphas.append(alpha)
                    pvs.append(_dot(p.astype(BF16), v2))
                acc = acc * jnp.where(lo, alphas[0], alphas[1]) + jnp.where(lo, pvs[0], pvs[1])
                return tuple(new_m), tuple(new_l), acc

            def body(j, carry, qs=qs, update=update):
                s, ms, ls, acc = carry
                s_next = scores(qs, j + 1)
                ms, ls, acc = update(j, s, ms, ls, acc)
                return s_next, ms, ls, acc

            m0 = jnp.full((t, 1), NEG_BIG, F32)
            l0 = jnp.zeros((t, 1), F32)
            carry = (scores(qs, j0), (m0, m0), (l0, l0), jnp.zeros((t, 2 * hd), F32))
            s, ms, ls, acc = lax.fori_loop(j0, i, body, carry)
            _, ls, acc = update(i, causal(s), ms, ls, acc)
            o_ref[sub * t:(sub + 1) * t, :] = (acc / jnp.where(lo, ls[0], ls[1])).astype(o_ref.dtype)

    @pl.when(thr_ref[1] > 0.0)
    def _():
        e_lo = jnp.where(lo, 1.0, 0.0).astype(BF16)
        e_hi = jnp.where(lo, 0.0, 1.0).astype(BF16)

        def pv(j, p16):
            v2 = v_ref[pl.ds(pl.multiple_of(j * t, t), t), :]
            vb = jnp.concatenate(
                [jnp.concatenate([v2 * e_lo, e_lo], axis=1),
                 jnp.concatenate([v2 * e_hi, e_hi], axis=1)], axis=0)
            return _dot(p16, vb)

        def weights(s, offs):
            return jnp.concatenate([jnp.exp2(s[hh] - offs[hh]).astype(BF16) for hh in range(2)], axis=1)

        qss, offss = [], []
        for sub in range(nsub):
            i, qs = tiles[sub], split_heads(sub)
            ck_i = ck_ref[i]
            offs = tuple(thr_ref[2] - jnp.broadcast_to(ck_i[hh:hh + 1, :], (LANES, t)).T[:, 0:1]
                         for hh in range(2))
            p_ref[sub, 0] = weights(causal(scores(qs, i)), offs)
            acc_ref[sub] = jnp.zeros((t, 4 * hd), F32)
            qss.append(qs)
            offss.append(offs)

        jprevs = []
        for sub in range(nsub):
            i, j0 = tiles[sub], j0s[sub]

            def body(j, jprev, sub=sub, j0=j0):
                slot = (j - j0) & 1
                p_new = weights(scores(qss[sub], j), offss[sub])
                acc_ref[sub] += pv(jprev, p_ref[sub, slot])
                p_ref[sub, 1 - slot] = p_new
                return j

            jprevs.append(lax.fori_loop(j0, i, body, i))

        for sub in range(nsub):
            i, j0 = tiles[sub], j0s[sub]
            acc = acc_ref[sub] + pv(jprevs[sub], p_ref[sub, (i - j0) & 1])
            o_ref[sub * t:(sub + 1) * t, :] = (acc[:, :2 * hd] / acc[:, 2 * hd:]).astype(o_ref.dtype)


def _fox_attn(thr, cs, ce, q, k, v, ck, *, t, n_heads, nsub=8):
    b, s, aw = q.shape
    pairs = aw // LANES
    smem = pl.BlockSpec(memory_space=pltpu.SMEM)
    return pl.pallas_call(
        functools.partial(_fox_attn_body, t=t, n_heads=n_heads, nsub=nsub),
        grid=(b, pairs, s // (nsub * t)),
        in_specs=[
            smem, smem, smem,
            pl.BlockSpec((None, nsub * t, LANES), lambda bi, p, i: (bi, i, p)),
            pl.BlockSpec((None, s, LANES), lambda bi, p, i: (bi, 0, p)),
            pl.BlockSpec((None, s, LANES), lambda bi, p, i: (bi, 0, p)),
            pl.BlockSpec((None, None, s // t, 2, t), lambda bi, p, i: (bi, p, 0, 0, 0)),
        ],
        out_specs=pl.BlockSpec((None, nsub * t, LANES), lambda bi, p, i: (bi, i, p)),
        out_shape=jax.ShapeDtypeStruct((b, s, aw), BF16),
        scratch_shapes=[pltpu.VMEM((nsub, 2, t, 2 * t), BF16), pltpu.VMEM((nsub, t, 2 * LANES), F32)],
        compiler_params=_params(("arbitrary", "arbitrary", "arbitrary")),
        name="fox_attn",
    )(thr, cs, ce, q, k, v, ck)


def _ssd_in_body(x_ref, nw_ref, w_ref, wdt_ref, cw_ref, cb_ref, dtb_ref,
                 z_ref, xs_ref, b_ref, c_ref, dt_ref, ext_ref, *, tm, halo, chunk):
    i = pl.program_id(1)
    di = xs_ref.shape[-1]
    gn = b_ref.shape[-1]
    cdim = cw_ref.shape[-1]
    x = x_ref[...]
    h = _rmsnorm(x, nw_ref[...]).astype(BF16)

    @pl.when(i == 0)
    def _():
        ext_ref[:, 0:halo, :] = jnp.zeros((cdim // LANES, halo, LANES), F32)

    @pl.when(i > 0)
    def _():
        ext_ref[:, 0:halo, :] = ext_ref[:, tm:tm + halo, :]

    dt_ref[...] = _softplus(_dot(h, wdt_ref[...]) + dtb_ref[...])
    kw = cw_ref.shape[0]
    outs = ((xs_ref, 0, di), (b_ref, di, di + gn), (c_ref, di + gn, cdim))
    for o_ref, lo_c, hi_c in outs:
        for q0 in range(lo_c, hi_c, chunk):
            raw = _dot(h, w_ref[:, di + q0:di + q0 + chunk])
            if q0 < di:
                z_ref[:, q0:q0 + chunk] = _dot(h, w_ref[:, q0:q0 + chunk]).astype(z_ref.dtype)
            for r0 in range(0, chunk, LANES):
                c0 = q0 + r0
                sl = c0 // LANES
                cs = slice(c0, c0 + LANES)
                cur = raw[:, r0:r0 + LANES]
                ext_ref[sl, halo:halo + tm, :] = cur
                u = cur * cw_ref[kw - 1:kw, cs] + cb_ref[:, cs]
                for k in range(kw - 1):
                    off = halo - (kw - 1) + k
                    u = u + ext_ref[sl, off:off + tm, :] * cw_ref[k:k + 1, cs]
                o_ref[:, c0 - lo_c:c0 - lo_c + LANES] = _silu(u).astype(o_ref.dtype)


def _ssd_in(x, nw, w, wdt, cw, cb, dtb, *, di, tm=512):
    b, s, d = x.shape
    cdim = cw.shape[1]
    gn = (cdim - di) // 2
    halo = 8
    chunk = 512
    row = lambda bi, i: (bi, i, 0)
    return pl.pallas_call(
        functools.partial(_ssd_in_body, tm=tm, halo=halo, chunk=chunk),
        grid=(b, s // tm),
        in_specs=[
            pl.BlockSpec((None, tm, d), row),
            _const_spec((1, d)),
            _const_spec(w.shape),
            _const_spec(wdt.shape),
            _const_spec(cw.shape),
            _const_spec(cb.shape),
            _const_spec(dtb.shape),
        ],
        out_specs=[
            pl.BlockSpec((None, tm, di), row),
            pl.BlockSpec((None, tm, di), row),
            pl.BlockSpec((None, tm, gn), row),
            pl.BlockSpec((None, tm, gn), row),
            pl.BlockSpec((None, tm, LANES), row),
        ],
        out_shape=[
            jax.ShapeDtypeStruct((b, s, di), BF16),
            jax.ShapeDtypeStruct((b, s, di), BF16),
            jax.ShapeDtypeStruct((b, s, gn), BF16),
            jax.ShapeDtypeStruct((b, s, gn), BF16),
            jax.ShapeDtypeStruct((b, s, LANES), F32),
        ],
        scratch_shapes=[pltpu.VMEM((cdim // LANES, halo + tm, LANES), F32)],
        compiler_params=_params(("arbitrary", "arbitrary")),
        name="ssd_in",
    )(x, nw, w, wdt, cw, cb, dtb)


def _ssd_scan_body(xs_ref, b_ref, c_ref, dt_ref, alog_ref, dskip_ref, eh_ref, y_ref, st_ref,
                   *, n_heads, nc):
    ci = pl.program_id(1)
    L = SSM_CHUNK
    P, N, G = SSM_HEAD_DIM, SSM_STATE, SSM_GROUPS
    hpg = n_heads // G
    gw = hpg * P

    @pl.when(ci == 0)
    def _():
        st_ref[...] = jnp.zeros_like(st_ref)

    lane_h = lax.broadcasted_iota(jnp.int32, (1, LANES), 1)
    a = jnp.where(lane_h < n_heads, -jnp.exp(alog_ref[...]), 0.0)
    row = lax.broadcasted_iota(jnp.int32, (L, L), 0)
    col = lax.broadcasted_iota(jnp.int32, (L, L), 1)
    tril = row >= col
    tril16 = jnp.where(tril, 1.0, 0.0).astype(BF16)
    eh = eh_ref[...]
    head_of_lane = lax.broadcasted_iota(jnp.int32, (L, gw), 1) // P
    head_mask = [jnp.where(head_of_lane == k, 1.0, 0.0).astype(BF16) for k in range(hpg)]

    def prologue(c):
        rs = slice(c * L, (c + 1) * L)
        dt = dt_ref[rs, :]
        dtT = dt.T
        da = dt * a
        da_hi = da.astype(BF16)
        rest = da - da_hi.astype(F32)
        da_mid = rest.astype(BF16)
        da_lo = (rest - da_mid.astype(F32)).astype(BF16)
        parts = _dot(tril16, jnp.concatenate([da_hi, da_mid, da_lo], axis=1))
        acum = parts[:, :LANES] + parts[:, LANES:2 * LANES] + parts[:, 2 * LANES:]
        acumT = acum.T
        a_last = acum[L - 1:L, :]
        sdt_b = _dot((jnp.exp(a_last - acum) * dt).astype(BF16), eh)
        ea_b = _dot(jnp.exp(acum).astype(BF16), eh)
        dl = jnp.broadcast_to(jnp.exp(a_last), (16, LANES))
        dl_hi = dl.astype(BF16)
        dl_lo = (dl - dl_hi.astype(F32)).astype(BF16)
        decay_last = (_dot(dl_hi, eh) + _dot(dl_lo, eh))[0:1, :]
        return dtT, acum, acumT, sdt_b, ea_b, decay_last

    def main(c, pro):
        dtT, acum, acumT, sdt_b, ea_b, decay_last = pro
        rs = slice(c * L, (c + 1) * L)
        xs16 = xs_ref[rs, :]
        xs = xs16.astype(F32)
        xw = (xs * sdt_b).astype(BF16)
        cbs, y_offs = [], []
        for g in range(G):
            bg = b_ref[rs, g * N:(g + 1) * N]
            cg = c_ref[rs, g * N:(g + 1) * N]
            cbs.append(lax.dot_general(cg, bg, (((1,), (1,)), ((), ())), preferred_element_type=F32))
            gs = slice(g * gw, (g + 1) * gw)
            st = st_ref[g]
            y_offs.append(_dot(cg, st.astype(BF16)) * ea_b[:, gs])
            contrib = lax.dot_general(bg, xw[:, gs], (((0,), (0,)), ((), ())),
                                      preferred_element_type=F32)
            st_ref[g] = st * decay_last[:, gs] + contrib

        for g in range(G):
            gs = slice(g * gw, (g + 1) * gw)
            cb, y_off = cbs[g], y_offs[g]
            ws = []
            for k in range(hpg):
                hidx = g * hpg + k
                seg = acum[:, hidx:hidx + 1] - acumT[hidx:hidx + 1, :]
                decay = jnp.exp(jnp.where(tril, seg, NEG_BIG))
                ws.append((cb * decay * dtT[hidx:hidx + 1, :]).astype(BF16))
            xg = xs16[:, gs]
            xblk = jnp.concatenate([xg * head_mask[k] for k in range(hpg)], axis=0)
            y_diag = _dot(jnp.concatenate(ws, axis=1), xblk)
            y_ref[rs, gs] = (y_diag + y_off + dskip_ref[:, gs] * xs[:, gs]).astype(y_ref.dtype)

    pros = [prologue(c) for c in range(nc)]
    for c in range(nc):
        main(c, pros[c])


def _ssd_scan(xs, bm, cm, dt, alog, dskip, eh, *, n_heads, nc=4):
    b, s, di = xs.shape
    gn = bm.shape[-1]
    L = nc * SSM_CHUNK
    row = lambda bi, i: (bi, i, 0)
    return pl.pallas_call(
        functools.partial(_ssd_scan_body, n_heads=n_heads, nc=nc),
        grid=(b, s // L),
        in_specs=[
            pl.BlockSpec((None, L, di), row),
            pl.BlockSpec((None, L, gn), row),
            pl.BlockSpec((None, L, gn), row),
            pl.BlockSpec((None, L, LANES), row),
            _const_spec(alog.shape),
            _const_spec(dskip.shape),
            _const_spec(eh.shape),
        ],
        out_specs=pl.BlockSpec((None, L, di), row),
        out_shape=jax.ShapeDtypeStruct((b, s, di), BF16),
        scratch_shapes=[pltpu.VMEM((SSM_GROUPS, SSM_STATE, di // SSM_GROUPS), F32)],
        compiler_params=_params(("arbitrary", "arbitrary")),
        name="ssd_scan",
    )(xs, bm, cm, dt, alog, dskip, eh)


def _ssd_out_body(y_ref, z_ref, nw_ref, w_ref, x_ref, o_ref, yn_ref):
    di = y_ref.shape[-1]
    gw = di // SSM_GROUPS
    for g in range(SSM_GROUPS):
        gs = slice(g * gw, (g + 1) * gw)
        yg = y_ref[:, gs].astype(F32) * _silu(z_ref[:, gs].astype(F32))
        yn_ref[:, gs] = _rmsnorm(yg, nw_ref[:, gs]).astype(BF16)
    o_ref[...] = x_ref[...] + _dot(yn_ref[...], w_ref[...])


def _ssd_out(y, z, nw, w, x, *, tm=1024):
    m, di = y.shape
    d = w.shape[1]
    return pl.pallas_call(
        _ssd_out_body,
        grid=(m // tm,),
        in_specs=[
            pl.BlockSpec((tm, di), lambda i: (i, 0)),
            pl.BlockSpec((tm, di), lambda i: (i, 0)),
            _const_spec(nw.shape),
            _const_spec(w.shape),
            pl.BlockSpec((tm, d), lambda i: (i, 0)),
        ],
        out_specs=pl.BlockSpec((tm, d), lambda i: (i, 0)),
        out_shape=jax.ShapeDtypeStruct((m, d), F32),
        scratch_shapes=[pltpu.VMEM((tm, di), BF16)],
        compiler_params=_params(("arbitrary",)),
        name="ssd_out",
    )(y, z, nw, w, x)


def _pad_lanes(a, width=LANES):
    return jnp.pad(a, ((0, 0), (0, width - a.shape[-1])))


def _conv_layer(x, nw, w_in, w_dw, w_out, *, layer):
    return _conv_mixer(x, nw[None], w_in, w_dw, w_out, layer=layer)


def _fox_layer(x, nw, w_in, b_f, q_gain, k_gain, w_out, *, t=256):
    b, s, d = x.shape
    n_heads = b_f.shape[0]
    hd = ATTN_HEAD_DIM
    aw = n_heads * hd
    perm = jnp.argsort(b_f)

    def by_head(w):
        return w.reshape(d, n_heads, hd)[:, perm].reshape(d, aw)

    wqkv = jnp.concatenate([by_head(w_in[:, j * aw:(j + 1) * aw]) for j in range(3)],
                           axis=1).astype(BF16)
    wf = _pad_lanes(w_in[:, 3 * aw:][:, perm]).astype(BF16)
    bf = _pad_lanes(b_f[perm][None])
    w_out = w_out.reshape(n_heads, hd, d)[perm].reshape(aw, d)
    qg = jnp.tile(q_gain, n_heads)[None] * (hd ** -0.5 * LOG2E)
    kg = jnp.tile(k_gain, n_heads)[None]
    smax = 1.02 * LOG2E * hd ** 0.5 * jnp.max(jnp.abs(q_gain)) * jnp.max(jnp.abs(k_gain))
    fast = (2.0 * smax <= FAST_PATH_MAX_LOG2).astype(F32)
    thr = jnp.stack([-(SKIP_LOG2 + 2.0 * smax), fast, smax]).astype(F32)
    head_of = jnp.arange(aw) // hd
    e = (head_of[:, None] == jnp.arange(LANES)[None, :]).astype(F32)
    et = jnp.concatenate([e.T, e.T], axis=0).astype(BF16)
    e = (e / hd).astype(BF16)
    q, k, v, ct = _fox_in(x, nw[None], wqkv, wf, bf, qg, kg, e, et, n_heads=n_heads)
    ck = ct.reshape(b, n_heads // 2, 2, s // t, t).transpose(0, 1, 3, 2, 4)
    cs = ct[:, :, 0::t].reshape(b * n_heads, s // t)
    ce = ct[:, :, t - 1::t].reshape(b * n_heads, s // t)
    attn = _fox_attn(thr, cs, ce, q, k, v, ck, t=t, n_heads=n_heads)
    return attn.reshape(b * s, aw), w_out.astype(BF16)


def _ssd_layer(x, nw, w_in, conv_w, conv_b, dt_bias, a_log, d_skip, norm_w, w_out):
    b, s, d = x.shape
    n_heads = a_log.shape[0]
    di = n_heads * SSM_HEAD_DIM
    cdim = conv_w.shape[1]
    wdt = _pad_lanes(w_in[:, di + cdim:]).astype(BF16)
    z, xs, bm, cm, dt = _ssd_in(x, nw[None], w_in.astype(BF16), wdt, conv_w, conv_b[None],
                                _pad_lanes(dt_bias[None]), di=di)
    head_of = jnp.arange(di) // SSM_HEAD_DIM
    eh = (jnp.arange(LANES)[:, None] == head_of[None, :]).astype(BF16)
    dskip = jnp.repeat(d_skip, SSM_HEAD_DIM)[None]
    y = _ssd_scan(xs, bm, cm, dt, _pad_lanes(a_log[None]), dskip, eh, n_heads=n_heads)
    m = b * s
    return _ssd_out(y.reshape(m, di), z.reshape(m, di), norm_w[None], w_out.astype(BF16),
                    x.reshape(m, d)).reshape(b, s, d)


def kernel(x, mix_norm, ffn_norm, ffn_w_gu, ffn_w_down, conv_w_in, conv_w_dw, conv_w_out, fox_w_in, fox_b_f, fox_q_gain, fox_k_gain, fox_w_out, ssd_w_in, ssd_conv_w, ssd_conv_b, ssd_dt_bias, ssd_a_log, ssd_d, ssd_norm_w, ssd_w_out):
    b, s, d = x.shape
    depth = mix_norm.shape[0]
    for i in range(depth):
        kind, j = i % 3, i // 3
        proj = None
        if kind == 0:
            x = _conv_layer(x, mix_norm[i], conv_w_in, conv_w_dw[j], conv_w_out, layer=j)
        elif kind == 1:
            proj = _fox_layer(x, mix_norm[i], fox_w_in[j], fox_b_f[j], fox_q_gain[j], fox_k_gain[j],
                              fox_w_out[j])
        else:
            x = _ssd_layer(x, mix_norm[i], ssd_w_in[j], ssd_conv_w[j], ssd_conv_b[j],
                           ssd_dt_bias[j], ssd_a_log[j], ssd_d[j], ssd_norm_w[j], ssd_w_out[j])
        x = _ffn(x.reshape(b * s, d), ffn_norm[i][None], ffn_w_gu, ffn_w_down,
                 layer=i, proj=proj).reshape(b, s, d)
    return x
```

```python
import functools

import jax
import jax.numpy as jnp
from jax import lax
from jax.experimental import pallas as pl
from jax.experimental.pallas import tpu as pltpu

F32 = jnp.float32
BF16 = jnp.bfloat16

RMS_EPS = 1e-6
ATTN_HEAD_DIM = 64
SSM_HEAD_DIM = 64
SSM_GROUPS = 8
SSM_STATE = 128
SSM_CHUNK = 128
LANES = 128
NEG_BIG = -1e30
LOG2E = 1.4426950408889634
SKIP_LOG2 = 64.0
FAST_PATH_MAX_LOG2 = 100.0

VMEM_LIMIT = 56 * 1024 * 1024


def _params(sem):
    return pltpu.CompilerParams(dimension_semantics=sem, vmem_limit_bytes=VMEM_LIMIT)


def _const_spec(shape):
    nd = len(shape)
    return pl.BlockSpec(shape, lambda *_: (0,) * nd, pipeline_mode=pl.Buffered(1))


def _layer_spec(shape, layer):
    return pl.BlockSpec((None,) + tuple(shape[1:]), lambda *_: (layer, 0, 0),
                        pipeline_mode=pl.Buffered(1))


def _rmsnorm(x, w):
    return x * lax.rsqrt(jnp.mean(x * x, axis=-1, keepdims=True) + RMS_EPS) * w


def _dot(a, b):
    return jnp.dot(a, b, preferred_element_type=F32)


def _softplus(x):
    return jnp.maximum(x, 0.0) + jnp.log1p(jnp.exp(-jnp.abs(x)))


def _silu(x):
    return x * jax.nn.sigmoid(x)


def _lane_cumsum(x):
    n = x.shape[-1]
    lane = lax.broadcasted_iota(jnp.int32, x.shape, x.ndim - 1)
    s = 1
    while s < n:
        x = x + jnp.where(lane >= s, pltpu.roll(x, s, x.ndim - 1), 0.0)
        s *= 2
    return x


def _ffn_body(*refs, chunk, with_proj):
    if with_proj:
        y_ref, wo_ref, x_ref, nw_ref, wgu_ref, wd_ref, o_ref, a_ref = refs
        x = x_ref[...] + _dot(y_ref[...], wo_ref[...])
    else:
        x_ref, nw_ref, wgu_ref, wd_ref, o_ref, a_ref = refs
        x = x_ref[...]
    h = _rmsnorm(x, nw_ref[...]).astype(BF16)
    dff = wd_ref.shape[0]
    for c0 in range(0, dff, chunk):
        g = _dot(h, wgu_ref[:, c0:c0 + chunk].astype(BF16))
        u = _dot(h, wgu_ref[:, dff + c0:dff + c0 + chunk].astype(BF16))
        a_ref[:, c0:c0 + chunk] = (_silu(g) * u).astype(BF16)
    o_ref[...] = x + _dot(a_ref[...], wd_ref[...].astype(BF16))


def _ffn(x, nw, wgu, wd, *, layer, proj=None, tm=512, chunk=256):
    m, d = x.shape
    dff = wd.shape[1]
    rows = lambda i: (i, 0)
    args, specs = [], []
    if proj is not None:
        y, w_o = proj
        args += [y, w_o]
        specs += [pl.BlockSpec((tm, y.shape[1]), rows), _const_spec(w_o.shape)]
    args += [x, nw, wgu, wd]
    specs += [pl.BlockSpec((tm, d), rows), _const_spec((1, d)), _layer_spec(wgu.shape, layer),
              _layer_spec(wd.shape, layer)]
    return pl.pallas_call(
        functools.partial(_ffn_body, chunk=chunk, with_proj=proj is not None),
        grid=(m // tm,),
        in_specs=specs,
        out_specs=pl.BlockSpec((tm, d), rows),
        out_shape=jax.ShapeDtypeStruct((m, d), F32),
        scratch_shapes=[pltpu.VMEM((tm, dff), BF16)],
        compiler_params=_params(("arbitrary",)),
        name="ffn",
    )(*args)


def _conv_mixer_body(x_ref, nw_ref, win_ref, wdw_ref, wout_ref, o_ref, ext_ref, y_ref,
                     *, tm, halo, chunk):
    i = pl.program_id(1)
    d = x_ref.shape[-1]
    x = x_ref[...]
    h = _rmsnorm(x, nw_ref[...]).astype(BF16)

    @pl.when(i == 0)
    def _():
        ext_ref[:, 0:halo, :] = jnp.zeros((d // LANES, halo, LANES), F32)

    @pl.when(i > 0)
    def _():
        ext_ref[:, 0:halo, :] = ext_ref[:, tm:tm + halo, :]

    kw = wdw_ref.shape[0]
    for q0 in range(0, d, chunk):
        bq = _dot(h, win_ref[:, q0:q0 + chunk].astype(BF16))
        cq = _dot(h, win_ref[:, d + q0:d + q0 + chunk].astype(BF16))
        vq = _dot(h, win_ref[:, 2 * d + q0:2 * d + q0 + chunk].astype(BF16))
        for r0 in range(0, chunk, LANES):
            c0 = q0 + r0
            sl = c0 // LANES
            cs = slice(c0, c0 + LANES)
            rs = slice(r0, r0 + LANES)
            cv = cq[:, rs] * vq[:, rs]
            ext_ref[sl, halo:halo + tm, :] = cv
            u = cv * wdw_ref[kw - 1:kw, cs]
            for k in range(kw - 1):
                off = halo - (kw - 1) + k
                u = u + ext_ref[sl, off:off + tm, :] * wdw_ref[k:k + 1, cs]
            y_ref[:, cs] = (bq[:, rs] * u).astype(BF16)
    o_ref[...] = x + _dot(y_ref[...], wout_ref[...].astype(BF16))


def _conv_mixer(x, nw, win, wdw, wout, *, layer, tm=1024):
    b, s, d = x.shape
    halo = 8
    return pl.pallas_call(
        functools.partial(_conv_mixer_body, tm=tm, halo=halo, chunk=256),
        grid=(b, s // tm),
        in_specs=[
            pl.BlockSpec((None, tm, d), lambda bi, i: (bi, i, 0)),
            _const_spec((1, d)),
            _layer_spec(win.shape, layer),
            _const_spec(wdw.shape),
            _layer_spec(wout.shape, layer),
        ],
        out_specs=pl.BlockSpec((None, tm, d), lambda bi, i: (bi, i, 0)),
        out_shape=jax.ShapeDtypeStruct((b, s, d), F32),
        scratch_shapes=[pltpu.VMEM((d // LANES, halo + tm, LANES), F32),
                        pltpu.VMEM((tm, d), BF16)],
        compiler_params=_params(("arbitrary", "arbitrary")),
        name="conv_mixer",
    )(x, nw, win, wdw, wout)


def _fox_in_body(x_ref, nw_ref, wqkv_ref, wf_ref, bf_ref, qg_ref, kg_ref, e_ref, et_ref,
                 q_ref, k_ref, v_ref, ct_ref, carry_ref, *, n_heads):
    i = pl.program_id(1)
    aw = q_ref.shape[-1]
    x = x_ref[...]
    h = _rmsnorm(x, nw_ref[...]).astype(BF16)
    qkv = _dot(h, wqkv_ref[...])

    def head_norm(t, gain):
        ms = _dot((t * t).astype(BF16), e_ref[...])
        r = lax.rsqrt(ms + RMS_EPS)
        r_hi = r.astype(BF16)
        r_lo = (r - r_hi.astype(F32)).astype(BF16)
        rb = _dot(jnp.concatenate([r_hi, r_lo], axis=1), et_ref[...])
        return t * rb * gain

    q_ref[...] = head_norm(qkv[:, :aw], qg_ref[...]).astype(BF16)
    k_ref[...] = head_norm(qkv[:, aw:2 * aw], kg_ref[...]).astype(BF16)
    v_ref[...] = qkv[:, 2 * aw:].astype(BF16)

    fl = _dot(h, wf_ref[...]) + bf_ref[...]
    logf = -_softplus(-fl) * LOG2E
    local = _lane_cumsum(logf.T[0:n_heads, :])

    @pl.when(i == 0)
    def _():
        carry_ref[...] = jnp.zeros_like(carry_ref)

    cum = local + carry_ref[:, 0:1]
    ct_ref[...] = cum
    tm = cum.shape[-1]
    carry_ref[...] = jnp.broadcast_to(cum[:, tm - 1:tm], carry_ref.shape)


def _fox_in(x, nw, wqkv, wf, bf, qg, kg, e, et, *, n_heads, tm=1024):
    b, s, d = x.shape
    aw = wqkv.shape[1] // 3
    row = lambda bi, i: (bi, i, 0)
    return pl.pallas_call(
        functools.partial(_fox_in_body, n_heads=n_heads),
        grid=(b, s // tm),
        in_specs=[
            pl.BlockSpec((None, tm, d), row),
            _const_spec((1, d)),
            _const_spec(wqkv.shape),
            _const_spec(wf.shape),
            _const_spec(bf.shape),
            _const_spec(qg.shape),
            _const_spec(kg.shape),
            _const_spec(e.shape),
            _const_spec(et.shape),
        ],
        out_specs=[
            pl.BlockSpec((None, tm, aw), row),
            pl.BlockSpec((None, tm, aw), row),
            pl.BlockSpec((None, tm, aw), row),
            pl.BlockSpec((None, n_heads, tm), lambda bi, i: (bi, 0, i)),
        ],
        out_shape=[
            jax.ShapeDtypeStruct((b, s, aw), BF16),
            jax.ShapeDtypeStruct((b, s, aw), BF16),
            jax.ShapeDtypeStruct((b, s, aw), BF16),
            jax.ShapeDtypeStruct((b, n_heads, s), F32),
        ],
        scratch_shapes=[pltpu.VMEM((n_heads, LANES), F32)],
        compiler_params=_params(("arbitrary", "arbitrary")),
        name="fox_in",
    )(x, nw, wqkv, wf, bf, qg, kg, e, et)


def _fox_attn_body(thr_ref, cs_ref, ce_ref, q_ref, k_ref, v_ref, ck_ref, o_ref, p_ref, acc_ref,
                   *, t, n_heads, nsub):
    bi, pr, step = pl.program_id(0), pl.program_id(1), pl.program_id(2)
    hd = ATTN_HEAD_DIM
    lane = lax.broadcasted_iota(jnp.int32, (t, 2 * hd), 1)
    lo = lane < hd
    r0 = bi * n_heads + 2 * pr
    thr = thr_ref[0]
    tiles = [nsub * step + sub for sub in range(nsub)]

    def first_block(i):
        cs0, cs1 = cs_ref[r0, i], cs_ref[r0 + 1, i]

        def needed(j):
            jj = jnp.maximum(j, 0)
            near = (cs0 - ce_ref[r0, jj] >= thr) | (cs1 - ce_ref[r0 + 1, jj] >= thr)
            return (j >= 0) & near

        return lax.while_loop(needed, lambda j: j - 1, i - 1) + 1

    j0s = [first_block(i) for i in tiles]

    def split_heads(sub):
        q2 = q_ref[sub * t:(sub + 1) * t, :]
        zero = jnp.zeros_like(q2)
        return jnp.where(lo, q2, zero), jnp.where(lo, zero, q2)

    def scores(qs, j):
        k2 = k_ref[pl.ds(pl.multiple_of(j * t, t), t), :]
        ck = ck_ref[j]
        return tuple(lax.dot_general(qs[hh], k2, (((1,), (1,)), ((), ())),
                                     preferred_element_type=F32) - ck[hh:hh + 1, :]
                     for hh in range(2))

    def causal(s):
        row = lax.broadcasted_iota(jnp.int32, (t, t), 0)
        col = lax.broadcasted_iota(jnp.int32, (t, t), 1)
        return tuple(jnp.where(row >= col, sh, NEG_BIG) for sh in s)

    @pl.when(thr_ref[1] <= 0.0)
    def _():
        for sub in range(nsub):
            i, j0, qs = tiles[sub], j0s[sub], split_heads(sub)

            def update(j, s, ms, ls, acc):
                v2 = v_ref[pl.ds(pl.multiple_of(j * t, t), t), :]
                new_m, new_l, alphas, pvs = [], [], [], []
                for hh in range(2):
                    m_new = jnp.maximum(ms[hh], jnp.max(s[hh], axis=1, keepdims=True))
                    alpha = jnp.exp2(ms[hh] - m_new)
                    p = jnp.exp2(s[hh] - m_new)
                    new_l.append(alpha * ls[hh] + jnp.sum(p, axis=1, keepdims=True))
                    new_m.append(m_new)
                    alphas.append(alpha)
                    pvs.append(_dot(p.astype(BF16), v2))
                acc = acc * jnp.where(lo, alphas[0], alphas[1]) + jnp.where(lo, pvs[0], pvs[1])
                return tuple(new_m), tuple(new_l), acc

            def body(j, carry, qs=qs, update=update):
                s, ms, ls, acc = carry
                s_next = scores(qs, j + 1)
                ms, ls, acc = update(j, s, ms, ls, acc)
                return s_next, ms, ls, acc

            m0 = jnp.full((t, 1), NEG_BIG, F32)
            l0 = jnp.zeros((t, 1), F32)
            carry = (scores(qs, j0), (m0, m0), (l0, l0), jnp.zeros((t, 2 * hd), F32))
            s, ms, ls, acc = lax.fori_loop(j0, i, body, carry)
            _, ls, acc = update(i, causal(s), ms, ls, acc)
            o_ref[sub * t:(sub + 1) * t, :] = (acc / jnp.where(lo, ls[0], ls[1])).astype(o_ref.dtype)

    @pl.when(thr_ref[1] > 0.0)
    def _():
        e_lo = jnp.where(lo, 1.0, 0.0).astype(BF16)
        e_hi = jnp.where(lo, 0.0, 1.0).astype(BF16)

        def pv(j, p16):
            v2 = v_ref[pl.ds(pl.multiple_of(j * t, t), t), :]
            vb = jnp.concatenate(
                [jnp.concatenate([v2 * e_lo, e_lo], axis=1),
                 jnp.concatenate([v2 * e_hi, e_hi], axis=1)], axis=0)
            return _dot(p16, vb)

        def weights(s, offs):
            return jnp.concatenate([jnp.exp2(s[hh] - offs[hh]).astype(BF16) for hh in range(2)], axis=1)

        qss, offss = [], []
        for sub in range(nsub):
            i, qs = tiles[sub], split_heads(sub)
            ck_i = ck_ref[i]
            offs = tuple(thr_ref[2] - jnp.broadcast_to(ck_i[hh:hh + 1, :], (LANES, t)).T[:, 0:1]
                         for hh in range(2))
            p_ref[sub, 0] = weights(causal(scores(qs, i)), offs)
            acc_ref[sub] = jnp.zeros((t, 4 * hd), F32)
            qss.append(qs)
            offss.append(offs)

        jprevs = []
        for sub in range(nsub):
            i, j0 = tiles[sub], j0s[sub]

            def body(j, jprev, sub=sub, j0=j0):
                slot = (j - j0) & 1
                p_new = weights(scores(qss[sub], j), offss[sub])
                acc_ref[sub] += pv(jprev, p_ref[sub, slot])
                p_ref[sub, 1 - slot] = p_new
                return j

            jprevs.append(lax.fori_loop(j0, i, body, i))

        for sub in range(nsub):
            i, j0 = tiles[sub], j0s[sub]
            acc = acc_ref[sub] + pv(jprevs[sub], p_ref[sub, (i - j0) & 1])
            o_ref[sub * t:(sub + 1) * t, :] = (acc[:, :2 * hd] / acc[:, 2 * hd:]).astype(o_ref.dtype)


def _fox_attn(thr, cs, ce, q, k, v, ck, *, t, n_heads, nsub=8):
    b, s, aw = q.shape
    pairs = aw // LANES
    smem = pl.BlockSpec(memory_space=pltpu.SMEM)
    return pl.pallas_call(
        functools.partial(_fox_attn_body, t=t, n_heads=n_heads, nsub=nsub),
        grid=(b, pairs, s // (nsub * t)),
        in_specs=[
            smem, smem, smem,
            pl.BlockSpec((None, nsub * t, LANES), lambda bi, p, i: (bi, i, p)),
            pl.BlockSpec((None, s, LANES), lambda bi, p, i: (bi, 0, p)),
            pl.BlockSpec((None, s, LANES), lambda bi, p, i: (bi, 0, p)),
            pl.BlockSpec((None, None, s // t, 2, t), lambda bi, p, i: (bi, p, 0, 0, 0)),
        ],
        out_specs=pl.BlockSpec((None, nsub * t, LANES), lambda bi, p, i: (bi, i, p)),
        out_shape=jax.ShapeDtypeStruct((b, s, aw), BF16),
        scratch_shapes=[pltpu.VMEM((nsub, 2, t, 2 * t), BF16), pltpu.VMEM((nsub, t, 2 * LANES), F32)],
        compiler_params=_params(("arbitrary", "arbitrary", "arbitrary")),
        name="fox_attn",
    )(thr, cs, ce, q, k, v, ck)


def _ssd_in_body(x_ref, nw_ref, w_ref, wdt_ref, cw_ref, cb_ref, dtb_ref,
                 z_ref, xs_ref, b_ref, c_ref, dt_ref, ext_ref, *, tm, halo, chunk):
    i = pl.program_id(1)
    di = xs_ref.shape[-1]
    gn = b_ref.shape[-1]
    cdim = cw_ref.shape[-1]
    x = x_ref[...]
    h = _rmsnorm(x, nw_ref[...]).astype(BF16)

    @pl.when(i == 0)
    def _():
        ext_ref[:, 0:halo, :] = jnp.zeros((cdim // LANES, halo, LANES), F32)

    @pl.when(i > 0)
    def _():
        ext_ref[:, 0:halo, :] = ext_ref[:, tm:tm + halo, :]

    dt_ref[...] = _softplus(_dot(h, wdt_ref[...]) + dtb_ref[...])
    kw = cw_ref.shape[0]
    outs = ((xs_ref, 0, di), (b_ref, di, di + gn), (c_ref, di + gn, cdim))
    for o_ref, lo_c, hi_c in outs:
        for q0 in range(lo_c, hi_c, chunk):
            raw = _dot(h, w_ref[:, di + q0:di + q0 + chunk].astype(BF16))
            if q0 < di:
                z_ref[:, q0:q0 + chunk] = _dot(h, w_ref[:, q0:q0 + chunk].astype(BF16)).astype(z_ref.dtype)
            for r0 in range(0, chunk, LANES):
                c0 = q0 + r0
                sl = c0 // LANES
                cs = slice(c0, c0 + LANES)
                cur = raw[:, r0:r0 + LANES]
                ext_ref[sl, halo:halo + tm, :] = cur
                u = cur * cw_ref[kw - 1:kw, cs] + cb_ref[:, cs]
                for k in range(kw - 1):
                    off = halo - (kw - 1) + k
                    u = u + ext_ref[sl, off:off + tm, :] * cw_ref[k:k + 1, cs]
                o_ref[:, c0 - lo_c:c0 - lo_c + LANES] = _silu(u).astype(o_ref.dtype)


def _ssd_in(x, nw, w, wdt, cw, cb, dtb, *, di, tm=512):
    b, s, d = x.shape
    cdim = cw.shape[1]
    gn = (cdim - di) // 2
    halo = 8
    chunk = 512
    row = lambda bi, i: (bi, i, 0)
    return pl.pallas_call(
        functools.partial(_ssd_in_body, tm=tm, halo=halo, chunk=chunk),
        grid=(b, s // tm),
        in_specs=[
            pl.BlockSpec((None, tm, d), row),
            _const_spec((1, d)),
            _const_spec(w.shape),
            _const_spec(wdt.shape),
            _const_spec(cw.shape),
            _const_spec(cb.shape),
            _const_spec(dtb.shape),
        ],
        out_specs=[
            pl.BlockSpec((None, tm, di), row),
            pl.BlockSpec((None, tm, di), row),
            pl.BlockSpec((None, tm, gn), row),
            pl.BlockSpec((None, tm, gn), row),
            pl.BlockSpec((None, tm, LANES), row),
        ],
        out_shape=[
            jax.ShapeDtypeStruct((b, s, di), BF16),
            jax.ShapeDtypeStruct((b, s, di), BF16),
            jax.ShapeDtypeStruct((b, s, gn), BF16),
            jax.ShapeDtypeStruct((b, s, gn), BF16),
            jax.ShapeDtypeStruct((b, s, LANES), F32),
        ],
        scratch_shapes=[pltpu.VMEM((cdim // LANES, halo + tm, LANES), F32)],
        compiler_params=_params(("arbitrary", "arbitrary")),
        name="ssd_in",
    )(x, nw, w, wdt, cw, cb, dtb)


def _ssd_scan_body(xs_ref, b_ref, c_ref, dt_ref, alog_ref, dskip_ref, eh_ref, y_ref, st_ref,
                   *, n_heads, nc):
    ci = pl.program_id(1)
    L = SSM_CHUNK
    P, N, G = SSM_HEAD_DIM, SSM_STATE, SSM_GROUPS
    hpg = n_heads // G
    gw = hpg * P

    @pl.when(ci == 0)
    def _():
        st_ref[...] = jnp.zeros_like(st_ref)

    lane_h = lax.broadcasted_iota(jnp.int32, (1, LANES), 1)
    a = jnp.where(lane_h < n_heads, -jnp.exp(alog_ref[...]), 0.0)
    row = lax.broadcasted_iota(jnp.int32, (L, L), 0)
    col = lax.broadcasted_iota(jnp.int32, (L, L), 1)
    tril = row >= col
    tril16 = jnp.where(tril, 1.0, 0.0).astype(BF16)
    eh = eh_ref[...]
    head_of_lane = lax.broadcasted_iota(jnp.int32, (L, gw), 1) // P
    head_mask = [jnp.where(head_of_lane == k, 1.0, 0.0).astype(BF16) for k in range(hpg)]

    def prologue(c):
        rs = slice(c * L, (c + 1) * L)
        dt = dt_ref[rs, :]
        dtT = dt.T
        da = dt * a
        da_hi = da.astype(BF16)
        rest = da - da_hi.astype(F32)
        da_mid = rest.astype(BF16)
        da_lo = (rest - da_mid.astype(F32)).astype(BF16)
        parts = _dot(tril16, jnp.concatenate([da_hi, da_mid, da_lo], axis=1))
        acum = parts[:, :LANES] + parts[:, LANES:2 * LANES] + parts[:, 2 * LANES:]
        acumT = acum.T
        a_last = acum[L - 1:L, :]
        sdt_b = _dot((jnp.exp(a_last - acum) * dt).astype(BF16), eh)
        ea_b = _dot(jnp.exp(acum).astype(BF16), eh)
        dl = jnp.broadcast_to(jnp.exp(a_last), (16, LANES))
        dl_hi = dl.astype(BF16)
        dl_lo = (dl - dl_hi.astype(F32)).astype(BF16)
        decay_last = (_dot(dl_hi, eh) + _dot(dl_lo, eh))[0:1, :]
        return dtT, acum, acumT, sdt_b, ea_b, decay_last

    def main(c, pro):
        dtT, acum, acumT, sdt_b, ea_b, decay_last = pro
        rs = slice(c * L, (c + 1) * L)
        xs16 = xs_ref[rs, :]
        xs = xs16.astype(F32)
        xw = (xs * sdt_b).astype(BF16)
        cbs, y_offs = [], []
        for g in range(G):
            bg = b_ref[rs, g * N:(g + 1) * N]
            cg = c_ref[rs, g * N:(g + 1) * N]
            cbs.append(lax.dot_general(cg, bg, (((1,), (1,)), ((), ())), preferred_element_type=F32))
            gs = slice(g * gw, (g + 1) * gw)
            st = st_ref[g]
            y_offs.append(_dot(cg, st.astype(BF16)) * ea_b[:, gs])
            contrib = lax.dot_general(bg, xw[:, gs], (((0,), (0,)), ((), ())),
                                      preferred_element_type=F32)
            st_ref[g] = st * decay_last[:, gs] + contrib

        for g in range(G):
            gs = slice(g * gw, (g + 1) * gw)
            cb, y_off = cbs[g], y_offs[g]
            ws = []
            for k in range(hpg):
                hidx = g * hpg + k
                seg = acum[:, hidx:hidx + 1] - acumT[hidx:hidx + 1, :]
                decay = jnp.exp(jnp.where(tril, seg, NEG_BIG))
                ws.append((cb * decay * dtT[hidx:hidx + 1, :]).astype(BF16))
            xg = xs16[:, gs]
            xblk = jnp.concatenate([xg * head_mask[k] for k in range(hpg)], axis=0)
            y_diag = _dot(jnp.concatenate(ws, axis=1), xblk)
            y_ref[rs, gs] = (y_diag + y_off + dskip_ref[:, gs] * xs[:, gs]).astype(y_ref.dtype)

    pros = [prologue(c) for c in range(nc)]
    for c in range(nc):
        main(c, pros[c])


def _ssd_scan(xs, bm, cm, dt, alog, dskip, eh, *, n_heads, nc=4):
    b, s, di = xs.shape
    gn = bm.shape[-1]
    L = nc * SSM_CHUNK
    row = lambda bi, i: (bi, i, 0)
    return pl.pallas_call(
        functools.partial(_ssd_scan_body, n_heads=n_heads, nc=nc),
        grid=(b, s // L),
        in_specs=[
            pl.BlockSpec((None, L, di), row),
            pl.BlockSpec((None, L, gn), row),
            pl.BlockSpec((None, L, gn), row),
            pl.BlockSpec((None, L, LANES), row),
            _const_spec(alog.shape),
            _const_spec(dskip.shape),
            _const_spec(eh.shape),
        ],
        out_specs=pl.BlockSpec((None, L, di), row),
        out_shape=jax.ShapeDtypeStruct((b, s, di), BF16),
        scratch_shapes=[pltpu.VMEM((SSM_GROUPS, SSM_STATE, di // SSM_GROUPS), F32)],
        compiler_params=_params(("arbitrary", "arbitrary")),
        name="ssd_scan",
    )(xs, bm, cm, dt, alog, dskip, eh)


def _ssd_out_body(y_ref, z_ref, nw_ref, w_ref, x_ref, o_ref, yn_ref):
    di = y_ref.shape[-1]
    gw = di // SSM_GROUPS
    for g in range(SSM_GROUPS):
        gs = slice(g * gw, (g + 1) * gw)
        yg = y_ref[:, gs].astype(F32) * _silu(z_ref[:, gs].astype(F32))
        yn_ref[:, gs] = _rmsnorm(yg, nw_ref[:, gs]).astype(BF16)
    o_ref[...] = x_ref[...] + _dot(yn_ref[...], w_ref[...])


def _ssd_out(y, z, nw, w, x, *, tm=1024):
    m, di = y.shape
    d = w.shape[1]
    return pl.pallas_call(
        _ssd_out_body,
        grid=(m // tm,),
        in_specs=[
            pl.BlockSpec((tm, di), lambda i: (i, 0)),
            pl.BlockSpec((tm, di), lambda i: (i, 0)),
            _const_spec(nw.shape),
            _const_spec(w.shape),
            pl.BlockSpec((tm, d), lambda i: (i, 0)),
        ],
        out_specs=pl.BlockSpec((tm, d), lambda i: (i, 0)),
        out_shape=jax.ShapeDtypeStruct((m, d), F32),
        scratch_shapes=[pltpu.VMEM((tm, di), BF16)],
        compiler_params=_params(("arbitrary",)),
        name="ssd_out",
    )(y, z, nw, w, x)


def _pad_lanes(a, width=LANES):
    return jnp.pad(a, ((0, 0), (0, width - a.shape[-1])))


def _conv_layer(x, nw, w_in, w_dw, w_out, *, layer):
    return _conv_mixer(x, nw[None], w_in, w_dw, w_out, layer=layer)


def _fox_layer(x, nw, w_in, b_f, q_gain, k_gain, w_out, *, t=256):
    b, s, d = x.shape
    n_heads = b_f.shape[0]
    hd = ATTN_HEAD_DIM
    aw = n_heads * hd
    perm = jnp.argsort(b_f)

    def by_head(w):
        return w.reshape(d, n_heads, hd)[:, perm].reshape(d, aw)

    wqkv = jnp.concatenate([by_head(w_in[:, j * aw:(j + 1) * aw]) for j in range(3)],
                           axis=1).astype(BF16)
    wf = _pad_lanes(w_in[:, 3 * aw:][:, perm]).astype(BF16)
    bf = _pad_lanes(b_f[perm][None])
    w_out = w_out.reshape(n_heads, hd, d)[perm].reshape(aw, d)
    qg = jnp.tile(q_gain, n_heads)[None] * (hd ** -0.5 * LOG2E)
    kg = jnp.tile(k_gain, n_heads)[None]
    smax = 1.02 * LOG2E * hd ** 0.5 * jnp.max(jnp.abs(q_gain)) * jnp.max(jnp.abs(k_gain))
    fast = (2.0 * smax <= FAST_PATH_MAX_LOG2).astype(F32)
    thr = jnp.stack([-(SKIP_LOG2 + 2.0 * smax), fast, smax]).astype(F32)
    head_of = jnp.arange(aw) // hd
    e = (head_of[:, None] == jnp.arange(LANES)[None, :]).astype(F32)
    et = jnp.concatenate([e.T, e.T], axis=0).astype(BF16)
    e = (e / hd).astype(BF16)
    q, k, v, ct = _fox_in(x, nw[None], wqkv, wf, bf, qg, kg, e, et, n_heads=n_heads)
    ck = ct.reshape(b, n_heads // 2, 2, s // t, t).transpose(0, 1, 3, 2, 4)
    cs = ct[:, :, 0::t].reshape(b * n_heads, s // t)
    ce = ct[:, :, t - 1::t].reshape(b * n_heads, s // t)
    attn = _fox_attn(thr, cs, ce, q, k, v, ck, t=t, n_heads=n_heads)
    return attn.reshape(b * s, aw), w_out.astype(BF16)


def _ssd_layer(x, nw, w_in, conv_w, conv_b, dt_bias, a_log, d_skip, norm_w, w_out):
    b, s, d = x.shape
    n_heads = a_log.shape[0]
    di = n_heads * SSM_HEAD_DIM
    cdim = conv_w.shape[1]
    wdt = _pad_lanes(w_in[:, di + cdim:]).astype(BF16)
    z, xs, bm, cm, dt = _ssd_in(x, nw[None], w_in, wdt, conv_w, conv_b[None],
                                _pad_lanes(dt_bias[None]), di=di)
    head_of = jnp.arange(di) // SSM_HEAD_DIM
    eh = (jnp.arange(LANES)[:, None] == head_of[None, :]).astype(BF16)
    dskip = jnp.repeat(d_skip, SSM_HEAD_DIM)[None]
    y = _ssd_scan(xs, bm, cm, dt, _pad_lanes(a_log[None]), dskip, eh, n_heads=n_heads)
    m = b * s
    return _ssd_out(y.reshape(m, di), z.reshape(m, di), norm_w[None], w_out.astype(BF16),
                    x.reshape(m, d)).reshape(b, s, d)


def kernel(x, mix_norm, ffn_norm, ffn_w_gu, ffn_w_down, conv_w_in, conv_w_dw, conv_w_out, fox_w_in, fox_b_f, fox_q_gain, fox_k_gain, fox_w_out, ssd_w_in, ssd_conv_w, ssd_conv_b, ssd_dt_bias, ssd_a_log, ssd_d, ssd_norm_w, ssd_w_out):
    b, s, d = x.shape
    depth = mix_norm.shape[0]
    for i in range(depth):
        kind, j = i % 3, i // 3
        proj = None
        if kind == 0:
            x = _conv_layer(x, mix_norm[i], conv_w_in, conv_w_dw[j], conv_w_out, layer=j)
        elif kind == 1:
            proj = _fox_layer(x, mix_norm[i], fox_w_in[j], fox_b_f[j], fox_q_gain[j], fox_k_gain[j],
                              fox_w_out[j])
        else:
            x = _ssd_layer(x, mix_norm[i], ssd_w_in[j], ssd_conv_w[j], ssd_conv_b[j],
                           ssd_dt_bias[j], ssd_a_log[j], ssd_d[j], ssd_norm_w[j], ssd_w_out[j])
        x = _ffn(x.reshape(b * s, d), ffn_norm[i][None], ffn_w_gu, ffn_w_down,
                 layer=i, proj=proj).reshape(b, s, d)
    return x
```
